```python
import jax
import jax.numpy as jnp
from jax import lax
import numpy as np

D_MODEL = 2048
BATCH = 4
SEQ = 4096
DEPTH = 2

GRID_W = 64
CTX_LEN = 256
D_MIX = D_MODEL
D_A = D_MIX // 2
NA_HEAD_DIM = 64
NA_HEADS = D_A // NA_HEAD_DIM
NA_KH_MAX = 8
NA_KW = 16
D_B = D_MIX // 4
GLA_HEADS = 4
GLA_DV = D_B // GLA_HEADS
GLA_DK = GLA_DV // 2
DK_B = GLA_HEADS * GLA_DK
GATE_RANK = 16
GATE_TAU = 16.0
GLA_CHUNK = 64
D_C = D_MIX - D_A - D_B
CONV_K = 31

ROPE_BASE = 10000.0
LN_EPS = 1e-5
RMS_EPS = 1e-6
DEEPNORM_ALPHA = (2 * DEPTH) ** 0.25
DEEPNORM_BETA = (8 * DEPTH) ** -0.25
SPLIT_SIZES = (D_A, D_A, D_A, D_A, DK_B, DK_B, D_B, D_B, 2 * GATE_RANK, D_C, D_C, D_C)
N_IN = 4 * D_A + 2 * DK_B + 2 * D_B + 2 * GATE_RANK + 3 * D_C

kernel_name = 'hybrid_na_gla_conformer_dit'


def _layer_norm(x):
    xf = x.astype(jnp.float32)
    mu = jnp.mean(xf, axis=-1, keepdims=True)
    var = jnp.mean(jnp.square(xf - mu), axis=-1, keepdims=True)
    return ((xf - mu) * lax.rsqrt(var + LN_EPS)).astype(x.dtype)


def _rms_norm(x, g):
    xf = x.astype(jnp.float32)
    y = xf * lax.rsqrt(jnp.mean(jnp.square(xf), axis=-1, keepdims=True) + RMS_EPS)
    return y.astype(x.dtype) * g


def _split_proj(p):
    idx, acc = [], 0
    for s in SPLIT_SIZES[:-1]:
        acc += s
        idx.append(acc)
    return jnp.split(p, idx, axis=-1)


def _rope_1d(x, pos):
    d = x.shape[-1]
    inv = ROPE_BASE ** (-jnp.arange(0, d, 2, dtype=jnp.float32) / d)
    ang = pos[:, None] * inv[None, :]
    cos = jnp.cos(ang)[None, :, None, :].astype(x.dtype)
    sin = jnp.sin(ang)[None, :, None, :].astype(x.dtype)
    x1, x2 = jnp.split(x, 2, axis=-1)
    return jnp.concatenate([x1 * cos - x2 * sin, x1 * sin + x2 * cos], axis=-1)


def _axial_rope(x, rows, cols):
    xr, xc = jnp.split(x, 2, axis=-1)
    return jnp.concatenate([_rope_1d(xr, rows), _rope_1d(xc, cols)], axis=-1)


def _neighbourhood_attention(q, k, v, kc, vc, rpb):
    B, L, H, Dh = q.shape
    rows = L // GRID_W
    kh = min(NA_KH_MAX, rows)
    qg = q.reshape(B, rows, GRID_W, H, Dh)
    kg = k.reshape(B, rows, GRID_W, H, Dh)
    vg = v.reshape(B, rows, GRID_W, H, Dh)
    col = jnp.arange(GRID_W)
    col_start = jnp.clip(col - NA_KW // 2, 0, GRID_W - NA_KW)
    col_idx = col_start[:, None] + jnp.arange(NA_KW)[None, :]
    dcol = col_idx - col[:, None] + (NA_KW - 1)
    scale = Dh ** -0.5
    n_loc = kh * NA_KW

    def row_block(r):
        rs = jnp.clip(r - kh // 2, 0, rows - kh)
        q_r = lax.dynamic_index_in_dim(qg, r, axis=1, keepdims=False)
        k_rows = lax.dynamic_slice_in_dim(kg, rs, kh, axis=1)
        v_rows = lax.dynamic_slice_in_dim(vg, rs, kh, axis=1)
        k_win = k_rows[:, :, col_idx]
        v_win = v_rows[:, :, col_idx]
        drow = rs + jnp.arange(kh) - r + (NA_KH_MAX - 1)
        bias = rpb[:, drow[:, None, None], dcol[None, :, :]]
        s_loc = jnp.einsum('bqhd,biqjhd->bhqij', q_r, k_win) * scale + jnp.transpose(bias, (0, 2, 1, 3))[None]
        s_loc = s_loc.reshape(B, H, GRID_W, n_loc)
        s_ctx = jnp.einsum('bqhd,bkhd->bhqk', q_r, kc) * scale
        p = jax.nn.softmax(jnp.concatenate([s_loc, s_ctx], axis=-1).astype(jnp.float32), axis=-1).astype(v.dtype)
        p_loc = p[..., :n_loc].reshape(B, H, GRID_W, kh, NA_KW)
        p_ctx = p[..., n_loc:]
        return (jnp.einsum('bhqij,biqjhd->bqhd', p_loc, v_win)
                + jnp.einsum('bhqk,bkhd->bqhd', p_ctx, vc))

    out = lax.map(row_block, jnp.arange(rows))
    return jnp.moveaxis(out, 0, 1).reshape(B, L, H * Dh)


def _context_attention(q, k, v):
    B, Lc, H, Dh = q.shape
    s = jnp.einsum('bqhd,bkhd->bhqk', q, k) * Dh ** -0.5
    p = jax.nn.softmax(s.astype(jnp.float32), axis=-1).astype(v.dtype)
    return jnp.einsum('bhqk,bkhd->bqhd', p, v).reshape(B, Lc, H * Dh)


def _to_chunks(t):
    B, L, H, D = t.shape
    return t.reshape(B, L // GLA_CHUNK, GLA_CHUNK, H, D).transpose(0, 1, 3, 2, 4)


def _from_chunks(t):
    B, N, H, C, D = t.shape
    return t.transpose(0, 1, 3, 2, 4).reshape(B, N * C, H, D)


def _gla_chunk_terms(kc, vc, gc):
    b = jnp.cumsum(gc, axis=3)
    b_last = b[:, :, :, -1:, :]
    kv = jnp.einsum('bnhck,bnhcv->bnhkv', kc * jnp.exp(b_last - b), vc)
    decay = jnp.exp(b_last[:, :, :, 0, :])
    return b, kv, decay


def _gla_states(kv, decay, s0):
    def step(s, inp):
        kv_n, d_n = inp
        return d_n[..., None] * s + kv_n, s
    s_fin, starts = lax.scan(step, s0, (jnp.moveaxis(kv, 1, 0), jnp.moveaxis(decay, 1, 0)))
    return jnp.moveaxis(starts, 0, 1), s_fin


def _gla_forward(q, k, v, g, s0):
    dt = v.dtype
    qc = _to_chunks(q.astype(jnp.float32))
    kc = _to_chunks(k.astype(jnp.float32))
    vc = _to_chunks(v.astype(jnp.float32))
    gc = _to_chunks(g)
    b, kv, decay = _gla_chunk_terms(kc, vc, gc)
    starts, s_fin = _gla_states(kv, decay, s0)
    q_t = qc * jnp.exp(b)
    k_t = kc * jnp.exp(-b)
    mask = jnp.tril(jnp.ones((GLA_CHUNK, GLA_CHUNK), dtype=bool))
    att = jnp.where(mask, jnp.einsum('bnhik,bnhjk->bnhij', q_t, k_t), 0.0)
    o = jnp.einsum('bnhck,bnhkv->bnhcv', q_t, starts) + jnp.einsum('bnhij,bnhjv->bnhiv', att, vc)
    return _from_chunks(o).astype(dt), s_fin


def _gla_final_state(k, v, g, s0):
    kc = _to_chunks(k.astype(jnp.float32))
    vc = _to_chunks(v.astype(jnp.float32))
    _, kv, decay = _gla_chunk_terms(kc, vc, _to_chunks(g))
    _, s_fin = _gla_states(kv, decay, s0)
    return s_fin


def _gla_log_decay(glr, w2, b):
    B, L, _ = glr.shape
    lr = glr.astype(jnp.float32).reshape(B, L, 2, GATE_RANK)
    logits = jnp.einsum('bldr,drk->bldk', lr, w2.astype(jnp.float32)) + b.astype(jnp.float32)
    g = (jax.nn.log_sigmoid(logits) / GATE_TAU).reshape(B, L, 2, GLA_HEADS, GLA_DK)
    return g[:, :, 0], g[:, :, 1]


def _conformer_conv(a, gt, w, bconv, ln_g, ln_b):
    u = a * jax.nn.sigmoid(gt)
    u = lax.conv_general_dilated(u, w[:, None, :].astype(u.dtype), window_strides=(1,),
                                 padding=[(CONV_K // 2, CONV_K // 2)],
                                 dimension_numbers=('NWC', 'WIO', 'NWC'),
                                 feature_group_count=u.shape[-1]) + bconv
    return jax.nn.silu(_layer_norm(u) * ln_g + ln_b)


def _layer(x, cx, c, c_ctx, w_ada, b_ada, w_in, rpb, gla_w2, gla_b, gla_norm,
           conv_w, conv_b, conv_ln_g, conv_ln_b, w_out, post_ln_g, post_ln_b, last):
    B, L, _ = x.shape
    Lc = cx.shape[1]
    shift, scale, gate = jnp.split(jax.nn.silu(c) @ w_ada + b_ada, 3, axis=-1)
    shift_c, scale_c, gate_c = jnp.split(jax.nn.silu(c_ctx) @ w_ada + b_ada, 3, axis=-1)
    h = _layer_norm(x) * (1 + scale[:, None]) + shift[:, None]
    hc = _layer_norm(cx) * (1 + scale_c) + shift_c
    qa, ka, va, za, qb, kb, vb, zb, glr, ca, cg, zc = _split_proj(h @ w_in)
    qa_c, ka_c, va_c, za_c, qb_c, kb_c, vb_c, zb_c, glr_c, ca_c, cg_c, zc_c = _split_proj(hc @ w_in)

    def heads_a(t):
        return t.reshape(t.shape[0], t.shape[1], NA_HEADS, NA_HEAD_DIM)

    def heads_k(t):
        return t.reshape(t.shape[0], t.shape[1], GLA_HEADS, GLA_DK)

    def heads_v(t):
        return t.reshape(t.shape[0], t.shape[1], GLA_HEADS, GLA_DV)

    kca, vca = heads_a(ka_c), heads_a(va_c)
    out_a = _neighbourhood_attention(heads_a(qa), heads_a(ka), heads_a(va), kca, vca, rpb) * jax.nn.silu(za)

    pos = jnp.arange(L)
    rows_pos = (pos // GRID_W).astype(jnp.float32)
    cols_pos = (pos % GRID_W).astype(jnp.float32)
    q_scale = GLA_DK ** -0.5
    qbh = _axial_rope(heads_k(qb), rows_pos, cols_pos) * q_scale
    kbh = _axial_rope(heads_k(kb), rows_pos, cols_pos)
    vbh = heads_v(vb)
    g_f, g_b = _gla_log_decay(glr, gla_w2, gla_b)
    kbc, vbc = heads_k(kb_c), heads_v(vb_c)
    gc_f, gc_b = _gla_log_decay(glr_c, gla_w2, gla_b)
    s0 = jnp.zeros((B, GLA_HEADS, GLA_DK, GLA_DV), jnp.float32)
    flip = lambda t: jnp.flip(t, axis=1)
    if last:
        s_f = _gla_final_state(kbc, vbc, gc_f, s0)
        s_b = _gla_final_state(flip(kbc), flip(vbc), flip(gc_b), s0)
    else:
        qbc = heads_k(qb_c) * q_scale
        oc_f, s_f = _gla_forward(qbc, kbc, vbc, gc_f, s0)
        oc_b, s_b = _gla_forward(flip(qbc), flip(kbc), flip(vbc), flip(gc_b), s0)
        oc = oc_f + flip(oc_b)
    o_f, _ = _gla_forward(qbh, kbh, vbh, g_f, s_f)
    o_b, _ = _gla_forward(flip(qbh), flip(kbh), flip(vbh), flip(g_b), s_b)
    out_b = _rms_norm(o_f + flip(o_b), gla_norm).reshape(B, L, D_B) * jax.nn.silu(zb)

    out_c = _conformer_conv(ca, cg, conv_w, conv_b, conv_ln_g, conv_ln_b) * jax.nn.silu(zc)

    y = jnp.concatenate([out_a, out_b, out_c], axis=-1) @ w_out
    x_new = _layer_norm(DEEPNORM_ALPHA * x + gate[:, None] * y) * post_ln_g + post_ln_b
    if last:
        return x_new, None

    out_a_c = _context_attention(heads_a(qa_c), kca, vca) * jax.nn.silu(za_c)
    out_b_c = _rms_norm(oc, gla_norm).reshape(B, Lc, D_B) * jax.nn.silu(zb_c)
    out_c_c = _conformer_conv(ca_c, cg_c, conv_w, conv_b, conv_ln_g, conv_ln_b) * jax.nn.silu(zc_c)
    yc = jnp.concatenate([out_a_c, out_b_c, out_c_c], axis=-1) @ w_out
    cx_new = _layer_norm(DEEPNORM_ALPHA * cx + gate_c * yc) * post_ln_g + post_ln_b
    return x_new, cx_new


def setup_inputs(seed: int = 0) -> dict:
    key = jax.random.key(seed)
    ks = jax.random.split(key, 20)
    nrm = jax.random.normal
    D = D_MODEL
    return {
        'x': nrm(ks[0], (BATCH, SEQ, D), jnp.float32),
        'c': nrm(ks[1], (BATCH, D), jnp.float32),
        'ctx': nrm(ks[2], (BATCH, CTX_LEN, D), jnp.float32),
        'c_ctx': nrm(ks[3], (D,), jnp.float32),
        'w_ada': nrm(ks[4], (DEPTH, D, 3 * D), jnp.float32) * (0.5 * D ** -0.5),
        'b_ada': nrm(ks[5], (DEPTH, 3 * D), jnp.float32) * 0.02,
        'w_in': nrm(ks[6], (DEPTH, D, N_IN), jnp.float32) * D ** -0.5,
        'rpb': nrm(ks[7], (DEPTH, NA_HEADS, 2 * NA_KH_MAX - 1, 2 * NA_KW - 1), jnp.float32) * 0.02,
        'gla_w2': nrm(ks[8], (DEPTH, 2, GATE_RANK, DK_B), jnp.float32) * GATE_RANK ** -0.5,
        'gla_b': nrm(ks[9], (DEPTH, 2, DK_B), jnp.float32) * 0.1,
        'gla_norm': 1.0 + 0.01 * nrm(ks[10], (DEPTH, GLA_DV), jnp.float32),
        'conv_w': nrm(ks[11], (DEPTH, CONV_K, D_C), jnp.float32) * CONV_K ** -0.5,
        'conv_b': nrm(ks[12], (DEPTH, D_C), jnp.float32) * 0.01,
        'conv_ln_g': 1.0 + 0.01 * nrm(ks[13], (DEPTH, D_C), jnp.float32),
        'conv_ln_b': nrm(ks[14], (DEPTH, D_C), jnp.float32) * 0.01,
        'w_out': nrm(ks[15], (DEPTH, D_MIX, D), jnp.float32) * (D_MIX ** -0.5 * DEEPNORM_BETA),
        'post_ln_g': 1.0 + 0.01 * nrm(ks[16], (DEPTH, D), jnp.float32),
        'post_ln_b': nrm(ks[17], (DEPTH, D), jnp.float32) * 0.01,
    }


def reference(x, c, ctx, c_ctx, w_ada, b_ada, w_in, rpb, gla_w2, gla_b, gla_norm,
              conv_w, conv_b, conv_ln_g, conv_ln_b, w_out, post_ln_g, post_ln_b):
    cx = ctx
    for l in range(DEPTH):
        x, cx = _layer(x, cx, c, c_ctx, w_ada[l], b_ada[l], w_in[l], rpb[l], gla_w2[l], gla_b[l],
                       gla_norm[l], conv_w[l], conv_b[l], conv_ln_g[l], conv_ln_b[l], w_out[l],
                       post_ln_g[l], post_ln_b[l], last=(l == DEPTH - 1))
    return x
```

```python
import functools

import numpy as np
import jax
import jax.numpy as jnp
from jax import lax
from jax.experimental import pallas as pl
from jax.experimental.pallas import tpu as pltpu

F32 = jnp.float32
BF16 = jnp.bfloat16

GRID_W = 64
NA_HEAD_DIM = 64
NA_KH_MAX = 8
NA_KW = 16
GLA_HEADS = 4
GATE_RANK = 16
GATE_TAU = 16.0
GLA_CHUNK = 64
CONV_K = 31
ROPE_BASE = 10000.0
LN_EPS = 1e-5
RMS_EPS = 1e-6

LANES = 128
VMEM_LIMIT_BYTES = 56 * 1024 * 1024

NEG_BIG = -1e30

PROJ_TM = 1024
OUT_TM = 256
LN_ROWS = 32
ATT_TQ_ROWS = 4
ATT_WIN_ROWS = 12
GLA_TG = 256
CONV_T = 512
CONV_HALO = 16
CONV_RC = 64


def _silu(x):
    return x * jax.nn.sigmoid(x)


def _dot(a, b):
    return jnp.dot(a, b, preferred_element_type=F32)


def _dot_nt(a, b):
    return lax.dot_general(a, b, (((1,), (1,)), ((), ())), preferred_element_type=F32)


def _dot_tn(a, b):
    return lax.dot_general(a, b, (((0,), (0,)), ((), ())), preferred_element_type=F32)


def _shr(x, pow2):
    shift = pow2.bit_length() - 1
    assert 1 << shift == pow2
    return jnp.right_shift(x, shift)


def _split_hi_lo(x):
    hi = x.astype(BF16)
    lo = (x - hi.astype(F32)).astype(BF16)
    return hi, lo


def _params(*sem):
    return pltpu.CompilerParams(dimension_semantics=sem, vmem_limit_bytes=VMEM_LIMIT_BYTES)


def _ada_kernel(c_ref, w_ref, b_ref, o_ref):
    s = _silu(c_ref[...]).astype(BF16)
    o_ref[0] = _dot(s, w_ref[0].astype(BF16)) + b_ref[0]


def _ada_call(cond, w_ada, b_ada, tn=512):
    depth, d, n = w_ada.shape
    rows = cond.shape[0]
    return pl.pallas_call(
        _ada_kernel,
        grid=(depth, n // tn),
        in_specs=[
            pl.BlockSpec((rows, d), lambda l, j: (0, 0)),
            pl.BlockSpec((1, d, tn), lambda l, j: (l, 0, j)),
            pl.BlockSpec((1, 1, tn), lambda l, j: (l, 0, j)),
        ],
        out_specs=pl.BlockSpec((1, rows, tn), lambda l, j: (l, 0, j)),
        out_shape=jax.ShapeDtypeStruct((depth, rows, n), F32),
        compiler_params=_params("parallel", "parallel"),
        name="ada_mod",
    )(cond, w_ada, b_ada.reshape(depth, 1, n))


def _proj_kernel(x_ref, mod_ref, w_ref, o_ref, h_ref):
    tm = x_ref.shape[0]

    @pl.when(pl.program_id(1) == 0)
    def _():
        shift = mod_ref[0, 0:1, :]
        scale1 = 1.0 + mod_ref[0, 1:2, :]

        def body(i, carry):
            r0 = pl.multiple_of(i * LN_ROWS, LN_ROWS)
            xv = x_ref[pl.ds(r0, LN_ROWS), :]
            mu = jnp.mean(xv, axis=-1, keepdims=True)
            xc = xv - mu
            var = jnp.mean(xc * xc, axis=-1, keepdims=True)
            hn = xc * lax.rsqrt(var + LN_EPS)
            h_ref[pl.ds(r0, LN_ROWS), :] = (hn * scale1 + shift).astype(BF16)
            return carry

        lax.fori_loop(0, tm // LN_ROWS, body, 0)

    o_ref[...] = _dot(h_ref[...], w_ref[...]).astype(o_ref.dtype)


def _proj_call(x2, mod, w, rows_per_mod, tn, out_dtype, name):
    m, d = x2.shape
    n = w.shape[1]
    tm = min(PROJ_TM, m)
    per = rows_per_mod // tm
    return pl.pallas_call(
        _proj_kernel,
        grid=(m // tm, n // tn),
        in_specs=[
            pl.BlockSpec((tm, d), lambda i, j: (i, 0)),
            pl.BlockSpec((1, 2, d), lambda i, j: (i // per, 0, 0)),
            pl.BlockSpec((d, tn), lambda i, j: (0, j)),
        ],
        out_specs=pl.BlockSpec((tm, tn), lambda i, j: (i, j)),
        out_shape=jax.ShapeDtypeStruct((m, n), out_dtype),
        scratch_shapes=[pltpu.VMEM((tm, d), BF16)],
        compiler_params=_params("parallel", "arbitrary"),
        name=name,
    )(x2, mod, w)


def _softmax_pv(q, lane_lo, pieces):
    outs = []
    for h in range(2):
        keep = lane_lo if h == 0 else jnp.logical_not(lane_lo)
        qh = jnp.where(keep, q, jnp.zeros_like(q)) * jnp.asarray(NA_HEAD_DIM ** -0.5, BF16)
        scores = []
        for k, _, bias in pieces:
            s = _dot_nt(qh, k)
            if bias is not None:
                s = s + bias[h]
            scores.append(s)
        m = scores[0].max(axis=-1, keepdims=True)
        for s in scores[1:]:
            m = jnp.maximum(m, s.max(axis=-1, keepdims=True))
        den = None
        acc = None
        for s, (_, v, _) in zip(scores, pieces):
            p = jnp.exp(s - m)
            ps = p.sum(axis=-1, keepdims=True)
            pv = _dot(p.astype(BF16), v)
            den = ps if den is None else den + ps
            acc = pv if acc is None else acc + pv
        outs.append(acc / den)
    return jnp.where(lane_lo, outs[0], outs[1])


def _na_kernel(q_ref, k_ref, v_ref, z_ref, kc_ref, vc_ref, bias_ref, o_ref, *, rows):
    tq = q_ref.shape[1]
    nwin = ATT_WIN_ROWS * GRID_W
    qi = pl.program_id(2)
    w0 = jnp.clip(qi * ATT_TQ_ROWS - NA_KH_MAX // 2, 0, rows - ATT_WIN_ROWS)
    start = pl.multiple_of(w0 * GRID_W, GRID_W)
    kw = k_ref[0, pl.ds(start, nwin), :]
    vw = v_ref[0, pl.ds(start, nwin), :]
    lane_lo = lax.broadcasted_iota(jnp.int32, (tq, LANES), 1) < NA_HEAD_DIM
    bias = (bias_ref[0, 0, 0], bias_ref[0, 0, 1])
    o = _softmax_pv(q_ref[0], lane_lo, [(kw, vw, bias), (kc_ref[0], vc_ref[0], None)])
    z = z_ref[0].astype(F32)
    o_ref[0] = (o * _silu(z)).astype(o_ref.dtype)


def _na_bias_table(rpb, rows):
    heads = rpb.shape[0]
    kh = min(NA_KH_MAX, rows)
    nq = rows // ATT_TQ_ROWS

    def geometry(qi):
        r0 = qi * ATT_TQ_ROWS
        w0 = int(np.clip(r0 - NA_KH_MAX // 2, 0, rows - ATT_WIN_ROWS))
        r = r0 + np.arange(ATT_TQ_ROWS)[:, None, None, None]
        c = np.arange(GRID_W)[None, :, None, None]
        kr = w0 + np.arange(ATT_WIN_ROWS)[None, None, :, None]
        kc = np.arange(GRID_W)[None, None, None, :]
        rs = np.clip(r - kh // 2, 0, rows - kh)
        cs = np.clip(c - NA_KW // 2, 0, GRID_W - NA_KW)
        valid = (kr >= rs) & (kr < rs + kh) & (kc >= cs) & (kc < cs + NA_KW)
        drow = np.broadcast_to(kr - r + (NA_KH_MAX - 1), valid.shape)
        dcol = np.broadcast_to(kc - c + (NA_KW - 1), valid.shape)
        shape = (ATT_TQ_ROWS * GRID_W, ATT_WIN_ROWS * GRID_W)
        drow = np.where(valid, drow, 0).reshape(shape)
        dcol = np.where(valid, dcol, 0).reshape(shape)
        return valid.reshape(shape), drow, dcol

    reps = [0, 1, nq - 1]
    geo = [geometry(qi) for qi in reps]
    for qi in range(1, nq - 1):
        g = geometry(qi)
        assert all(np.array_equal(a, b) for a, b in zip(g, geo[1]))
    valid = np.stack([g[0] for g in geo])
    drow = np.stack([g[1] for g in geo])
    dcol = np.stack([g[2] for g in geo])
    table = jnp.where(valid[None], rpb[:, drow, dcol], NEG_BIG)
    table = table.reshape(heads // 2, 2, 3, *table.shape[2:])
    return jnp.transpose(table, (2, 0, 1, 3, 4))


def _na_call(p_lat, p_ctx, rpb, d_a):
    b, l, _ = p_lat.shape
    lc = p_ctx.shape[1]
    rows = l // GRID_W
    assert rows % ATT_TQ_ROWS == 0 and rows >= ATT_WIN_ROWS
    tq = ATT_TQ_ROWS * GRID_W
    nq = l // tq
    nwin = ATT_WIN_ROWS * GRID_W
    npair = d_a // LANES
    bias = _na_bias_table(rpb, rows)

    def variant(qi):
        return jnp.where(qi == 0, 0, jnp.where(qi == nq - 1, 2, 1))

    return pl.pallas_call(
        functools.partial(_na_kernel, rows=rows),
        grid=(b, npair, nq),
        in_specs=[
            pl.BlockSpec((1, tq, LANES), lambda i, p, q: (i, q, p)),
            pl.BlockSpec((1, l, LANES), lambda i, p, q: (i, 0, npair + p)),
            pl.BlockSpec((1, l, LANES), lambda i, p, q: (i, 0, 2 * npair + p)),
            pl.BlockSpec((1, tq, LANES), lambda i, p, q: (i, q, 3 * npair + p)),
            pl.BlockSpec((1, lc, LANES), lambda i, p, q: (i, 0, npair + p)),
            pl.BlockSpec((1, lc, LANES), lambda i, p, q: (i, 0, 2 * npair + p)),
            pl.BlockSpec((1, 1, 2, tq, nwin), lambda i, p, q: (variant(q), p, 0, 0, 0)),
        ],
        out_specs=pl.BlockSpec((1, tq, LANES), lambda i, p, q: (i, q, p)),
        out_shape=jax.ShapeDtypeStruct((b, l, d_a), BF16),
        compiler_params=_params("parallel", "parallel", "arbitrary"),
        name="na_attention",
    )(p_lat, p_lat, p_lat, p_lat, p_ctx, p_ctx, bias)


def _ctx_attn_kernel(q_ref, k_ref, v_ref, z_ref, o_ref):
    tq = q_ref.shape[1]
    lane_lo = lax.broadcasted_iota(jnp.int32, (tq, LANES), 1) < NA_HEAD_DIM
    o = _softmax_pv(q_ref[0], lane_lo, [(k_ref[0], v_ref[0], None)])
    o_ref[0] = (o * _silu(z_ref[0].astype(F32))).astype(o_ref.dtype)


def _ctx_attn_call(p_ctx, d_a):
    b, lc, _ = p_ctx.shape
    npair = d_a // LANES
    return pl.pallas_call(
        _ctx_attn_kernel,
        grid=(b, npair),
        in_specs=[
            pl.BlockSpec((1, lc, LANES), lambda i, p: (i, 0, p)),
            pl.BlockSpec((1, lc, LANES), lambda i, p: (i, 0, npair + p)),
            pl.BlockSpec((1, lc, LANES), lambda i, p: (i, 0, 2 * npair + p)),
            pl.BlockSpec((1, lc, LANES), lambda i, p: (i, 0, 3 * npair + p)),
        ],
        out_specs=pl.BlockSpec((1, lc, LANES), lambda i, p: (i, 0, p)),
        out_shape=jax.ShapeDtypeStruct((b, lc, d_a), BF16),
        compiler_params=_params("parallel", "parallel"),
        name="ctx_attention",
    )(p_ctx, p_ctx, p_ctx, p_ctx)


def _rope(x, cos, sin_signed):
    halves = []
    for j in range(x.shape[1] // LANES):
        xs = x[:, j * LANES:(j + 1) * LANES]
        lane = lax.broadcasted_iota(jnp.int32, xs.shape, 1)
        first = (lane & 16) == 0
        partner = jnp.where(first, pltpu.roll(xs, LANES - 16, 1), pltpu.roll(xs, 16, 1))
        halves.append(partner)
    partner = jnp.concatenate(halves, axis=1)
    return x * cos + partner * sin_signed


def _gla_kernel(q_ref, k_ref, v_ref, lr_ref, cos_ref, sin_ref, w2_ref, b2_ref, s0_ref,
                o_ref, sfin_ref, s_ref, qt_ref, kt_ref, kd_ref, g_ref, *, reverse, rope):
    tg = q_ref.shape[1]
    dkh = q_ref.shape[2]
    dvh = v_ref.shape[2]
    dk = dkh // GLA_HEADS
    dv = dvh // GLA_HEADS
    nchunk = tg // GLA_CHUNK
    step = pl.program_id(1)

    @pl.when(step == 0)
    def _():
        s_ref[...] = s0_ref[0]

    lr_hi, lr_lo = _split_hi_lo(lr_ref[0])
    w_hi, w_lo = _split_hi_lo(w2_ref[...])
    logits = _dot(lr_hi, w_hi) + _dot(lr_lo, w_hi) + _dot(lr_hi, w_lo) + b2_ref[...]
    g = (jnp.minimum(logits, 0.0) - jnp.log1p(jnp.exp(-jnp.abs(logits)))) * (1.0 / GATE_TAU)

    r = lax.broadcasted_iota(jnp.int32, (tg, tg), 0)
    c = lax.broadcasted_iota(jnp.int32, (tg, tg), 1)
    same = _shr(r, GLA_CHUNK) == _shr(c, GLA_CHUNK)
    upto = (c >= r) if reverse else (c <= r)
    t_inc = jnp.where(same & upto, 1.0, 0.0).astype(BF16)
    t_rest = jnp.where(same & jnp.logical_not(upto), 1.0, 0.0).astype(BF16)
    g_hi, g_lo = _split_hi_lo(g)
    b_inc = _dot(t_inc, g_hi) + _dot(t_inc, g_lo)
    b_rest = _dot(t_rest, g_hi) + _dot(t_rest, g_lo)

    q = q_ref[0]
    k = k_ref[0]
    if rope:
        q = _rope(q, cos_ref[...], sin_ref[...])
        k = _rope(k, cos_ref[...], sin_ref[...])
    q = q * (dk ** -0.5)
    qt_ref[...] = (q * jnp.exp(b_inc)).astype(BF16)
    kt_ref[...] = (k * jnp.exp(-b_inc)).astype(BF16)
    kd_ref[...] = k * jnp.exp(b_rest)
    g_ref[...] = g

    rk = _shr(lax.broadcasted_iota(jnp.int32, (dkh, dkh), 0), dk)
    ck = _shr(lax.broadcasted_iota(jnp.int32, (dkh, dkh), 1), dk)
    mask_k = rk == ck
    rv = _shr(lax.broadcasted_iota(jnp.int32, (dkh, dvh), 0), dk)
    cv = _shr(lax.broadcasted_iota(jnp.int32, (dkh, dvh), 1), dv)
    mask_v = rv == cv
    ti = lax.broadcasted_iota(jnp.int32, (GLA_CHUNK, dkh), 0)
    tj = lax.broadcasted_iota(jnp.int32, (GLA_CHUNK, dkh), 1) & (GLA_CHUNK - 1)
    causal = (tj >= ti) if reverse else (tj <= ti)
    ones = jnp.ones((GLA_CHUNK, LANES), BF16)

    order = range(nchunk - 1, -1, -1) if reverse else range(nchunk)
    for ci in order:
        sl = slice(ci * GLA_CHUNK, (ci + 1) * GLA_CHUNK)
        qt = qt_ref[sl, :]
        kt = kt_ref[sl, :]
        vb = v_ref[0, sl, :].astype(BF16)
        state = s_ref[...]
        o_inter = _dot(qt, state.astype(BF16))
        k_blk = jnp.where(mask_k, jnp.concatenate([kt] * GLA_HEADS, axis=0), jnp.zeros((), BF16))
        att = jnp.where(causal, _dot_nt(qt, k_blk), 0.0)
        v_blk = jnp.where(mask_v, jnp.concatenate([vb] * GLA_HEADS, axis=0), jnp.zeros((), BF16))
        o_ref[0, sl, :] = o_inter + _dot(att.astype(BF16), v_blk)
        kv = jnp.where(mask_v, _dot_tn(kd_ref[sl, :].astype(BF16), vb), 0.0)
        gh, gl = _split_hi_lo(g_ref[sl, :])
        decay = jnp.exp(_dot_tn(gh, ones) + _dot_tn(gl, ones))
        s_ref[...] = jnp.concatenate([decay] * (dvh // LANES), axis=1) * state + kv

    @pl.when(step == pl.num_programs(1) - 1)
    def _():
        sfin_ref[0] = s_ref[...]


def _gla_call(p_b, cos, sin_signed, w2p, b2, s0, *, reverse, rope, dkh, dvh, name):
    b, t, _ = p_b.shape
    tg = min(GLA_TG, t)
    ng = t // tg
    lr_blk = (2 * dkh + 2 * dvh) // LANES

    def blk(i):
        return ng - 1 - i if reverse else i

    kern = functools.partial(_gla_kernel, reverse=reverse, rope=rope)
    return pl.pallas_call(
        kern,
        grid=(b, ng),
        in_specs=[
            pl.BlockSpec((1, tg, dkh), lambda n, i: (n, blk(i), 0)),
            pl.BlockSpec((1, tg, dkh), lambda n, i: (n, blk(i), 1)),
            pl.BlockSpec((1, tg, dvh), lambda n, i: (n, blk(i), 2 * dkh // dvh)),
            pl.BlockSpec((1, tg, LANES), lambda n, i: (n, blk(i), lr_blk)),
            pl.BlockSpec((tg, dkh), lambda n, i: (blk(i), 0)),
            pl.BlockSpec((tg, dkh), lambda n, i: (blk(i), 0)),
            pl.BlockSpec((LANES, dkh), lambda n, i: (0, 0)),
            pl.BlockSpec((1, dkh), lambda n, i: (0, 0)),
            pl.BlockSpec((1, dkh, dvh), lambda n, i: (n, 0, 0)),
        ],
        out_specs=[
            pl.BlockSpec((1, tg, dvh), lambda n, i: (n, blk(i), 0)),
            pl.BlockSpec((1, dkh, dvh), lambda n, i: (n, 0, 0)),
        ],
        out_shape=[
            jax.ShapeDtypeStruct((b, t, dvh), F32),
            jax.ShapeDtypeStruct((b, dkh, dvh), F32),
        ],
        scratch_shapes=[
            pltpu.VMEM((dkh, dvh), F32),
            pltpu.VMEM((tg, dkh), BF16),
            pltpu.VMEM((tg, dkh), BF16),
            pltpu.VMEM((tg, dkh), F32),
            pltpu.VMEM((tg, dkh), F32),
        ],
        compiler_params=_params("parallel", "arbitrary"),
        name=name,
    )(p_b, p_b, p_b, p_b, cos, sin_signed, w2p, b2, s0)


def _rope_tables(l, dk):
    quarter = dk // 4
    pos = jnp.arange(l)
    rows_pos = (pos // GRID_W).astype(F32)
    cols_pos = (pos % GRID_W).astype(F32)
    inv = ROPE_BASE ** (-jnp.arange(0, 2 * quarter, 2, dtype=F32) / (2 * quarter))
    ang_r = rows_pos[:, None] * inv[None, :]
    ang_c = cols_pos[:, None] * inv[None, :]
    cos = jnp.concatenate([jnp.cos(ang_r)] * 2 + [jnp.cos(ang_c)] * 2, axis=1)
    sin = jnp.concatenate([-jnp.sin(ang_r), jnp.sin(ang_r), -jnp.sin(ang_c), jnp.sin(ang_c)], axis=1)
    return jnp.tile(cos, (1, GLA_HEADS)), jnp.tile(sin, (1, GLA_HEADS))


def _conv_kernel(a_ref, g_ref, z_ref, ap_ref, gp_ref, an_ref, gn_ref, w_ref, cb_ref, lg_ref, lb_ref,
                 o_ref, u_ref):
    t = a_ref.shape[1]
    step = pl.program_id(1)
    nstep = pl.num_programs(1)

    def glu(a, g):
        return a.astype(F32) * jax.nn.sigmoid(g.astype(F32))

    prev_ok = (step > 0).astype(F32)
    next_ok = (step < nstep - 1).astype(F32)
    u_ref[0:CONV_HALO, :] = glu(ap_ref[0], gp_ref[0]) * prev_ok
    u_ref[CONV_HALO:CONV_HALO + t, :] = glu(a_ref[0], g_ref[0])
    u_ref[CONV_HALO + t:2 * CONV_HALO + t, :] = glu(an_ref[0], gn_ref[0]) * next_ok

    off = CONV_HALO - CONV_K // 2
    for rc in range(t // CONV_RC):
        base = rc * CONV_RC
        acc = None
        for j in range(CONV_K):
            term = u_ref[base + off + j:base + off + j + CONV_RC, :] * w_ref[j:j + 1, :]
            acc = term if acc is None else acc + term
        acc = acc + cb_ref[...]
        mu = jnp.mean(acc, axis=-1, keepdims=True)
        xc = acc - mu
        var = jnp.mean(xc * xc, axis=-1, keepdims=True)
        y = xc * lax.rsqrt(var + LN_EPS) * lg_ref[...] + lb_ref[...]
        z = z_ref[0, base:base + CONV_RC, :].astype(F32)
        o_ref[0, base:base + CONV_RC, :] = (_silu(y) * _silu(z)).astype(o_ref.dtype)


def _conv_call(p, conv_w, conv_b, ln_g, ln_b, col0, d_c, name):
    b, t, _ = p.shape
    tt = min(CONV_T, t)
    nt = t // tt
    cb = col0 // d_c
    hb = tt // CONV_HALO
    nhalo = t // CONV_HALO
    w = jnp.zeros((CONV_K + 1, d_c), F32).at[:CONV_K].set(conv_w)

    def prev(i):
        return jnp.maximum(i * hb - 1, 0)

    def nxt(i):
        return jnp.minimum((i + 1) * hb, nhalo - 1)

    row = lambda v: v.reshape(1, d_c)
    return pl.pallas_call(
        _conv_kernel,
        grid=(b, nt),
        in_specs=[
            pl.BlockSpec((1, tt, d_c), lambda n, i: (n, i, cb)),
            pl.BlockSpec((1, tt, d_c), lambda n, i: (n, i, cb + 1)),
            pl.BlockSpec((1, tt, d_c), lambda n, i: (n, i, cb + 2)),
            pl.BlockSpec((1, CONV_HALO, d_c), lambda n, i: (n, prev(i), cb)),
            pl.BlockSpec((1, CONV_HALO, d_c), lambda n, i: (n, prev(i), cb + 1)),
            pl.BlockSpec((1, CONV_HALO, d_c), lambda n, i: (n, nxt(i), cb)),
            pl.BlockSpec((1, CONV_HALO, d_c), lambda n, i: (n, nxt(i), cb + 1)),
            pl.BlockSpec((CONV_K + 1, d_c), lambda n, i: (0, 0)),
            pl.BlockSpec((1, d_c), lambda n, i: (0, 0)),
            pl.BlockSpec((1, d_c), lambda n, i: (0, 0)),
            pl.BlockSpec((1, d_c), lambda n, i: (0, 0)),
        ],
        out_specs=pl.BlockSpec((1, tt, d_c), lambda n, i: (n, i, 0)),
        out_shape=jax.ShapeDtypeStruct((b, t, d_c), BF16),
        scratch_shapes=[pltpu.VMEM((tt + 2 * CONV_HALO, d_c), F32)],
        compiler_params=_params("parallel", "arbitrary"),
        name=name,
    )(p, p, p, p, p, p, p, w, row(conv_b), row(ln_g), row(ln_b))


def _out_kernel(oa_ref, of_ref, ob_ref, zb_ref, oc_ref, x_ref, gate_ref, gn_ref,
                wa_ref, wb_ref, wc_ref, lg_ref, lb_ref, o_ref, y_ref, *, alpha):
    tm = x_ref.shape[0]
    dvh = of_ref.shape[1]
    dv = dvh // GLA_HEADS
    s = of_ref[...] + ob_ref[...]
    parts = []
    for h in range(GLA_HEADS):
        sh = s[:, h * dv:(h + 1) * dv]
        ms = jnp.mean(sh * sh, axis=-1, keepdims=True)
        parts.append(sh * lax.rsqrt(ms + RMS_EPS))
    out_b = jnp.concatenate(parts, axis=1) * gn_ref[...] * _silu(zb_ref[...])
    y_ref[...] = (_dot(oa_ref[...], wa_ref[...]) + _dot(out_b.astype(BF16), wb_ref[...])
                  + _dot(oc_ref[...], wc_ref[...]))
    gate = gate_ref[0]

    def body(i, carry):
        r0 = pl.multiple_of(i * LN_ROWS, LN_ROWS)
        rv = alpha * x_ref[pl.ds(r0, LN_ROWS), :] + gate * y_ref[pl.ds(r0, LN_ROWS), :]
        mu = jnp.mean(rv, axis=-1, keepdims=True)
        xc = rv - mu
        var = jnp.mean(xc * xc, axis=-1, keepdims=True)
        o_ref[pl.ds(r0, LN_ROWS), :] = xc * lax.rsqrt(var + LN_EPS) * lg_ref[...] + lb_ref[...]
        return carry

    lax.fori_loop(0, tm // LN_ROWS, body, 0)


def _out_call(oa, o_f, o_b, p_b, oc, x2, gate, gla_norm, w_out, ln_g, ln_b, rows_per_gate, zb_col, alpha, name):
    m, d = x2.shape
    d_a, dvh, d_c = oa.shape[1], o_f.shape[1], oc.shape[1]
    tm = min(OUT_TM, m)
    per = rows_per_gate // tm
    wa = w_out[:d_a].astype(BF16)
    wb = w_out[d_a:d_a + dvh].astype(BF16)
    wc = w_out[d_a + dvh:].astype(BF16)
    gn = jnp.tile(gla_norm, GLA_HEADS).reshape(1, dvh)
    full = lambda shape: pl.BlockSpec(shape, lambda i: (0,) * len(shape))
    return pl.pallas_call(
        functools.partial(_out_kernel, alpha=alpha),
        grid=(m // tm,),
        in_specs=[
            pl.BlockSpec((tm, d_a), lambda i: (i, 0)),
            pl.BlockSpec((tm, dvh), lambda i: (i, 0)),
            pl.BlockSpec((tm, dvh), lambda i: (i, 0)),
            pl.BlockSpec((tm, dvh), lambda i: (i, zb_col // dvh)),
            pl.BlockSpec((tm, d_c), lambda i: (i, 0)),
            pl.BlockSpec((tm, d), lambda i: (i, 0)),
            pl.BlockSpec((1, 1, d), lambda i: (i // per, 0, 0)),
            full((1, dvh)),
            full((d_a, d)),
            full((dvh, d)),
            full((d_c, d)),
            full((1, d)),
            full((1, d)),
        ],
        out_specs=pl.BlockSpec((tm, d), lambda i: (i, 0)),
        out_shape=jax.ShapeDtypeStruct((m, d), F32),
        scratch_shapes=[pltpu.VMEM((tm, d), F32)],
        compiler_params=_params("parallel"),
        name=name,
    )(oa, o_f, o_b, p_b, oc, x2, gate, gn, wa, wb, wc, ln_g.reshape(1, d), ln_b.reshape(1, d))


def kernel(x, c, ctx, c_ctx, w_ada, b_ada, w_in, rpb, gla_w2, gla_b, gla_norm, conv_w, conv_b,
           conv_ln_g, conv_ln_b, w_out, post_ln_g, post_ln_b):
    b, l, d = x.shape
    lc = ctx.shape[1]
    depth = w_ada.shape[0]
    heads_a = rpb.shape[1]
    d_a = heads_a * NA_HEAD_DIM
    dkh = gla_w2.shape[-1]
    dvh = gla_norm.shape[-1] * GLA_HEADS
    d_c = conv_w.shape[-1]
    n_a = 4 * d_a
    n_b = 2 * dkh + 2 * dvh + 2 * GATE_RANK
    n_ac = n_a + 3 * d_c
    n_bp = -(-n_b // LANES) * LANES
    alpha = (2 * depth) ** 0.25
    assert w_in.shape[-1] == n_a + n_b + 3 * d_c and d_a + dvh + d_c == w_out.shape[1]

    cond = jnp.zeros((8, d), F32).at[:b].set(c).at[b].set(c_ctx)
    mod = _ada_call(cond, w_ada, b_ada)

    cos, sin_signed = _rope_tables(l, dkh // GLA_HEADS)
    ones_c = jnp.ones((lc, dkh), F32)
    zeros_c = jnp.zeros((lc, dkh), F32)
    tn_ac = n_ac // 4

    x2 = x.reshape(b * l, d)
    cx2 = ctx.reshape(b * lc, d)
    for layer in range(depth):
        last = layer == depth - 1
        shift, scale, gate = jnp.split(mod[layer], 3, axis=-1)
        mod_lat = jnp.stack([shift[:b], scale[:b]], axis=1)
        mod_ctx = jnp.stack([shift[b:b + 1], scale[b:b + 1]], axis=1)
        gate_lat = gate[:b].reshape(b, 1, d)
        gate_ctx = gate[b:b + 1].reshape(1, 1, d)

        wl = w_in[layer]
        w_ac = jnp.concatenate([wl[:, :n_a], wl[:, n_a + n_b:]], axis=1).astype(BF16)
        w_b = jnp.pad(wl[:, n_a:n_a + n_b], ((0, 0), (0, n_bp - n_b))).astype(BF16)

        p_ac = _proj_call(x2, mod_lat, w_ac, l, tn_ac, BF16, "proj_ac").reshape(b, l, n_ac)
        p_b = _proj_call(x2, mod_lat, w_b, l, n_bp, F32, "proj_b").reshape(b, l, n_bp)
        pc_ac = _proj_call(cx2, mod_ctx, w_ac, b * lc, tn_ac, BF16, "proj_ac_ctx").reshape(b, lc, n_ac)
        pc_b = _proj_call(cx2, mod_ctx, w_b, b * lc, n_bp, F32, "proj_b_ctx").reshape(b, lc, n_bp)

        out_a = _na_call(p_ac, pc_ac, rpb[layer], d_a)

        w2p = [jnp.zeros((LANES, dkh), F32).at[i * GATE_RANK:(i + 1) * GATE_RANK].set(gla_w2[layer, i])
               for i in range(2)]
        b2 = [gla_b[layer, i].reshape(1, dkh) for i in range(2)]
        s0 = jnp.zeros((b, dkh, dvh), F32)
        gla = functools.partial(_gla_call, dkh=dkh, dvh=dvh)
        oc_f, s_f = gla(pc_b, ones_c, zeros_c, w2p[0], b2[0], s0, reverse=False, rope=False, name="gla_ctx_fwd")
        oc_b, s_b = gla(pc_b, ones_c, zeros_c, w2p[1], b2[1], s0, reverse=True, rope=False, name="gla_ctx_bwd")
        o_f, _ = gla(p_b, cos, sin_signed, w2p[0], b2[0], s_f, reverse=False, rope=True, name="gla_fwd")
        o_b, _ = gla(p_b, cos, sin_signed, w2p[1], b2[1], s_b, reverse=True, rope=True, name="gla_bwd")

        out_c = _conv_call(p_ac, conv_w[layer], conv_b[layer], conv_ln_g[layer], conv_ln_b[layer],
                           n_a, d_c, "conv")

        zb_col = 2 * dkh + dvh
        x_new = _out_call(out_a.reshape(b * l, d_a), o_f.reshape(b * l, dvh), o_b.reshape(b * l, dvh),
                          p_b.reshape(b * l, n_bp), out_c.reshape(b * l, d_c), x2, gate_lat,
                          gla_norm[layer], w_out[layer], post_ln_g[layer], post_ln_b[layer],
                          l, zb_col, alpha, "out_proj")
        if not last:
            out_a_c = _ctx_attn_call(pc_ac, d_a)
            out_c_c = _conv_call(pc_ac, conv_w[layer], conv_b[layer], conv_ln_g[layer], conv_ln_b[layer],
                                 n_a, d_c, "conv_ctx")
            cx2 = _out_call(out_a_c.reshape(b * lc, d_a), oc_f.reshape(b * lc, dvh),
                            oc_b.reshape(b * lc, dvh), pc_b.reshape(b * lc, n_bp),
                            out_c_c.reshape(b * lc, d_c), cx2, gate_ctx, gla_norm[layer], w_out[layer],
                            post_ln_g[layer], post_ln_b[layer], b * lc, zb_col, alpha, "out_proj_ctx")
        x2 = x_new
    return x2.reshape(b, l, d)
```

```python
import functools

import numpy as np
import jax
import jax.numpy as jnp
from jax import lax
from jax.experimental import pallas as pl
from jax.experimental.pallas import tpu as pltpu

F32 = jnp.float32
BF16 = jnp.bfloat16

GRID_W = 64
NA_HEAD_DIM = 64
NA_KH_MAX = 8
NA_KW = 16
GLA_HEADS = 4
GATE_RANK = 16
GATE_TAU = 16.0
GLA_CHUNK = 64
CONV_K = 31
ROPE_BASE = 10000.0
LN_EPS = 1e-5
RMS_EPS = 1e-6

LANES = 128
VMEM_LIMIT_BYTES = 56 * 1024 * 1024

NEG_BIG = -1e30

PROJ_TM = 1024
OUT_TM = 256
LN_TM = 512
LN_ROWS = 32
LN_UNROLL = 4
ATT_TQ_ROWS = 8
ATT_ITEM_ROWS = 2
GLA_TG = 256
CONV_T = 512
CONV_HALO = 16
CONV_RC = 64


def _silu(x):
    return x * jax.nn.sigmoid(x)


def _dot(a, b):
    return jnp.dot(a, b, preferred_element_type=F32)


def _dot_nt(a, b):
    return lax.dot_general(a, b, (((1,), (1,)), ((), ())), preferred_element_type=F32)


def _dot_tn(a, b):
    return lax.dot_general(a, b, (((0,), (0,)), ((), ())), preferred_element_type=F32)


def _shr(x, pow2):
    shift = pow2.bit_length() - 1
    assert 1 << shift == pow2
    return jnp.right_shift(x, shift)


def _split_hi_lo(x):
    hi = x.astype(BF16)
    lo = (x - hi.astype(F32)).astype(BF16)
    return hi, lo


def _params(*sem):
    return pltpu.CompilerParams(dimension_semantics=sem, vmem_limit_bytes=VMEM_LIMIT_BYTES)


def _ada_kernel(c_ref, w_ref, b_ref, o_ref):
    s = _silu(c_ref[...]).astype(BF16)
    o_ref[0] = _dot(s, w_ref[0].astype(BF16)) + b_ref[0]


def _ada_call(cond, w_ada, b_ada, tn=512):
    depth, d, n = w_ada.shape
    rows = cond.shape[0]
    return pl.pallas_call(
        _ada_kernel,
        grid=(depth, n // tn),
        in_specs=[
            pl.BlockSpec((rows, d), lambda l, j: (0, 0)),
            pl.BlockSpec((1, d, tn), lambda l, j: (l, 0, j)),
            pl.BlockSpec((1, 1, tn), lambda l, j: (l, 0, j)),
        ],
        out_specs=pl.BlockSpec((1, rows, tn), lambda l, j: (l, 0, j)),
        out_shape=jax.ShapeDtypeStruct((depth, rows, n), F32),
        compiler_params=_params("parallel", "parallel"),
        name="ada_mod",
    )(cond, w_ada, b_ada.reshape(depth, 1, n))


def _layer_norm_rows(v):
    mu = jnp.mean(v, axis=-1, keepdims=True)
    vc = v - mu
    var = jnp.mean(vc * vc, axis=-1, keepdims=True)
    return vc * lax.rsqrt(var + LN_EPS)


def _ln_mod_kernel(x_ref, mod_ref, h_ref):
    shift = mod_ref[0, 0:1, :]
    scale1 = 1.0 + mod_ref[0, 1:2, :]

    def body(i, carry):
        r0 = pl.multiple_of(i * LN_ROWS, LN_ROWS)
        hn = _layer_norm_rows(x_ref[pl.ds(r0, LN_ROWS), :])
        h_ref[pl.ds(r0, LN_ROWS), :] = (hn * scale1 + shift).astype(BF16)
        return carry

    lax.fori_loop(0, x_ref.shape[0] // LN_ROWS, body, 0, unroll=LN_UNROLL)


def _ln_mod_call(x2, mod, rows_per_mod, name):
    m, d = x2.shape
    tm = min(LN_TM, m)
    per = rows_per_mod // tm
    return pl.pallas_call(
        _ln_mod_kernel,
        grid=(m // tm,),
        in_specs=[
            pl.BlockSpec((tm, d), lambda i: (i, 0)),
            pl.BlockSpec((1, 2, d), lambda i: (i // per, 0, 0)),
        ],
        out_specs=pl.BlockSpec((tm, d), lambda i: (i, 0)),
        out_shape=jax.ShapeDtypeStruct((m, d), BF16),
        compiler_params=_params("parallel"),
        name=name,
    )(x2, mod)


def _matmul_kernel(h_ref, w_ref, o_ref):
    o_ref[...] = _dot(h_ref[...], w_ref[...]).astype(o_ref.dtype)


def _proj_call(h2, w, tn, out_dtype, name):
    m, d = h2.shape
    n = w.shape[1]
    tm = min(PROJ_TM, m)
    return pl.pallas_call(
        _matmul_kernel,
        grid=(m // tm, n // tn),
        in_specs=[
            pl.BlockSpec((tm, d), lambda i, j: (i, 0)),
            pl.BlockSpec((d, tn), lambda i, j: (0, j)),
        ],
        out_specs=pl.BlockSpec((tm, tn), lambda i, j: (i, j)),
        out_shape=jax.ShapeDtypeStruct((m, n), out_dtype),
        compiler_params=_params("parallel", "arbitrary"),
        name=name,
    )(h2, w)


def _softmax_pv(q, lane_lo, pieces):
    outs = []
    for h in range(2):
        keep = lane_lo if h == 0 else jnp.logical_not(lane_lo)
        qh = jnp.where(keep, q, jnp.zeros_like(q)) * jnp.asarray(NA_HEAD_DIM ** -0.5, BF16)
        scores = []
        for k, _, bias in pieces:
            s = _dot_nt(qh, k)
            if bias is not None:
                s = s + bias[h]
            scores.append(s)
        m = scores[0].max(axis=-1, keepdims=True)
        for s in scores[1:]:
            m = jnp.maximum(m, s.max(axis=-1, keepdims=True))
        den = None
        acc = None
        for s, (_, v, _) in zip(scores, pieces):
            p = jnp.exp(s - m)
            ps = p.sum(axis=-1, keepdims=True)
            pv = _dot(p.astype(BF16), v)
            den = ps if den is None else den + ps
            acc = pv if acc is None else acc + pv
        outs.append(acc / den)
    return jnp.where(lane_lo, outs[0], outs[1])


def _na_kernel(q_ref, k_ref, v_ref, z_ref, kc_ref, vc_ref, bias_ref, o_ref, *, rows):
    tq = q_ref.shape[1]
    kh = min(NA_KH_MAX, rows)
    nkeys = kh * GRID_W
    qi = pl.program_id(2)
    lane_lo = lax.broadcasted_iota(jnp.int32, (tq, LANES), 1) < NA_HEAD_DIM
    q = q_ref[0]
    kc = kc_ref[0]
    vc = vc_ref[0]
    starts, d0s = [], []
    for i in range(ATT_TQ_ROWS):
        r = qi * ATT_TQ_ROWS + i
        rs = jnp.clip(r - kh // 2, 0, rows - kh)
        d0s.append(rs - r + (NA_KH_MAX - 1))
        starts.append(pl.multiple_of(rs * GRID_W, GRID_W))
    qhs = []
    for h in range(2):
        keep = lane_lo if h == 0 else jnp.logical_not(lane_lo)
        qhs.append(jnp.where(keep, q, jnp.zeros_like(q)) * jnp.asarray(NA_HEAD_DIM ** -0.5, BF16))

    def stage_scores(h, rows_i):
        qs = qhs[h][rows_i[0] * GRID_W:(rows_i[-1] + 1) * GRID_W]
        s_loc = jnp.concatenate(
            [_dot_nt(qhs[h][i * GRID_W:(i + 1) * GRID_W], k_ref[0, pl.ds(starts[i], nkeys), :])
             + bias_ref[d0s[i], 0, h] for i in rows_i], axis=0)
        return s_loc, _dot_nt(qs, kc)

    def stage_softmax(s_loc, s_ctx):
        m = jnp.maximum(s_loc.max(axis=-1, keepdims=True), s_ctx.max(axis=-1, keepdims=True))
        p_loc = jnp.exp(s_loc - m)
        p_ctx = jnp.exp(s_ctx - m)
        den = p_loc.sum(axis=-1, keepdims=True) + p_ctx.sum(axis=-1, keepdims=True)
        return p_loc.astype(BF16), p_ctx.astype(BF16), den

    def stage_values(p_loc, p_ctx, den, rows_i):
        acc = jnp.concatenate(
            [_dot(p_loc[j * GRID_W:(j + 1) * GRID_W], v_ref[0, pl.ds(starts[i], nkeys), :])
             for j, i in enumerate(rows_i)], axis=0)
        return (acc + _dot(p_ctx, vc)) / den

    groups = [list(range(g, g + ATT_ITEM_ROWS)) for g in range(0, ATT_TQ_ROWS, ATT_ITEM_ROWS)]
    items = [(h, g) for h in range(2) for g in groups]
    scores, probs, outs = {}, {}, {}
    for t in range(len(items) + 2):
        if t < len(items):
            scores[t] = stage_scores(*items[t])
        if 0 <= t - 1 < len(items):
            probs[t - 1] = stage_softmax(*scores.pop(t - 1))
        if 0 <= t - 2 < len(items):
            outs[t - 2] = stage_values(*probs.pop(t - 2), items[t - 2][1])
    per_head = [jnp.concatenate([outs[t] for t, (hh, _) in enumerate(items) if hh == h], axis=0)
                for h in range(2)]
    o = jnp.where(lane_lo, per_head[0], per_head[1])
    o_ref[0] = (o * _silu(z_ref[0].astype(F32))).astype(o_ref.dtype)


def _na_bias_table(rpb, rows):
    heads = rpb.shape[0]
    kh = min(NA_KH_MAX, rows)
    nd = 2 * NA_KH_MAX - kh

    c = np.arange(GRID_W)[:, None]
    kc = np.arange(GRID_W)[None, :]
    cs = np.clip(c - NA_KW // 2, 0, GRID_W - NA_KW)
    valid_c = (kc >= cs) & (kc < cs + NA_KW)
    dcol = kc - c + (NA_KW - 1)
    onehot = ((dcol[None] == np.arange(2 * NA_KW - 1)[:, None, None]) & valid_c[None]).astype(np.float32)
    blocks = jnp.einsum("hde,eck->hdck", rpb, onehot, precision=lax.Precision.HIGHEST)
    blocks = jnp.where(valid_c, blocks, NEG_BIG)

    for r in range(rows):
        assert 0 <= int(np.clip(r - kh // 2, 0, rows - kh)) - r + (NA_KH_MAX - 1) < nd
    slabs = [jnp.transpose(blocks[:, d0:d0 + kh], (0, 2, 1, 3)).reshape(heads, GRID_W, kh * GRID_W)
             for d0 in range(nd)]
    return jnp.stack(slabs, axis=0).reshape(nd, heads // 2, 2, GRID_W, kh * GRID_W)


def _na_call(p_lat, p_ctx, rpb, d_a):
    b, l, _ = p_lat.shape
    lc = p_ctx.shape[1]
    rows = l // GRID_W
    assert rows % ATT_TQ_ROWS == 0
    tq = ATT_TQ_ROWS * GRID_W
    nq = l // tq
    npair = d_a // LANES
    bias = _na_bias_table(rpb, rows)
    nd, _, _, _, nkeys = bias.shape

    return pl.pallas_call(
        functools.partial(_na_kernel, rows=rows),
        grid=(b, npair, nq),
        in_specs=[
            pl.BlockSpec((1, tq, LANES), lambda i, p, q: (i, q, p)),
            pl.BlockSpec((1, l, LANES), lambda i, p, q: (i, 0, npair + p)),
            pl.BlockSpec((1, l, LANES), lambda i, p, q: (i, 0, 2 * npair + p)),
            pl.BlockSpec((1, tq, LANES), lambda i, p, q: (i, q, 3 * npair + p)),
            pl.BlockSpec((1, lc, LANES), lambda i, p, q: (i, 0, npair + p)),
            pl.BlockSpec((1, lc, LANES), lambda i, p, q: (i, 0, 2 * npair + p)),
            pl.BlockSpec((nd, 1, 2, GRID_W, nkeys), lambda i, p, q: (0, p, 0, 0, 0)),
        ],
        out_specs=pl.BlockSpec((1, tq, LANES), lambda i, p, q: (i, q, p)),
        out_shape=jax.ShapeDtypeStruct((b, l, d_a), BF16),
        compiler_params=_params("parallel", "parallel", "arbitrary"),
        name="na_attention",
    )(p_lat, p_lat, p_lat, p_lat, p_ctx, p_ctx, bias)


def _ctx_attn_kernel(q_ref, k_ref, v_ref, z_ref, o_ref):
    tq = q_ref.shape[1]
    lane_lo = lax.broadcasted_iota(jnp.int32, (tq, LANES), 1) < NA_HEAD_DIM
    o = _softmax_pv(q_ref[0], lane_lo, [(k_ref[0], v_ref[0], None)])
    o_ref[0] = (o * _silu(z_ref[0].astype(F32))).astype(o_ref.dtype)


def _ctx_attn_call(p_ctx, d_a):
    b, lc, _ = p_ctx.shape
    npair = d_a // LANES
    return pl.pallas_call(
        _ctx_attn_kernel,
        grid=(b, npair),
        in_specs=[
            pl.BlockSpec((1, lc, LANES), lambda i, p: (i, 0, p)),
            pl.BlockSpec((1, lc, LANES), lambda i, p: (i, 0, npair + p)),
            pl.BlockSpec((1, lc, LANES), lambda i, p: (i, 0, 2 * npair + p)),
            pl.BlockSpec((1, lc, LANES), lambda i, p: (i, 0, 3 * npair + p)),
        ],
        out_specs=pl.BlockSpec((1, lc, LANES), lambda i, p: (i, 0, p)),
        out_shape=jax.ShapeDtypeStruct((b, lc, d_a), BF16),
        compiler_params=_params("parallel", "parallel"),
        name="ctx_attention",
    )(p_ctx, p_ctx, p_ctx, p_ctx)


def _rope(x, cos, sin_signed):
    halves = []
    for j in range(x.shape[1] // LANES):
        xs = x[:, j * LANES:(j + 1) * LANES]
        lane = lax.broadcasted_iota(jnp.int32, xs.shape, 1)
        first = (lane & 16) == 0
        partner = jnp.where(first, pltpu.roll(xs, LANES - 16, 1), pltpu.roll(xs, 16, 1))
        halves.append(partner)
    partner = jnp.concatenate(halves, axis=1)
    return x * cos + partner * sin_signed


def _gla_kernel(q_ref, k_ref, v_ref, lr_ref, cos_ref, sin_ref, w2_ref, b2_ref, s0_ref,
                o_ref, sfin_ref, s_ref, qt_ref, kt_ref, kd_ref, g_ref, *, reverse, rope):
    tg = q_ref.shape[1]
    dkh = q_ref.shape[2]
    dvh = v_ref.shape[2]
    dk = dkh // GLA_HEADS
    dv = dvh // GLA_HEADS
    nchunk = tg // GLA_CHUNK
    step = pl.program_id(1)

    @pl.when(step == 0)
    def _():
        s_ref[...] = s0_ref[0]

    lr_hi, lr_lo = _split_hi_lo(lr_ref[0])
    w_hi, w_lo = _split_hi_lo(w2_ref[...])
    logits = _dot(lr_hi, w_hi) + _dot(lr_lo, w_hi) + _dot(lr_hi, w_lo) + b2_ref[...]
    g = (jnp.minimum(logits, 0.0) - jnp.log1p(jnp.exp(-jnp.abs(logits)))) * (1.0 / GATE_TAU)

    r = lax.broadcasted_iota(jnp.int32, (tg, tg), 0)
    c = lax.broadcasted_iota(jnp.int32, (tg, tg), 1)
    same = _shr(r, GLA_CHUNK) == _shr(c, GLA_CHUNK)
    upto = (c >= r) if reverse else (c <= r)
    t_inc = jnp.where(same & upto, 1.0, 0.0).astype(BF16)
    t_rest = jnp.where(same & jnp.logical_not(upto), 1.0, 0.0).astype(BF16)
    g_hi, g_lo = _split_hi_lo(g)
    b_inc = _dot(t_inc, g_hi) + _dot(t_inc, g_lo)
    b_rest = _dot(t_rest, g_hi) + _dot(t_rest, g_lo)

    q = q_ref[0]
    k = k_ref[0]
    if rope:
        q = _rope(q, cos_ref[...], sin_ref[...])
        k = _rope(k, cos_ref[...], sin_ref[...])
    q = q * (dk ** -0.5)
    qt_ref[...] = (q * jnp.exp(b_inc)).astype(BF16)
    kt_ref[...] = (k * jnp.exp(-b_inc)).astype(BF16)
    kd_ref[...] = k * jnp.exp(b_rest)
    g_ref[...] = g

    rk = _shr(lax.broadcasted_iota(jnp.int32, (dkh, dkh), 0), dk)
    ck = _shr(lax.broadcasted_iota(jnp.int32, (dkh, dkh), 1), dk)
    mask_k = rk == ck
    rv = _shr(lax.broadcasted_iota(jnp.int32, (dkh, dvh), 0), dk)
    cv = _shr(lax.broadcasted_iota(jnp.int32, (dkh, dvh), 1), dv)
    mask_v = rv == cv
    ti = lax.broadcasted_iota(jnp.int32, (GLA_CHUNK, dkh), 0)
    tj = lax.broadcasted_iota(jnp.int32, (GLA_CHUNK, dkh), 1) & (GLA_CHUNK - 1)
    causal = (tj >= ti) if reverse else (tj <= ti)
    ones = jnp.ones((GLA_CHUNK, LANES), BF16)

    order = range(nchunk - 1, -1, -1) if reverse else range(nchunk)
    for ci in order:
        sl = slice(ci * GLA_CHUNK, (ci + 1) * GLA_CHUNK)
        qt = qt_ref[sl, :]
        kt = kt_ref[sl, :]
        vb = v_ref[0, sl, :].astype(BF16)
        state = s_ref[...]
        o_inter = _dot(qt, state.astype(BF16))
        k_blk = jnp.where(mask_k, jnp.concatenate([kt] * GLA_HEADS, axis=0), jnp.zeros((), BF16))
        att = jnp.where(causal, _dot_nt(qt, k_blk), 0.0)
        v_blk = jnp.where(mask_v, jnp.concatenate([vb] * GLA_HEADS, axis=0), jnp.zeros((), BF16))
        o_ref[0, sl, :] = o_inter + _dot(att.astype(BF16), v_blk)
        kv = jnp.where(mask_v, _dot_tn(kd_ref[sl, :].astype(BF16), vb), 0.0)
        gh, gl = _split_hi_lo(g_ref[sl, :])
        decay = jnp.exp(_dot_tn(gh, ones) + _dot_tn(gl, ones))
        s_ref[...] = jnp.concatenate([decay] * (dvh // LANES), axis=1) * state + kv

    @pl.when(step == pl.num_programs(1) - 1)
    def _():
        sfin_ref[0] = s_ref[...]


def _gla_call(p_b, cos, sin_signed, w2p, b2, s0, *, reverse, rope, dkh, dvh, name):
    b, t, _ = p_b.shape
    tg = min(GLA_TG, t)
    ng = t // tg
    lr_blk = (2 * dkh + 2 * dvh) // LANES

    def blk(i):
        return ng - 1 - i if reverse else i

    kern = functools.partial(_gla_kernel, reverse=reverse, rope=rope)
    return pl.pallas_call(
        kern,
        grid=(b, ng),
        in_specs=[
            pl.BlockSpec((1, tg, dkh), lambda n, i: (n, blk(i), 0)),
            pl.BlockSpec((1, tg, dkh), lambda n, i: (n, blk(i), 1)),
            pl.BlockSpec((1, tg, dvh), lambda n, i: (n, blk(i), 2 * dkh // dvh)),
            pl.BlockSpec((1, tg, LANES), lambda n, i: (n, blk(i), lr_blk)),
            pl.BlockSpec((tg, dkh), lambda n, i: (blk(i), 0)),
            pl.BlockSpec((tg, dkh), lambda n, i: (blk(i), 0)),
            pl.BlockSpec((LANES, dkh), lambda n, i: (0, 0)),
            pl.BlockSpec((1, dkh), lambda n, i: (0, 0)),
            pl.BlockSpec((1, dkh, dvh), lambda n, i: (n, 0, 0)),
        ],
        out_specs=[
            pl.BlockSpec((1, tg, dvh), lambda n, i: (n, blk(i), 0)),
            pl.BlockSpec((1, dkh, dvh), lambda n, i: (n, 0, 0)),
        ],
        out_shape=[
            jax.ShapeDtypeStruct((b, t, dvh), F32),
            jax.ShapeDtypeStruct((b, dkh, dvh), F32),
        ],
        scratch_shapes=[
            pltpu.VMEM((dkh, dvh), F32),
            pltpu.VMEM((tg, dkh), BF16),
            pltpu.VMEM((tg, dkh), BF16),
            pltpu.VMEM((tg, dkh), F32),
            pltpu.VMEM((tg, dkh), F32),
        ],
        compiler_params=_params("parallel", "arbitrary"),
        name=name,
    )(p_b, p_b, p_b, p_b, cos, sin_signed, w2p, b2, s0)


def _rope_tables(l, dk):
    quarter = dk // 4
    pos = jnp.arange(l)
    rows_pos = (pos // GRID_W).astype(F32)
    cols_pos = (pos % GRID_W).astype(F32)
    inv = ROPE_BASE ** (-jnp.arange(0, 2 * quarter, 2, dtype=F32) / (2 * quarter))
    ang_r = rows_pos[:, None] * inv[None, :]
    ang_c = cols_pos[:, None] * inv[None, :]
    cos = jnp.concatenate([jnp.cos(ang_r)] * 2 + [jnp.cos(ang_c)] * 2, axis=1)
    sin = jnp.concatenate([-jnp.sin(ang_r), jnp.sin(ang_r), -jnp.sin(ang_c), jnp.sin(ang_c)], axis=1)
    return jnp.tile(cos, (1, GLA_HEADS)), jnp.tile(sin, (1, GLA_HEADS))


def _conv_kernel(a_ref, g_ref, z_ref, ap_ref, gp_ref, an_ref, gn_ref, w_ref, cb_ref, lg_ref, lb_ref,
                 o_ref, u_ref):
    t = a_ref.shape[1]
    step = pl.program_id(1)
    nstep = pl.num_programs(1)

    def glu(a, g):
        return a.astype(F32) * jax.nn.sigmoid(g.astype(F32))

    prev_ok = (step > 0).astype(F32)
    next_ok = (step < nstep - 1).astype(F32)
    u_ref[0:CONV_HALO, :] = glu(ap_ref[0], gp_ref[0]) * prev_ok
    u_ref[CONV_HALO:CONV_HALO + t, :] = glu(a_ref[0], g_ref[0])
    u_ref[CONV_HALO + t:2 * CONV_HALO + t, :] = glu(an_ref[0], gn_ref[0]) * next_ok

    off = CONV_HALO - CONV_K // 2
    for rc in range(t // CONV_RC):
        base = rc * CONV_RC
        acc = None
        for j in range(CONV_K):
            term = u_ref[base + off + j:base + off + j + CONV_RC, :] * w_ref[j:j + 1, :]
            acc = term if acc is None else acc + term
        acc = acc + cb_ref[...]
        mu = jnp.mean(acc, axis=-1, keepdims=True)
        xc = acc - mu
        var = jnp.mean(xc * xc, axis=-1, keepdims=True)
        y = xc * lax.rsqrt(var + LN_EPS) * lg_ref[...] + lb_ref[...]
        z = z_ref[0, base:base + CONV_RC, :].astype(F32)
        o_ref[0, base:base + CONV_RC, :] = (_silu(y) * _silu(z)).astype(o_ref.dtype)


def _conv_call(p, conv_w, conv_b, ln_g, ln_b, col0, d_c, name):
    b, t, _ = p.shape
    tt = min(CONV_T, t)
    nt = t // tt
    cb = col0 // d_c
    hb = tt // CONV_HALO
    nhalo = t // CONV_HALO
    w = jnp.zeros((CONV_K + 1, d_c), F32).at[:CONV_K].set(conv_w)

    def prev(i):
        return jnp.maximum(i * hb - 1, 0)

    def nxt(i):
        return jnp.minimum((i + 1) * hb, nhalo - 1)

    row = lambda v: v.reshape(1, d_c)
    return pl.pallas_call(
        _conv_kernel,
        grid=(b, nt),
        in_specs=[
            pl.BlockSpec((1, tt, d_c), lambda n, i: (n, i, cb)),
            pl.BlockSpec((1, tt, d_c), lambda n, i: (n, i, cb + 1)),
            pl.BlockSpec((1, tt, d_c), lambda n, i: (n, i, cb + 2)),
            pl.BlockSpec((1, CONV_HALO, d_c), lambda n, i: (n, prev(i), cb)),
            pl.BlockSpec((1, CONV_HALO, d_c), lambda n, i: (n, prev(i), cb + 1)),
            pl.BlockSpec((1, CONV_HALO, d_c), lambda n, i: (n, nxt(i), cb)),
            pl.BlockSpec((1, CONV_HALO, d_c), lambda n, i: (n, nxt(i), cb + 1)),
            pl.BlockSpec((CONV_K + 1, d_c), lambda n, i: (0, 0)),
            pl.BlockSpec((1, d_c), lambda n, i: (0, 0)),
            pl.BlockSpec((1, d_c), lambda n, i: (0, 0)),
            pl.BlockSpec((1, d_c), lambda n, i: (0, 0)),
        ],
        out_specs=pl.BlockSpec((1, tt, d_c), lambda n, i: (n, i, 0)),
        out_shape=jax.ShapeDtypeStruct((b, t, d_c), BF16),
        scratch_shapes=[pltpu.VMEM((tt + 2 * CONV_HALO, d_c), F32)],
        compiler_params=_params("parallel", "arbitrary"),
        name=name,
    )(p, p, p, p, p, p, p, w, row(conv_b), row(ln_g), row(ln_b))


def _out_kernel(oa_ref, of_ref, ob_ref, zb_ref, oc_ref, x_ref, gate_ref, gn_ref,
                wa_ref, wb_ref, wc_ref, lg_ref, lb_ref, *rest, alpha, emit_h):
    if emit_h:
        mod_ref, o_ref, h_ref, y0_ref, y1_ref = rest
    else:
        o_ref, y0_ref, y1_ref = rest
    tm = x_ref.shape[0]
    dvh = of_ref.shape[1]
    dv = dvh // GLA_HEADS
    step = pl.program_id(0)

    @pl.when(step == 0)
    def _():
        y1_ref[...] = jnp.zeros_like(y1_ref)

    def run(y_prev, y_cur):
        gate = gate_ref[0]
        if emit_h:
            shift = mod_ref[0, 0:1, :]
            scale1 = 1.0 + mod_ref[0, 1:2, :]
        for c in range(tm // LN_ROWS):
            sl = slice(c * LN_ROWS, (c + 1) * LN_ROWS)
            rv = alpha * x_ref[sl, :] + gate * y_prev[sl, :]
            xn = _layer_norm_rows(rv) * lg_ref[...] + lb_ref[...]
            o_ref[sl, :] = xn
            if emit_h:
                h_ref[sl, :] = (_layer_norm_rows(xn) * scale1 + shift).astype(BF16)

        s = of_ref[...] + ob_ref[...]
        parts = []
        for h in range(GLA_HEADS):
            sh = s[:, h * dv:(h + 1) * dv]
            ms = jnp.mean(sh * sh, axis=-1, keepdims=True)
            parts.append(sh * lax.rsqrt(ms + RMS_EPS))
        out_b = jnp.concatenate(parts, axis=1) * gn_ref[...] * _silu(zb_ref[...])
        y_cur[...] = (_dot(oa_ref[...], wa_ref[...]) + _dot(out_b.astype(BF16), wb_ref[...])
                      + _dot(oc_ref[...], wc_ref[...]))

    @pl.when(step % 2 == 0)
    def _():
        run(y1_ref, y0_ref)

    @pl.when(step % 2 == 1)
    def _():
        run(y0_ref, y1_ref)


def _out_call(oa, o_f, o_b, p_b, oc, x2, gate, gla_norm, w_out, ln_g, ln_b, next_mod, rows_per_gate, zb_col,
              alpha, name):
    m, d = x2.shape
    d_a, dvh, d_c = oa.shape[1], o_f.shape[1], oc.shape[1]
    tm = min(OUT_TM, m)
    per = rows_per_gate // tm
    emit_h = next_mod is not None
    wa = w_out[:d_a].astype(BF16)
    wb = w_out[d_a:d_a + dvh].astype(BF16)
    wc = w_out[d_a + dvh:].astype(BF16)
    gn = jnp.tile(gla_norm, GLA_HEADS).reshape(1, dvh)
    nt = m // tm
    cur = lambda i: jnp.minimum(i, nt - 1)
    prev = lambda i: jnp.maximum(i - 1, 0)
    full = lambda shape: pl.BlockSpec(shape, lambda i: (0,) * len(shape))
    rows = pl.BlockSpec((tm, d), lambda i: (prev(i), 0))
    in_specs = [
        pl.BlockSpec((tm, d_a), lambda i: (cur(i), 0)),
        pl.BlockSpec((tm, dvh), lambda i: (cur(i), 0)),
        pl.BlockSpec((tm, dvh), lambda i: (cur(i), 0)),
        pl.BlockSpec((tm, dvh), lambda i: (cur(i), zb_col // dvh)),
        pl.BlockSpec((tm, d_c), lambda i: (cur(i), 0)),
        rows,
        pl.BlockSpec((1, 1, d), lambda i: (prev(i) // per, 0, 0)),
        full((1, dvh)),
        full((d_a, d)),
        full((dvh, d)),
        full((d_c, d)),
        full((1, d)),
        full((1, d)),
    ]
    args = [oa, o_f, o_b, p_b, oc, x2, gate, gn, wa, wb, wc, ln_g.reshape(1, d), ln_b.reshape(1, d)]
    out_specs = rows
    out_shape = jax.ShapeDtypeStruct((m, d), F32)
    if emit_h:
        in_specs.append(pl.BlockSpec((1, 2, d), lambda i: (prev(i) // per, 0, 0)))
        args.append(next_mod)
        out_specs = [rows, rows]
        out_shape = [out_shape, jax.ShapeDtypeStruct((m, d), BF16)]
    return pl.pallas_call(
        functools.partial(_out_kernel, alpha=alpha, emit_h=emit_h),
        grid=(nt + 1,),
        in_specs=in_specs,
        out_specs=out_specs,
        out_shape=out_shape,
        scratch_shapes=[pltpu.VMEM((tm, d), F32), pltpu.VMEM((tm, d), F32)],
        compiler_params=_params("arbitrary"),
        name=name,
    )(*args)


def kernel(x, c, ctx, c_ctx, w_ada, b_ada, w_in, rpb, gla_w2, gla_b, gla_norm, conv_w, conv_b,
           conv_ln_g, conv_ln_b, w_out, post_ln_g, post_ln_b):
    b, l, d = x.shape
    lc = ctx.shape[1]
    depth = w_ada.shape[0]
    heads_a = rpb.shape[1]
    d_a = heads_a * NA_HEAD_DIM
    dkh = gla_w2.shape[-1]
    dvh = gla_norm.shape[-1] * GLA_HEADS
    d_c = conv_w.shape[-1]
    n_a = 4 * d_a
    n_b = 2 * dkh + 2 * dvh + 2 * GATE_RANK
    n_ac = n_a + 3 * d_c
    n_bp = -(-n_b // LANES) * LANES
    alpha = (2 * depth) ** 0.25
    assert w_in.shape[-1] == n_a + n_b + 3 * d_c and d_a + dvh + d_c == w_out.shape[1]

    cond = jnp.zeros((8, d), F32).at[:b].set(c).at[b].set(c_ctx)
    mod = _ada_call(cond, w_ada, b_ada)

    cos, sin_signed = _rope_tables(l, dkh // GLA_HEADS)
    ones_c = jnp.ones((lc, dkh), F32)
    zeros_c = jnp.zeros((lc, dkh), F32)
    tn_ac = n_ac // 4

    def layer_mods(layer):
        shift, scale, gate = jnp.split(mod[layer], 3, axis=-1)
        mod_lat = jnp.stack([shift[:b], scale[:b]], axis=1)
        mod_ctx = jnp.stack([shift[b:b + 1], scale[b:b + 1]], axis=1)
        return mod_lat, mod_ctx, gate[:b].reshape(b, 1, d), gate[b:b + 1].reshape(1, 1, d)

    x2 = x.reshape(b * l, d)
    cx2 = ctx.reshape(b * lc, d)
    mod_lat, mod_ctx, gate_lat, gate_ctx = layer_mods(0)
    h2 = _ln_mod_call(x2, mod_lat, l, "ln_mod")
    hc2 = _ln_mod_call(cx2, mod_ctx, b * lc, "ln_mod_ctx")
    for layer in range(depth):
        last = layer == depth - 1
        if not last:
            next_lat, next_ctx, next_gate_lat, next_gate_ctx = layer_mods(layer + 1)
        else:
            next_lat = next_ctx = None

        wl = w_in[layer]
        w_ac = jnp.concatenate([wl[:, :n_a], wl[:, n_a + n_b:]], axis=1).astype(BF16)
        w_b = jnp.pad(wl[:, n_a:n_a + n_b], ((0, 0), (0, n_bp - n_b))).astype(BF16)

        p_ac = _proj_call(h2, w_ac, tn_ac, BF16, "proj_ac").reshape(b, l, n_ac)
        p_b = _proj_call(h2, w_b, n_bp, F32, "proj_b").reshape(b, l, n_bp)
        pc_ac = _proj_call(hc2, w_ac, tn_ac, BF16, "proj_ac_ctx").reshape(b, lc, n_ac)
        pc_b = _proj_call(hc2, w_b, n_bp, F32, "proj_b_ctx").reshape(b, lc, n_bp)

        out_a = _na_call(p_ac, pc_ac, rpb[layer], d_a)

        w2p = [jnp.zeros((LANES, dkh), F32).at[i * GATE_RANK:(i + 1) * GATE_RANK].set(gla_w2[layer, i])
               for i in range(2)]
        b2 = [gla_b[layer, i].reshape(1, dkh) for i in range(2)]
        s0 = jnp.zeros((b, dkh, dvh), F32)
        gla = functools.partial(_gla_call, dkh=dkh, dvh=dvh)
        oc_f, s_f = gla(pc_b, ones_c, zeros_c, w2p[0], b2[0], s0, reverse=False, rope=False, name="gla_ctx_fwd")
        oc_b, s_b = gla(pc_b, ones_c, zeros_c, w2p[1], b2[1], s0, reverse=True, rope=False, name="gla_ctx_bwd")
        o_f, _ = gla(p_b, cos, sin_signed, w2p[0], b2[0], s_f, reverse=False, rope=True, name="gla_fwd")
        o_b, _ = gla(p_b, cos, sin_signed, w2p[1], b2[1], s_b, reverse=True, rope=True, name="gla_bwd")

        out_c = _conv_call(p_ac, conv_w[layer], conv_b[layer], conv_ln_g[layer], conv_ln_b[layer],
                           n_a, d_c, "conv")

        zb_col = 2 * dkh + dvh
        res = _out_call(out_a.reshape(b * l, d_a), o_f.reshape(b * l, dvh), o_b.reshape(b * l, dvh),
                        p_b.reshape(b * l, n_bp), out_c.reshape(b * l, d_c), x2, gate_lat,
                        gla_norm[layer], w_out[layer], post_ln_g[layer], post_ln_b[layer], next_lat,
                        l, zb_col, alpha, "out_proj")
        if last:
            x2 = res
        else:
            out_a_c = _ctx_attn_call(pc_ac, d_a)
            out_c_c = _conv_call(pc_ac, conv_w[layer], conv_b[layer], conv_ln_g[layer], conv_ln_b[layer],
                                 n_a, d_c, "conv_ctx")
            cx2, hc2 = _out_call(out_a_c.reshape(b * lc, d_a), oc_f.reshape(b * lc, dvh),
                                 oc_b.reshape(b * lc, dvh), pc_b.reshape(b * lc, n_bp),
                                 out_c_c.reshape(b * lc, d_c), cx2, gate_ctx, gla_norm[layer], w_out[layer],
                                 post_ln_g[layer], post_ln_b[layer], next_ctx, b * lc, zb_col, alpha,
                                 "out_proj_ctx")
            x2, h2 = res
            gate_lat, gate_ctx = next_gate_lat, next_gate_ctx
    return x2.reshape(b, l, d)
```

```python
import functools

import numpy as np
import jax
import jax.numpy as jnp
from jax import lax
from jax.experimental import pallas as pl
from jax.experimental.pallas import tpu as pltpu

F32 = jnp.float32
BF16 = jnp.bfloat16

GRID_W = 64
NA_HEAD_DIM = 64
NA_KH_MAX = 8
NA_KW = 16
GLA_HEADS = 4
GATE_RANK = 16
GATE_TAU = 16.0
GLA_CHUNK = 64
CONV_K = 31
ROPE_BASE = 10000.0
LN_EPS = 1e-5
RMS_EPS = 1e-6

LANES = 128
SUBLANES = 8
VMEM_LIMIT_BYTES = 56 * 1024 * 1024

NEG_BIG = -1e30

PROJ_TM = 1024
OUT_TM = 256
LN_TM = 512
LN_ROWS = 32
LN_UNROLL = 4
ATT_TQ_ROWS = 8
ATT_ITEM_ROWS = 2
GLA_TG = 256
CONV_T = 512
CONV_HALO = 16
CONV_RC = 64


def _silu(x):
    return x * jax.nn.sigmoid(x)


def _dot(a, b):
    return jnp.dot(a, b, preferred_element_type=F32)


def _dot_nt(a, b):
    return lax.dot_general(a, b, (((1,), (1,)), ((), ())), preferred_element_type=F32)


def _dot_tn(a, b):
    return lax.dot_general(a, b, (((0,), (0,)), ((), ())), preferred_element_type=F32)


def _shr(x, pow2):
    shift = pow2.bit_length() - 1
    assert 1 << shift == pow2
    return jnp.right_shift(x, shift)


def _split_hi_lo(x):
    hi = x.astype(BF16)
    lo = (x - hi.astype(F32)).astype(BF16)
    return hi, lo


def _params(*sem):
    return pltpu.CompilerParams(dimension_semantics=sem, vmem_limit_bytes=VMEM_LIMIT_BYTES)


def _ada_kernel(c_ref, w_ref, b_ref, o_ref):
    s = _silu(c_ref[...]).astype(BF16)
    o_ref[0] = _dot(s, w_ref[0].astype(BF16)) + b_ref[0]


def _ada_call(cond, w_ada, b_ada, tn=512):
    depth, d, n = w_ada.shape
    rows = cond.shape[0]
    return pl.pallas_call(
        _ada_kernel,
        grid=(depth, n // tn),
        in_specs=[
            pl.BlockSpec((rows, d), lambda l, j: (0, 0)),
            pl.BlockSpec((1, d, tn), lambda l, j: (l, 0, j)),
            pl.BlockSpec((1, 1, tn), lambda l, j: (l, 0, j)),
        ],
        out_specs=pl.BlockSpec((1, rows, tn), lambda l, j: (l, 0, j)),
        out_shape=jax.ShapeDtypeStruct((depth, rows, n), F32),
        compiler_params=_params("parallel", "parallel"),
        name="ada_mod",
    )(cond, w_ada, b_ada.reshape(depth, 1, n))


def _layer_norm_rows(v):
    mu = jnp.mean(v, axis=-1, keepdims=True)
    vc = v - mu
    var = jnp.mean(vc * vc, axis=-1, keepdims=True)
    return vc * lax.rsqrt(var + LN_EPS)


def _ln_mod_kernel(x_ref, mod_ref, h_ref):
    shift = mod_ref[0, 0:1, :]
    scale1 = 1.0 + mod_ref[0, 1:2, :]

    def body(i, carry):
        r0 = pl.multiple_of(i * LN_ROWS, LN_ROWS)
        hn = _layer_norm_rows(x_ref[pl.ds(r0, LN_ROWS), :])
        h_ref[pl.ds(r0, LN_ROWS), :] = (hn * scale1 + shift).astype(BF16)
        return carry

    lax.fori_loop(0, x_ref.shape[0] // LN_ROWS, body, 0, unroll=LN_UNROLL)


def _ln_mod_call(x2, mod, rows_per_mod, name):
    m, d = x2.shape
    tm = min(LN_TM, m)
    per = rows_per_mod // tm
    return pl.pallas_call(
        _ln_mod_kernel,
        grid=(m // tm,),
        in_specs=[
            pl.BlockSpec((tm, d), lambda i: (i, 0)),
            pl.BlockSpec((1, 2, d), lambda i: (i // per, 0, 0)),
        ],
        out_specs=pl.BlockSpec((tm, d), lambda i: (i, 0)),
        out_shape=jax.ShapeDtypeStruct((m, d), BF16),
        compiler_params=_params("parallel"),
        name=name,
    )(x2, mod)


def _matmul_kernel(h_ref, w_ref, o_ref):
    o_ref[...] = _dot(h_ref[...], w_ref[...]).astype(o_ref.dtype)


def _proj_call(h2, w, tn, out_dtype, name):
    m, d = h2.shape
    n = w.shape[1]
    tm = min(PROJ_TM, m)
    return pl.pallas_call(
        _matmul_kernel,
        grid=(m // tm, n // tn),
        in_specs=[
            pl.BlockSpec((tm, d), lambda i, j: (i, 0)),
            pl.BlockSpec((d, tn), lambda i, j: (0, j)),
        ],
        out_specs=pl.BlockSpec((tm, tn), lambda i, j: (i, j)),
        out_shape=jax.ShapeDtypeStruct((m, n), out_dtype),
        compiler_params=_params("parallel", "arbitrary"),
        name=name,
    )(h2, w)


def _softmax_pv(q, lane_lo, pieces):
    outs = []
    for h in range(2):
        keep = lane_lo if h == 0 else jnp.logical_not(lane_lo)
        qh = jnp.where(keep, q, jnp.zeros_like(q)) * jnp.asarray(NA_HEAD_DIM ** -0.5, BF16)
        scores = []
        for k, _, bias in pieces:
            s = _dot_nt(qh, k)
            if bias is not None:
                s = s + bias[h]
            scores.append(s)
        m = scores[0].max(axis=-1, keepdims=True)
        for s in scores[1:]:
            m = jnp.maximum(m, s.max(axis=-1, keepdims=True))
        den = None
        acc = None
        for s, (_, v, _) in zip(scores, pieces):
            p = jnp.exp(s - m)
            ps = p.sum(axis=-1, keepdims=True)
            pv = _dot(p.astype(BF16), v)
            den = ps if den is None else den + ps
            acc = pv if acc is None else acc + pv
        outs.append(acc / den)
    return jnp.where(lane_lo, outs[0], outs[1])


def _na_kernel(q_ref, k_ref, v_ref, z_ref, kc_ref, vc_ref, bias_ref, o_ref, *, rows):
    tq = q_ref.shape[1]
    kh = min(NA_KH_MAX, rows)
    nkeys = kh * GRID_W
    qi = pl.program_id(2)
    lane_lo = lax.broadcasted_iota(jnp.int32, (tq, LANES), 1) < NA_HEAD_DIM
    q = q_ref[0]
    kc = kc_ref[0]
    vc = vc_ref[0]
    starts, d0s = [], []
    for i in range(ATT_TQ_ROWS):
        r = qi * ATT_TQ_ROWS + i
        rs = jnp.clip(r - kh // 2, 0, rows - kh)
        d0s.append(rs - r + (NA_KH_MAX - 1))
        starts.append(pl.multiple_of(rs * GRID_W, GRID_W))
    qhs = []
    for h in range(2):
        keep = lane_lo if h == 0 else jnp.logical_not(lane_lo)
        qhs.append(jnp.where(keep, q, jnp.zeros_like(q)) * jnp.asarray(NA_HEAD_DIM ** -0.5, BF16))

    def stage_scores(h, rows_i):
        qs = qhs[h][rows_i[0] * GRID_W:(rows_i[-1] + 1) * GRID_W]
        s_loc = jnp.concatenate(
            [_dot_nt(qhs[h][i * GRID_W:(i + 1) * GRID_W], k_ref[0, pl.ds(starts[i], nkeys), :])
             + bias_ref[d0s[i], 0, h] for i in rows_i], axis=0)
        return s_loc, _dot_nt(qs, kc)

    def stage_softmax(s_loc, s_ctx):
        m = jnp.maximum(s_loc.max(axis=-1, keepdims=True), s_ctx.max(axis=-1, keepdims=True))
        p_loc = jnp.exp(s_loc - m)
        p_ctx = jnp.exp(s_ctx - m)
        den = p_loc.sum(axis=-1, keepdims=True) + p_ctx.sum(axis=-1, keepdims=True)
        return p_loc.astype(BF16), p_ctx.astype(BF16), den

    def stage_values(p_loc, p_ctx, den, rows_i):
        acc = jnp.concatenate(
            [_dot(p_loc[j * GRID_W:(j + 1) * GRID_W], v_ref[0, pl.ds(starts[i], nkeys), :])
             for j, i in enumerate(rows_i)], axis=0)
        return (acc + _dot(p_ctx, vc)) / den

    groups = [list(range(g, g + ATT_ITEM_ROWS)) for g in range(0, ATT_TQ_ROWS, ATT_ITEM_ROWS)]
    items = [(h, g) for h in range(2) for g in groups]
    scores, probs, outs = {}, {}, {}
    for t in range(len(items) + 2):
        if t < len(items):
            scores[t] = stage_scores(*items[t])
        if 0 <= t - 1 < len(items):
            probs[t - 1] = stage_softmax(*scores.pop(t - 1))
        if 0 <= t - 2 < len(items):
            outs[t - 2] = stage_values(*probs.pop(t - 2), items[t - 2][1])
    per_head = [jnp.concatenate([outs[t] for t, (hh, _) in enumerate(items) if hh == h], axis=0)
                for h in range(2)]
    o = jnp.where(lane_lo, per_head[0], per_head[1])
    o_ref[0] = (o * _silu(z_ref[0].astype(F32))).astype(o_ref.dtype)


def _na_bias_table(rpb, rows):
    depth, heads = rpb.shape[:2]
    kh = min(NA_KH_MAX, rows)
    nd = 2 * NA_KH_MAX - kh
    for r in range(rows):
        assert 0 <= int(np.clip(r - kh // 2, 0, rows - kh)) - r + (NA_KH_MAX - 1) < nd

    c = np.arange(GRID_W)[:, None]
    kc = np.arange(GRID_W)[None, :]
    cs = np.clip(c - NA_KW // 2, 0, GRID_W - NA_KW)
    valid_c = (kc >= cs) & (kc < cs + NA_KW)
    dcol = kc - c + (NA_KW - 1)
    sel_col = ((dcol[None] == np.arange(2 * NA_KW - 1)[:, None, None]) & valid_c[None]).astype(np.float32)
    d0 = np.arange(nd)[:, None, None]
    j = np.arange(kh)[None, :, None]
    sel_row = (np.arange(2 * NA_KH_MAX - 1)[None, None, :] == d0 + j).astype(np.float32)
    table = jnp.einsum("lhde,pjd,eck->lphcjk", rpb, sel_row, sel_col, precision=lax.Precision.HIGHEST)
    table = jnp.where(valid_c[:, None, :], table, NEG_BIG)
    return table.reshape(depth, nd, heads // 2, 2, GRID_W, kh * GRID_W)


def _na_call(p_lat, p_ctx, bias, d_a):
    b, l, _ = p_lat.shape
    lc = p_ctx.shape[1]
    rows = l // GRID_W
    assert rows % ATT_TQ_ROWS == 0
    tq = ATT_TQ_ROWS * GRID_W
    nq = l // tq
    npair = d_a // LANES
    nd, _, _, _, nkeys = bias.shape

    return pl.pallas_call(
        functools.partial(_na_kernel, rows=rows),
        grid=(b, npair, nq),
        in_specs=[
            pl.BlockSpec((1, tq, LANES), lambda i, p, q: (i, q, p)),
            pl.BlockSpec((1, l, LANES), lambda i, p, q: (i, 0, npair + p)),
            pl.BlockSpec((1, l, LANES), lambda i, p, q: (i, 0, 2 * npair + p)),
            pl.BlockSpec((1, tq, LANES), lambda i, p, q: (i, q, 3 * npair + p)),
            pl.BlockSpec((1, lc, LANES), lambda i, p, q: (i, 0, npair + p)),
            pl.BlockSpec((1, lc, LANES), lambda i, p, q: (i, 0, 2 * npair + p)),
            pl.BlockSpec((nd, 1, 2, GRID_W, nkeys), lambda i, p, q: (0, p, 0, 0, 0)),
        ],
        out_specs=pl.BlockSpec((1, tq, LANES), lambda i, p, q: (i, q, p)),
        out_shape=jax.ShapeDtypeStruct((b, l, d_a), BF16),
        compiler_params=_params("parallel", "parallel", "arbitrary"),
        name="na_attention",
    )(p_lat, p_lat, p_lat, p_lat, p_ctx, p_ctx, bias)


def _ctx_attn_kernel(q_ref, k_ref, v_ref, z_ref, o_ref):
    tq = q_ref.shape[1]
    lane_lo = lax.broadcasted_iota(jnp.int32, (tq, LANES), 1) < NA_HEAD_DIM
    o = _softmax_pv(q_ref[0], lane_lo, [(k_ref[0], v_ref[0], None)])
    o_ref[0] = (o * _silu(z_ref[0].astype(F32))).astype(o_ref.dtype)


def _ctx_attn_call(p_ctx, d_a):
    b, lc, _ = p_ctx.shape
    npair = d_a // LANES
    return pl.pallas_call(
        _ctx_attn_kernel,
        grid=(b, npair),
        in_specs=[
            pl.BlockSpec((1, lc, LANES), lambda i, p: (i, 0, p)),
            pl.BlockSpec((1, lc, LANES), lambda i, p: (i, 0, npair + p)),
            pl.BlockSpec((1, lc, LANES), lambda i, p: (i, 0, 2 * npair + p)),
            pl.BlockSpec((1, lc, LANES), lambda i, p: (i, 0, 3 * npair + p)),
        ],
        out_specs=pl.BlockSpec((1, lc, LANES), lambda i, p: (i, 0, p)),
        out_shape=jax.ShapeDtypeStruct((b, lc, d_a), BF16),
        compiler_params=_params("parallel", "parallel"),
        name="ctx_attention",
    )(p_ctx, p_ctx, p_ctx, p_ctx)


def _rope(x, cos, sin_signed):
    halves = []
    for j in range(x.shape[1] // LANES):
        xs = x[:, j * LANES:(j + 1) * LANES]
        lane = lax.broadcasted_iota(jnp.int32, xs.shape, 1)
        first = (lane & 16) == 0
        partner = jnp.where(first, pltpu.roll(xs, LANES - 16, 1), pltpu.roll(xs, 16, 1))
        halves.append(partner)
    partner = jnp.concatenate(halves, axis=1)
    return x * cos + partner * sin_signed


def _gla_kernel(q_ref, k_ref, v_ref, lr_ref, cos_ref, sin_ref, w2_ref, b2_ref, s0_ref,
                o_ref, sfin_ref, s_ref, *, reverse, rope):
    tg = q_ref.shape[1]
    dkh = q_ref.shape[2]
    dvh = v_ref.shape[2]
    dk = dkh // GLA_HEADS
    dv = dvh // GLA_HEADS
    nchunk = tg // GLA_CHUNK
    step = pl.program_id(1)

    @pl.when(step == 0)
    def _():
        s_ref[...] = s0_ref[0]

    lr_hi, lr_lo = _split_hi_lo(lr_ref[0])
    w_hi, w_lo = _split_hi_lo(w2_ref[...])
    logits = _dot(lr_hi, w_hi) + _dot(lr_lo, w_hi) + _dot(lr_hi, w_lo) + b2_ref[...]
    g = (jnp.minimum(logits, 0.0) - jnp.log1p(jnp.exp(-jnp.abs(logits)))) * (1.0 / GATE_TAU)

    r = lax.broadcasted_iota(jnp.int32, (tg, tg), 0)
    c = lax.broadcasted_iota(jnp.int32, (tg, tg), 1)
    same = _shr(r, GLA_CHUNK) == _shr(c, GLA_CHUNK)
    upto = (c >= r) if reverse else (c <= r)
    t_inc = jnp.where(same & upto, 1.0, 0.0).astype(BF16)
    g_hi, g_lo = _split_hi_lo(g)
    b_inc = _dot(t_inc, g_hi) + _dot(t_inc, g_lo)

    chunks = [slice(ci * GLA_CHUNK, (ci + 1) * GLA_CHUNK) for ci in range(nchunk)]
    end_row = 0 if reverse else GLA_CHUNK - 1
    b_end = [b_inc[sl][end_row:end_row + 1, :] for sl in chunks]
    b_rest = jnp.concatenate([be - b_inc[sl] for be, sl in zip(b_end, chunks)], axis=0)

    q = q_ref[0]
    k = k_ref[0]
    if rope:
        q = _rope(q, cos_ref[...], sin_ref[...])
        k = _rope(k, cos_ref[...], sin_ref[...])
    q = q * (dk ** -0.5)
    qt = (q * jnp.exp(b_inc)).astype(BF16)
    kt = (k * jnp.exp(-b_inc)).astype(BF16)
    kd = (k * jnp.exp(b_rest)).astype(BF16)
    vb = v_ref[0].astype(BF16)

    lane_head = _shr(lax.broadcasted_iota(jnp.int32, (GLA_CHUNK, dkh), 1), dk)
    hc = GLA_HEADS * GLA_CHUNK
    rk = _shr(lax.broadcasted_iota(jnp.int32, (hc, dkh), 0), GLA_CHUNK)
    ck = _shr(lax.broadcasted_iota(jnp.int32, (hc, dkh), 1), dk)
    mask_k = rk == ck
    rv = _shr(lax.broadcasted_iota(jnp.int32, (hc, dvh), 0), GLA_CHUNK)
    cv = _shr(lax.broadcasted_iota(jnp.int32, (hc, dvh), 1), dv)
    mask_v = rv == cv
    ti = lax.broadcasted_iota(jnp.int32, (GLA_CHUNK, dkh), 0)
    tj = lax.broadcasted_iota(jnp.int32, (GLA_CHUNK, dkh), 1) & (GLA_CHUNK - 1)
    causal = (tj >= ti) if reverse else (tj <= ti)
    zero = jnp.zeros((), BF16)

    def head_stack(x):
        return jnp.concatenate([jnp.where(lane_head == h, x, zero) for h in range(GLA_HEADS)], axis=0)

    o_intra, kv, decay, q_stack = [], [], [], []
    for sl, be in zip(chunks, b_end):
        k_blk = jnp.where(mask_k, jnp.concatenate([kt[sl]] * GLA_HEADS, axis=0), zero)
        att = jnp.where(causal, _dot_nt(qt[sl], k_blk), 0.0)
        v_blk = jnp.where(mask_v, jnp.concatenate([vb[sl]] * GLA_HEADS, axis=0), zero)
        o_intra.append(_dot(att.astype(BF16), v_blk))
        v_rows = jnp.concatenate([vb[sl][:, h * dv:(h + 1) * dv] for h in range(GLA_HEADS)], axis=0)
        kv.append(_dot_tn(head_stack(kd[sl]), v_rows))
        col = jnp.exp(jnp.transpose(jnp.broadcast_to(be, (LANES, dkh))))
        decay.append(jnp.concatenate([col] * (dv // LANES), axis=1))
        q_stack.append(head_stack(qt[sl]))

    order = list(range(nchunk - 1, -1, -1)) if reverse else list(range(nchunk))
    state = s_ref[...]
    states = {}
    for ci in order:
        states[ci] = state
        state = decay[ci] * state + kv[ci]
    s_ref[...] = state

    for ci in range(nchunk):
        inter = _dot(q_stack[ci], states[ci].astype(BF16))
        inter = jnp.concatenate([inter[h * GLA_CHUNK:(h + 1) * GLA_CHUNK] for h in range(GLA_HEADS)], axis=1)
        o_ref[0, chunks[ci], :] = o_intra[ci] + inter

    @pl.when(step == pl.num_programs(1) - 1)
    def _():
        sfin_ref[0] = state


def _gla_call(p_b, cos, sin_signed, w2p, b2, s0, *, reverse, rope, dkh, dvh, name):
    b, t, _ = p_b.shape
    tg = min(GLA_TG, t)
    ng = t // tg
    lr_blk = (2 * dkh + 2 * dvh) // LANES
    dv = dvh // GLA_HEADS
    assert dv % LANES == 0

    def blk(i):
        return ng - 1 - i if reverse else i

    kern = functools.partial(_gla_kernel, reverse=reverse, rope=rope)
    return pl.pallas_call(
        kern,
        grid=(b, ng),
        in_specs=[
            pl.BlockSpec((1, tg, dkh), lambda n, i: (n, blk(i), 0)),
            pl.BlockSpec((1, tg, dkh), lambda n, i: (n, blk(i), 1)),
            pl.BlockSpec((1, tg, dvh), lambda n, i: (n, blk(i), 2 * dkh // dvh)),
            pl.BlockSpec((1, tg, LANES), lambda n, i: (n, blk(i), lr_blk)),
            pl.BlockSpec((tg, dkh), lambda n, i: (blk(i), 0)),
            pl.BlockSpec((tg, dkh), lambda n, i: (blk(i), 0)),
            pl.BlockSpec((LANES, dkh), lambda n, i: (0, 0)),
            pl.BlockSpec((1, dkh), lambda n, i: (0, 0)),
            pl.BlockSpec((1, dkh, dv), lambda n, i: (n, 0, 0)),
        ],
        out_specs=[
            pl.BlockSpec((1, tg, dvh), lambda n, i: (n, blk(i), 0)),
            pl.BlockSpec((1, dkh, dv), lambda n, i: (n, 0, 0)),
        ],
        out_shape=[
            jax.ShapeDtypeStruct((b, t, dvh), F32),
            jax.ShapeDtypeStruct((b, dkh, dv), F32),
        ],
        scratch_shapes=[pltpu.VMEM((dkh, dv), F32)],
        compiler_params=_params("parallel", "arbitrary"),
        name=name,
    )(p_b, p_b, p_b, p_b, cos, sin_signed, w2p, b2, s0)


def _rope_tables(l, dk):
    quarter = dk // 4
    pos = jnp.arange(l)
    rows_pos = (pos // GRID_W).astype(F32)
    cols_pos = (pos % GRID_W).astype(F32)
    inv = ROPE_BASE ** (-jnp.arange(0, 2 * quarter, 2, dtype=F32) / (2 * quarter))
    ang_r = rows_pos[:, None] * inv[None, :]
    ang_c = cols_pos[:, None] * inv[None, :]
    cos = jnp.concatenate([jnp.cos(ang_r)] * 2 + [jnp.cos(ang_c)] * 2, axis=1)
    sin = jnp.concatenate([-jnp.sin(ang_r), jnp.sin(ang_r), -jnp.sin(ang_c), jnp.sin(ang_c)], axis=1)
    return jnp.tile(cos, (1, GLA_HEADS)), jnp.tile(sin, (1, GLA_HEADS))


def _conv_kernel(a_ref, g_ref, z_ref, ap_ref, gp_ref, an_ref, gn_ref, w_ref, cb_ref, lg_ref, lb_ref,
                 o_ref, u_ref, us_ref):
    t = a_ref.shape[1]
    nshift = u_ref.shape[0] - SUBLANES
    step = pl.program_id(1)
    nstep = pl.num_programs(1)

    def glu(a, g):
        return a.astype(F32) * jax.nn.sigmoid(g.astype(F32))

    prev_ok = (step > 0).astype(F32)
    next_ok = (step < nstep - 1).astype(F32)
    u_ref[0:CONV_HALO, :] = glu(ap_ref[0], gp_ref[0]) * prev_ok
    u_ref[CONV_HALO:CONV_HALO + t, :] = glu(a_ref[0], g_ref[0])
    u_ref[CONV_HALO + t:2 * CONV_HALO + t, :] = glu(an_ref[0], gn_ref[0]) * next_ok

    for b in range(1, SUBLANES):
        us_ref[b - 1] = u_ref[b:b + nshift, :]

    off = CONV_HALO - CONV_K // 2
    for rc in range(t // CONV_RC):
        base = rc * CONV_RC
        acc = None
        for j in range(CONV_K):
            q8, b = divmod(off + j, SUBLANES)
            lo = base + q8 * SUBLANES
            src = u_ref[lo:lo + CONV_RC, :] if b == 0 else us_ref[b - 1, lo:lo + CONV_RC, :]
            term = src * w_ref[j:j + 1, :]
            acc = term if acc is None else acc + term
        acc = acc + cb_ref[...]
        mu = jnp.mean(acc, axis=-1, keepdims=True)
        xc = acc - mu
        var = jnp.mean(xc * xc, axis=-1, keepdims=True)
        y = xc * lax.rsqrt(var + LN_EPS) * lg_ref[...] + lb_ref[...]
        z = z_ref[0, base:base + CONV_RC, :].astype(F32)
        o_ref[0, base:base + CONV_RC, :] = (_silu(y) * _silu(z)).astype(o_ref.dtype)


def _conv_call(p, conv_w, conv_b, ln_g, ln_b, col0, d_c, name):
    b, t, _ = p.shape
    tt = min(CONV_T, t)
    nt = t // tt
    cb = col0 // d_c
    hb = tt // CONV_HALO
    nhalo = t // CONV_HALO
    w = jnp.zeros((CONV_K + 1, d_c), F32).at[:CONV_K].set(conv_w)

    def prev(i):
        return jnp.maximum(i * hb - 1, 0)

    def nxt(i):
        return jnp.minimum((i + 1) * hb, nhalo - 1)

    row = lambda v: v.reshape(1, d_c)
    return pl.pallas_call(
        _conv_kernel,
        grid=(b, nt),
        in_specs=[
            pl.BlockSpec((1, tt, d_c), lambda n, i: (n, i, cb)),
            pl.BlockSpec((1, tt, d_c), lambda n, i: (n, i, cb + 1)),
            pl.BlockSpec((1, tt, d_c), lambda n, i: (n, i, cb + 2)),
            pl.BlockSpec((1, CONV_HALO, d_c), lambda n, i: (n, prev(i), cb)),
            pl.BlockSpec((1, CONV_HALO, d_c), lambda n, i: (n, prev(i), cb + 1)),
            pl.BlockSpec((1, CONV_HALO, d_c), lambda n, i: (n, nxt(i), cb)),
            pl.BlockSpec((1, CONV_HALO, d_c), lambda n, i: (n, nxt(i), cb + 1)),
            pl.BlockSpec((CONV_K + 1, d_c), lambda n, i: (0, 0)),
            pl.BlockSpec((1, d_c), lambda n, i: (0, 0)),
            pl.BlockSpec((1, d_c), lambda n, i: (0, 0)),
            pl.BlockSpec((1, d_c), lambda n, i: (0, 0)),
        ],
        out_specs=pl.BlockSpec((1, tt, d_c), lambda n, i: (n, i, 0)),
        out_shape=jax.ShapeDtypeStruct((b, t, d_c), BF16),
        scratch_shapes=[pltpu.VMEM((tt + 2 * CONV_HALO, d_c), F32),
                        pltpu.VMEM((SUBLANES - 1, tt + 2 * CONV_HALO - SUBLANES, d_c), F32)],
        compiler_params=_params("parallel", "arbitrary"),
        name=name,
    )(p, p, p, p, p, p, p, w, row(conv_b), row(ln_g), row(ln_b))


def _out_kernel(oa_ref, of_ref, ob_ref, zb_ref, oc_ref, x_ref, gate_ref, gn_ref,
                wa_ref, wb_ref, wc_ref, lg_ref, lb_ref, *rest, alpha, emit_h):
    if emit_h:
        mod_ref, o_ref, h_ref, y0_ref, y1_ref = rest
    else:
        o_ref, y0_ref, y1_ref = rest
    tm = x_ref.shape[0]
    dvh = of_ref.shape[1]
    dv = dvh // GLA_HEADS
    step = pl.program_id(0)

    @pl.when(step == 0)
    def _():
        y1_ref[...] = jnp.zeros_like(y1_ref)

    def run(y_prev, y_cur):
        gate = gate_ref[0]
        if emit_h:
            shift = mod_ref[0, 0:1, :]
            scale1 = 1.0 + mod_ref[0, 1:2, :]
        for c in range(tm // LN_ROWS):
            sl = slice(c * LN_ROWS, (c + 1) * LN_ROWS)
            rv = alpha * x_ref[sl, :] + gate * y_prev[sl, :]
            xn = _layer_norm_rows(rv) * lg_ref[...] + lb_ref[...]
            o_ref[sl, :] = xn
            if emit_h:
                h_ref[sl, :] = (_layer_norm_rows(xn) * scale1 + shift).astype(BF16)

        s = of_ref[...] + ob_ref[...]
        parts = []
        for h in range(GLA_HEADS):
            sh = s[:, h * dv:(h + 1) * dv]
            ms = jnp.mean(sh * sh, axis=-1, keepdims=True)
            parts.append(sh * lax.rsqrt(ms + RMS_EPS))
        out_b = jnp.concatenate(parts, axis=1) * gn_ref[...] * _silu(zb_ref[...])
        y_cur[...] = (_dot(oa_ref[...], wa_ref[...]) + _dot(out_b.astype(BF16), wb_ref[...])
                      + _dot(oc_ref[...], wc_ref[...]))

    @pl.when(step % 2 == 0)
    def _():
        run(y1_ref, y0_ref)

    @pl.when(step % 2 == 1)
    def _():
        run(y0_ref, y1_ref)


def _out_call(oa, o_f, o_b, p_b, oc, x2, gate, gla_norm, w_out, ln_g, ln_b, next_mod, rows_per_gate, zb_col,
              alpha, name):
    m, d = x2.shape
    d_a, dvh, d_c = oa.shape[1], o_f.shape[1], oc.shape[1]
    tm = min(OUT_TM, m)
    per = rows_per_gate // tm
    emit_h = next_mod is not None
    wa = w_out[:d_a].astype(BF16)
    wb = w_out[d_a:d_a + dvh].astype(BF16)
    wc = w_out[d_a + dvh:].astype(BF16)
    gn = jnp.tile(gla_norm, GLA_HEADS).reshape(1, dvh)
    nt = m // tm
    cur = lambda i: jnp.minimum(i, nt - 1)
    prev = lambda i: jnp.maximum(i - 1, 0)
    full = lambda shape: pl.BlockSpec(shape, lambda i: (0,) * len(shape))
    rows = pl.BlockSpec((tm, d), lambda i: (prev(i), 0))
    in_specs = [
        pl.BlockSpec((tm, d_a), lambda i: (cur(i), 0)),
        pl.BlockSpec((tm, dvh), lambda i: (cur(i), 0)),
        pl.BlockSpec((tm, dvh), lambda i: (cur(i), 0)),
        pl.BlockSpec((tm, dvh), lambda i: (cur(i), zb_col // dvh)),
        pl.BlockSpec((tm, d_c), lambda i: (cur(i), 0)),
        rows,
        pl.BlockSpec((1, 1, d), lambda i: (prev(i) // per, 0, 0)),
        full((1, dvh)),
        full((d_a, d)),
        full((dvh, d)),
        full((d_c, d)),
        full((1, d)),
        full((1, d)),
    ]
    args = [oa, o_f, o_b, p_b, oc, x2, gate, gn, wa, wb, wc, ln_g.reshape(1, d), ln_b.reshape(1, d)]
    out_specs = rows
    out_shape = jax.ShapeDtypeStruct((m, d), F32)
    if emit_h:
        in_specs.append(pl.BlockSpec((1, 2, d), lambda i: (prev(i) // per, 0, 0)))
        args.append(next_mod)
        out_specs = [rows, rows]
        out_shape = [out_shape, jax.ShapeDtypeStruct((m, d), BF16)]
    return pl.pallas_call(
        functools.partial(_out_kernel, alpha=alpha, emit_h=emit_h),
        grid=(nt + 1,),
        in_specs=in_specs,
        out_specs=out_specs,
        out_shape=out_shape,
        scratch_shapes=[pltpu.VMEM((tm, d), F32), pltpu.VMEM((tm, d), F32)],
        compiler_params=_params("arbitrary"),
        name=name,
    )(*args)


def kernel(x, c, ctx, c_ctx, w_ada, b_ada, w_in, rpb, gla_w2, gla_b, gla_norm, conv_w, conv_b,
           conv_ln_g, conv_ln_b, w_out, post_ln_g, post_ln_b):
    b, l, d = x.shape
    lc = ctx.shape[1]
    depth = w_ada.shape[0]
    heads_a = rpb.shape[1]
    d_a = heads_a * NA_HEAD_DIM
    dkh = gla_w2.shape[-1]
    dvh = gla_norm.shape[-1] * GLA_HEADS
    d_c = conv_w.shape[-1]
    n_a = 4 * d_a
    n_b = 2 * dkh + 2 * dvh + 2 * GATE_RANK
    n_ac = n_a + 3 * d_c
    n_bp = -(-n_b // LANES) * LANES
    alpha = (2 * depth) ** 0.25
    assert w_in.shape[-1] == n_a + n_b + 3 * d_c and d_a + dvh + d_c == w_out.shape[1]

    cond = jnp.zeros((8, d), F32).at[:b].set(c).at[b].set(c_ctx)
    mod = _ada_call(cond, w_ada, b_ada)

    w_bf = w_in.astype(BF16)
    w_ac_all = jnp.concatenate([w_bf[:, :, :n_a], w_bf[:, :, n_a + n_b:]], axis=2)
    w_b_all = jnp.pad(w_bf[:, :, n_a:n_a + n_b], ((0, 0), (0, 0), (0, n_bp - n_b)))
    na_bias = _na_bias_table(rpb, l // GRID_W)

    cos, sin_signed = _rope_tables(l, dkh // GLA_HEADS)
    ones_c = jnp.ones((lc, dkh), F32)
    zeros_c = jnp.zeros((lc, dkh), F32)
    tn_ac = n_ac // 4

    def layer_mods(layer):
        shift, scale, gate = jnp.split(mod[layer], 3, axis=-1)
        mod_lat = jnp.stack([shift[:b], scale[:b]], axis=1)
        mod_ctx = jnp.stack([shift[b:b + 1], scale[b:b + 1]], axis=1)
        return mod_lat, mod_ctx, gate[:b].reshape(b, 1, d), gate[b:b + 1].reshape(1, 1, d)

    x2 = x.reshape(b * l, d)
    cx2 = ctx.reshape(b * lc, d)
    mod_lat, mod_ctx, gate_lat, gate_ctx = layer_mods(0)
    h2 = _ln_mod_call(x2, mod_lat, l, "ln_mod")
    hc2 = _ln_mod_call(cx2, mod_ctx, b * lc, "ln_mod_ctx")
    for layer in range(depth):
        last = layer == depth - 1
        if not last:
            next_lat, next_ctx, next_gate_lat, next_gate_ctx = layer_mods(layer + 1)
        else:
            next_lat = next_ctx = None

        w_ac = w_ac_all[layer]
        w_b = w_b_all[layer]

        p_ac = _proj_call(h2, w_ac, tn_ac, BF16, "proj_ac").reshape(b, l, n_ac)
        p_b = _proj_call(h2, w_b, n_bp, F32, "proj_b").reshape(b, l, n_bp)
        pc_ac = _proj_call(hc2, w_ac, tn_ac, BF16, "proj_ac_ctx").reshape(b, lc, n_ac)
        pc_b = _proj_call(hc2, w_b, n_bp, F32, "proj_b_ctx").reshape(b, lc, n_bp)

        out_a = _na_call(p_ac, pc_ac, na_bias[layer], d_a)

        w2p = [jnp.zeros((LANES, dkh), F32).at[i * GATE_RANK:(i + 1) * GATE_RANK].set(gla_w2[layer, i])
               for i in range(2)]
        b2 = [gla_b[layer, i].reshape(1, dkh) for i in range(2)]
        s0 = jnp.zeros((b, dkh, dvh // GLA_HEADS), F32)
        gla = functools.partial(_gla_call, dkh=dkh, dvh=dvh)
        oc_f, s_f = gla(pc_b, ones_c, zeros_c, w2p[0], b2[0], s0, reverse=False, rope=False, name="gla_ctx_fwd")
        oc_b, s_b = gla(pc_b, ones_c, zeros_c, w2p[1], b2[1], s0, reverse=True, rope=False, name="gla_ctx_bwd")
        o_f, _ = gla(p_b, cos, sin_signed, w2p[0], b2[0], s_f, reverse=False, rope=True, name="gla_fwd")
        o_b, _ = gla(p_b, cos, sin_signed, w2p[1], b2[1], s_b, reverse=True, rope=True, name="gla_bwd")

        out_c = _conv_call(p_ac, conv_w[layer], conv_b[layer], conv_ln_g[layer], conv_ln_b[layer],
                           n_a, d_c, "conv")

        zb_col = 2 * dkh + dvh
        res = _out_call(out_a.reshape(b * l, d_a), o_f.reshape(b * l, dvh), o_b.reshape(b * l, dvh),
                        p_b.reshape(b * l, n_bp), out_c.reshape(b * l, d_c), x2, gate_lat,
                        gla_norm[layer], w_out[layer], post_ln_g[layer], post_ln_b[layer], next_lat,
                        l, zb_col, alpha, "out_proj")
        if last:
            x2 = res
        else:
            out_a_c = _ctx_attn_call(pc_ac, d_a)
            out_c_c = _conv_call(pc_ac, conv_w[layer], conv_b[layer], conv_ln_g[layer], conv_ln_b[layer],
                                 n_a, d_c, "conv_ctx")
            cx2, hc2 = _out_call(out_a_c.reshape(b * lc, d_a), oc_f.reshape(b * lc, dvh),
                                 oc_b.reshape(b * lc, dvh), pc_b.reshape(b * lc, n_bp),
                                 out_c_c.reshape(b * lc, d_c), cx2, gate_ctx, gla_norm[layer], w_out[layer],
                                 post_ln_g[layer], post_ln_b[layer], next_ctx, b * lc, zb_col, alpha,
                                 "out_proj_ctx")
            x2, h2 = res
            gate_lat, gate_ctx = next_gate_lat, next_gate_ctx
    return x2.reshape(b, l, d)
```

```python
import functools

import numpy as np
import jax
import jax.numpy as jnp
from jax import lax
from jax.experimental import pallas as pl
from jax.experimental.pallas import tpu as pltpu

F32 = jnp.float32
BF16 = jnp.bfloat16

GRID_W = 64
NA_HEAD_DIM = 64
NA_KH_MAX = 8
NA_KW = 16
GLA_HEADS = 4
GATE_RANK = 16
GATE_TAU = 16.0
GLA_CHUNK = 64
CONV_K = 31
ROPE_BASE = 10000.0
LN_EPS = 1e-5
RMS_EPS = 1e-6

LANES = 128
SUBLANES = 8
VMEM_LIMIT_BYTES = 56 * 1024 * 1024

NEG_BIG = -1e30
LOG2E = 1.4426950408889634

PROJ_TM = 1024
OUT_TM = 256
LN_TM = 512
LN_ROWS = 32
LN_UNROLL = 4
ATT_TQ_ROWS = 16
ATT_ITEM_ROWS = 2
GLA_TG = 256
CONV_T = 512
CONV_HALO = 16
CONV_RC = 64


def _silu(x):
    return x * jax.nn.sigmoid(x)


def _dot(a, b):
    return jnp.dot(a, b, preferred_element_type=F32)


def _dot_nt(a, b):
    return lax.dot_general(a, b, (((1,), (1,)), ((), ())), preferred_element_type=F32)


def _dot_tn(a, b):
    return lax.dot_general(a, b, (((0,), (0,)), ((), ())), preferred_element_type=F32)


def _shr(x, pow2):
    shift = pow2.bit_length() - 1
    assert 1 << shift == pow2
    return jnp.right_shift(x, shift)


def _split_hi_lo(x):
    hi = x.astype(BF16)
    lo = (x - hi.astype(F32)).astype(BF16)
    return hi, lo


def _params(*sem):
    return pltpu.CompilerParams(dimension_semantics=sem, vmem_limit_bytes=VMEM_LIMIT_BYTES)


def _ada_kernel(c_ref, w_ref, b_ref, o_ref):
    s = _silu(c_ref[...]).astype(BF16)
    o_ref[0] = _dot(s, w_ref[0].astype(BF16)) + b_ref[0]


def _ada_call(cond, w_ada, b_ada, tn=512):
    depth, d, n = w_ada.shape
    rows = cond.shape[0]
    return pl.pallas_call(
        _ada_kernel,
        grid=(depth, n // tn),
        in_specs=[
            pl.BlockSpec((rows, d), lambda l, j: (0, 0)),
            pl.BlockSpec((1, d, tn), lambda l, j: (l, 0, j)),
            pl.BlockSpec((1, 1, tn), lambda l, j: (l, 0, j)),
        ],
        out_specs=pl.BlockSpec((1, rows, tn), lambda l, j: (l, 0, j)),
        out_shape=jax.ShapeDtypeStruct((depth, rows, n), F32),
        compiler_params=_params("parallel", "parallel"),
        name="ada_mod",
    )(cond, w_ada, b_ada.reshape(depth, 1, n))


def _layer_norm_rows(v):
    mu = jnp.mean(v, axis=-1, keepdims=True)
    vc = v - mu
    var = jnp.mean(vc * vc, axis=-1, keepdims=True)
    return vc * lax.rsqrt(var + LN_EPS)


def _ln_mod_kernel(x_ref, mod_ref, h_ref):
    shift = mod_ref[0, 0:1, :]
    scale1 = 1.0 + mod_ref[0, 1:2, :]

    def body(i, carry):
        r0 = pl.multiple_of(i * LN_ROWS, LN_ROWS)
        hn = _layer_norm_rows(x_ref[pl.ds(r0, LN_ROWS), :])
        h_ref[pl.ds(r0, LN_ROWS), :] = (hn * scale1 + shift).astype(BF16)
        return carry

    lax.fori_loop(0, x_ref.shape[0] // LN_ROWS, body, 0, unroll=LN_UNROLL)


def _ln_mod_call(x2, mod, rows_per_mod, name):
    m, d = x2.shape
    tm = min(LN_TM, m)
    per = rows_per_mod // tm
    return pl.pallas_call(
        _ln_mod_kernel,
        grid=(m // tm,),
        in_specs=[
            pl.BlockSpec((tm, d), lambda i: (i, 0)),
            pl.BlockSpec((1, 2, d), lambda i: (i // per, 0, 0)),
        ],
        out_specs=pl.BlockSpec((tm, d), lambda i: (i, 0)),
        out_shape=jax.ShapeDtypeStruct((m, d), BF16),
        compiler_params=_params("parallel"),
        name=name,
    )(x2, mod)


def _matmul_kernel(h_ref, w_ref, o_ref):
    o_ref[...] = _dot(h_ref[...], w_ref[0]).astype(o_ref.dtype)


def _proj_call(h2, w_all, layer, blk0, nblk, tn, out_dtype, name):
    m, d = h2.shape
    tm = min(PROJ_TM, m)
    return pl.pallas_call(
        _matmul_kernel,
        grid=(m // tm, nblk),
        in_specs=[
            pl.BlockSpec((tm, d), lambda i, j: (i, 0)),
            pl.BlockSpec((1, d, tn), lambda i, j: (layer, 0, blk0 + j)),
        ],
        out_specs=pl.BlockSpec((tm, tn), lambda i, j: (i, j)),
        out_shape=jax.ShapeDtypeStruct((m, nblk * tn), out_dtype),
        compiler_params=_params("parallel", "arbitrary"),
        name=name,
    )(h2, w_all)


def _softmax_pv(q, lane_lo, pieces):
    outs = []
    for h in range(2):
        keep = lane_lo if h == 0 else jnp.logical_not(lane_lo)
        qh = jnp.where(keep, q, jnp.zeros_like(q)) * jnp.asarray(NA_HEAD_DIM ** -0.5, BF16)
        scores = []
        for k, _, bias in pieces:
            s = _dot_nt(qh, k)
            if bias is not None:
                s = s + bias[h]
            scores.append(s)
        m = scores[0].max(axis=-1, keepdims=True)
        for s in scores[1:]:
            m = jnp.maximum(m, s.max(axis=-1, keepdims=True))
        den = None
        acc = None
        for s, (_, v, _) in zip(scores, pieces):
            p = jnp.exp(s - m)
            ps = p.sum(axis=-1, keepdims=True)
            pv = _dot(p.astype(BF16), v)
            den = ps if den is None else den + ps
            acc = pv if acc is None else acc + pv
        outs.append(acc / den)
    return jnp.where(lane_lo, outs[0], outs[1])


def _na_kernel(q_ref, k_ref, v_ref, z_ref, kc_ref, vc_ref, bias_ref, o_ref, vaug_ref, *, rows):
    tq = q_ref.shape[1]
    kh = min(NA_KH_MAX, rows)
    nkeys = kh * GRID_W
    span = vaug_ref.shape[1]
    qi = pl.program_id(2)
    lane_lo = lax.broadcasted_iota(jnp.int32, (tq, LANES), 1) < NA_HEAD_DIM
    q = q_ref[0]
    kc = kc_ref[0]
    vc = vc_ref[0]
    span_row = jnp.clip(qi * ATT_TQ_ROWS - kh // 2, 0, rows - span // GRID_W)
    starts, offs, d0s = [], [], []
    for i in range(ATT_TQ_ROWS):
        r = qi * ATT_TQ_ROWS + i
        rs = jnp.clip(r - kh // 2, 0, rows - kh)
        d0s.append(rs - r + (NA_KH_MAX - 1))
        starts.append(pl.multiple_of(rs * GRID_W, GRID_W))
        offs.append(pl.multiple_of((rs - span_row) * GRID_W, GRID_W))
    v_span = v_ref[0, pl.ds(pl.multiple_of(span_row * GRID_W, GRID_W), span), :]
    one = jnp.ones((), BF16)

    def head_lanes(shape, h):
        lo = lax.broadcasted_iota(jnp.int32, shape, 1) < NA_HEAD_DIM
        return lo if h == 0 else jnp.logical_not(lo)

    qhs, vcs = [], []
    for h in range(2):
        keep = lane_lo if h == 0 else jnp.logical_not(lane_lo)
        qf = jnp.where(keep, q, jnp.zeros_like(q)).astype(F32) * (NA_HEAD_DIM ** -0.5 * LOG2E)
        qhs.append(qf.astype(BF16))
        vaug_ref[h] = jnp.where(head_lanes(v_span.shape, h), v_span, one)
        vcs.append(jnp.where(head_lanes(vc.shape, h), vc, one))

    def stage_scores(h, rows_i):
        qs = qhs[h][rows_i[0] * GRID_W:(rows_i[-1] + 1) * GRID_W]
        s_loc = jnp.concatenate(
            [_dot_nt(qhs[h][i * GRID_W:(i + 1) * GRID_W], k_ref[0, pl.ds(starts[i], nkeys), :])
             + bias_ref[d0s[i], 0, h] for i in rows_i], axis=0)
        return h, s_loc, _dot_nt(qs, kc)

    def stage_softmax(h, s_loc, s_ctx):
        m = jnp.maximum(s_loc.max(axis=-1, keepdims=True), s_ctx.max(axis=-1, keepdims=True))
        return h, jnp.exp2(s_loc - m).astype(BF16), jnp.exp2(s_ctx - m).astype(BF16)

    def stage_values(h, p_loc, p_ctx, rows_i):
        acc = jnp.concatenate(
            [_dot(p_loc[j * GRID_W:(j + 1) * GRID_W], vaug_ref[h, pl.ds(offs[i], nkeys), :])
             for j, i in enumerate(rows_i)], axis=0)
        acc = acc + _dot(p_ctx, vcs[h])
        return acc / pltpu.roll(acc, NA_HEAD_DIM, 1)

    groups = [list(range(g, g + ATT_ITEM_ROWS)) for g in range(0, ATT_TQ_ROWS, ATT_ITEM_ROWS)]
    items = [(h, g) for h in range(2) for g in groups]
    scores, probs, outs = {}, {}, {}
    for t in range(len(items) + 2):
        if t < len(items):
            scores[t] = stage_scores(*items[t])
        if 0 <= t - 1 < len(items):
            probs[t - 1] = stage_softmax(*scores.pop(t - 1))
        if 0 <= t - 2 < len(items):
            outs[t - 2] = stage_values(*probs.pop(t - 2), items[t - 2][1])
    per_head = [jnp.concatenate([outs[t] for t, (hh, _) in enumerate(items) if hh == h], axis=0)
                for h in range(2)]
    o = jnp.where(lane_lo, per_head[0], per_head[1])
    o_ref[0] = (o * _silu(z_ref[0].astype(F32))).astype(o_ref.dtype)


def _na_bias_table(rpb, rows):
    depth, heads = rpb.shape[:2]
    kh = min(NA_KH_MAX, rows)
    nd = 2 * NA_KH_MAX - kh
    for r in range(rows):
        assert 0 <= int(np.clip(r - kh // 2, 0, rows - kh)) - r + (NA_KH_MAX - 1) < nd

    c = np.arange(GRID_W)[:, None]
    kc = np.arange(GRID_W)[None, :]
    cs = np.clip(c - NA_KW // 2, 0, GRID_W - NA_KW)
    valid_c = (kc >= cs) & (kc < cs + NA_KW)
    dcol = kc - c + (NA_KW - 1)
    sel_col = ((dcol[None] == np.arange(2 * NA_KW - 1)[:, None, None]) & valid_c[None]).astype(np.float32)
    d0 = np.arange(nd)[:, None, None]
    j = np.arange(kh)[None, :, None]
    sel_row = (np.arange(2 * NA_KH_MAX - 1)[None, None, :] == d0 + j).astype(np.float32)
    table = jnp.einsum("lhde,pjd,eck->lphcjk", rpb, sel_row, sel_col, precision=lax.Precision.HIGHEST)
    table = jnp.where(valid_c[:, None, :], table * LOG2E, NEG_BIG)
    return table.reshape(depth, nd, heads // 2, 2, GRID_W, kh * GRID_W)


def _na_call(p_lat, p_ctx, bias, d_a):
    b, l, _ = p_lat.shape
    lc = p_ctx.shape[1]
    rows = l // GRID_W
    assert rows % ATT_TQ_ROWS == 0
    tq = ATT_TQ_ROWS * GRID_W
    nq = l // tq
    npair = d_a // LANES
    nd, _, _, _, nkeys = bias.shape
    span_rows = min(ATT_TQ_ROWS + nkeys // GRID_W - 1, rows)

    return pl.pallas_call(
        functools.partial(_na_kernel, rows=rows),
        grid=(b, npair, nq),
        in_specs=[
            pl.BlockSpec((1, tq, LANES), lambda i, p, q: (i, q, p)),
            pl.BlockSpec((1, l, LANES), lambda i, p, q: (i, 0, npair + p)),
            pl.BlockSpec((1, l, LANES), lambda i, p, q: (i, 0, 2 * npair + p)),
            pl.BlockSpec((1, tq, LANES), lambda i, p, q: (i, q, 3 * npair + p)),
            pl.BlockSpec((1, lc, LANES), lambda i, p, q: (i, 0, npair + p)),
            pl.BlockSpec((1, lc, LANES), lambda i, p, q: (i, 0, 2 * npair + p)),
            pl.BlockSpec((nd, 1, 2, GRID_W, nkeys), lambda i, p, q: (0, p, 0, 0, 0)),
        ],
        out_specs=pl.BlockSpec((1, tq, LANES), lambda i, p, q: (i, q, p)),
        out_shape=jax.ShapeDtypeStruct((b, l, d_a), BF16),
        scratch_shapes=[pltpu.VMEM((2, span_rows * GRID_W, LANES), BF16)],
        compiler_params=_params("parallel", "parallel", "arbitrary"),
        name="na_attention",
    )(p_lat, p_lat, p_lat, p_lat, p_ctx, p_ctx, bias)


def _ctx_attn_kernel(q_ref, k_ref, v_ref, z_ref, o_ref):
    tq = q_ref.shape[1]
    lane_lo = lax.broadcasted_iota(jnp.int32, (tq, LANES), 1) < NA_HEAD_DIM
    o = _softmax_pv(q_ref[0], lane_lo, [(k_ref[0], v_ref[0], None)])
    o_ref[0] = (o * _silu(z_ref[0].astype(F32))).astype(o_ref.dtype)


def _ctx_attn_call(p_ctx, d_a):
    b, lc, _ = p_ctx.shape
    npair = d_a // LANES
    return pl.pallas_call(
        _ctx_attn_kernel,
        grid=(b, npair),
        in_specs=[
            pl.BlockSpec((1, lc, LANES), lambda i, p: (i, 0, p)),
            pl.BlockSpec((1, lc, LANES), lambda i, p: (i, 0, npair + p)),
            pl.BlockSpec((1, lc, LANES), lambda i, p: (i, 0, 2 * npair + p)),
            pl.BlockSpec((1, lc, LANES), lambda i, p: (i, 0, 3 * npair + p)),
        ],
        out_specs=pl.BlockSpec((1, lc, LANES), lambda i, p: (i, 0, p)),
        out_shape=jax.ShapeDtypeStruct((b, lc, d_a), BF16),
        compiler_params=_params("parallel", "parallel"),
        name="ctx_attention",
    )(p_ctx, p_ctx, p_ctx, p_ctx)


def _rope(x, cos, sin_signed):
    halves = []
    for j in range(x.shape[1] // LANES):
        xs = x[:, j * LANES:(j + 1) * LANES]
        lane = lax.broadcasted_iota(jnp.int32, xs.shape, 1)
        first = (lane & 16) == 0
        partner = jnp.where(first, pltpu.roll(xs, LANES - 16, 1), pltpu.roll(xs, 16, 1))
        halves.append(partner)
    partner = jnp.concatenate(halves, axis=1)
    return x * cos + partner * sin_signed


def _gla_kernel(q_ref, k_ref, v_ref, lr_ref, cos_ref, sin_ref, w2_ref, b2_ref, s0_ref,
                o_ref, sfin_ref, s_ref, *, reverse, rope):
    tg = q_ref.shape[1]
    dkh = q_ref.shape[2]
    dvh = v_ref.shape[2]
    dk = dkh // GLA_HEADS
    dv = dvh // GLA_HEADS
    nchunk = tg // GLA_CHUNK
    step = pl.program_id(1)

    @pl.when(step == 0)
    def _():
        s_ref[...] = s0_ref[0]

    lr_hi, lr_lo = _split_hi_lo(lr_ref[0])
    w_hi, w_lo = _split_hi_lo(w2_ref[...])
    logits = _dot(lr_hi, w_hi) + _dot(lr_lo, w_hi) + _dot(lr_hi, w_lo) + b2_ref[...]
    g = (jnp.minimum(logits, 0.0) - jnp.log1p(jnp.exp(-jnp.abs(logits)))) * (1.0 / GATE_TAU)

    r = lax.broadcasted_iota(jnp.int32, (tg, tg), 0)
    c = lax.broadcasted_iota(jnp.int32, (tg, tg), 1)
    same = _shr(r, GLA_CHUNK) == _shr(c, GLA_CHUNK)
    upto = (c >= r) if reverse else (c <= r)
    t_inc = jnp.where(same & upto, 1.0, 0.0).astype(BF16)
    g_hi, g_lo = _split_hi_lo(g)
    b_inc = _dot(t_inc, g_hi) + _dot(t_inc, g_lo)

    chunks = [slice(ci * GLA_CHUNK, (ci + 1) * GLA_CHUNK) for ci in range(nchunk)]
    end_row = 0 if reverse else GLA_CHUNK - 1
    b_end = [b_inc[sl][end_row:end_row + 1, :] for sl in chunks]
    b_rest = jnp.concatenate([be - b_inc[sl] for be, sl in zip(b_end, chunks)], axis=0)

    q = q_ref[0]
    k = k_ref[0]
    if rope:
        q = _rope(q, cos_ref[...], sin_ref[...])
        k = _rope(k, cos_ref[...], sin_ref[...])
    q = q * (dk ** -0.5)
    qt = (q * jnp.exp(b_inc)).astype(BF16)
    kt = (k * jnp.exp(-b_inc)).astype(BF16)
    kd = (k * jnp.exp(b_rest)).astype(BF16)
    vb = v_ref[0].astype(BF16)

    lane_head = _shr(lax.broadcasted_iota(jnp.int32, (GLA_CHUNK, dkh), 1), dk)
    hc = GLA_HEADS * GLA_CHUNK
    rk = _shr(lax.broadcasted_iota(jnp.int32, (hc, dkh), 0), GLA_CHUNK)
    ck = _shr(lax.broadcasted_iota(jnp.int32, (hc, dkh), 1), dk)
    mask_k = rk == ck
    rv = _shr(lax.broadcasted_iota(jnp.int32, (hc, dvh), 0), GLA_CHUNK)
    cv = _shr(lax.broadcasted_iota(jnp.int32, (hc, dvh), 1), dv)
    mask_v = rv == cv
    ti = lax.broadcasted_iota(jnp.int32, (GLA_CHUNK, dkh), 0)
    tj = lax.broadcasted_iota(jnp.int32, (GLA_CHUNK, dkh), 1) & (GLA_CHUNK - 1)
    causal = (tj >= ti) if reverse else (tj <= ti)
    zero = jnp.zeros((), BF16)

    def head_stack(x):
        return jnp.concatenate([jnp.where(lane_head == h, x, zero) for h in range(GLA_HEADS)], axis=0)

    o_intra, kv, decay, q_stack = [], [], [], []
    for sl, be in zip(chunks, b_end):
        k_blk = jnp.where(mask_k, jnp.concatenate([kt[sl]] * GLA_HEADS, axis=0), zero)
        att = jnp.where(causal, _dot_nt(qt[sl], k_blk), 0.0)
        v_blk = jnp.where(mask_v, jnp.concatenate([vb[sl]] * GLA_HEADS, axis=0), zero)
        o_intra.append(_dot(att.astype(BF16), v_blk))
        v_rows = jnp.concatenate([vb[sl][:, h * dv:(h + 1) * dv] for h in range(GLA_HEADS)], axis=0)
        kv.append(_dot_tn(head_stack(kd[sl]), v_rows))
        col = jnp.exp(jnp.transpose(jnp.broadcast_to(be, (LANES, dkh))))
        decay.append(jnp.concatenate([col] * (dv // LANES), axis=1))
        q_stack.append(head_stack(qt[sl]))

    order = list(range(nchunk - 1, -1, -1)) if reverse else list(range(nchunk))
    state = s_ref[...]
    states = {}
    for ci in order:
        states[ci] = state
        state = decay[ci] * state + kv[ci]
    s_ref[...] = state

    for ci in range(nchunk):
        inter = _dot(q_stack[ci], states[ci].astype(BF16))
        inter = jnp.concatenate([inter[h * GLA_CHUNK:(h + 1) * GLA_CHUNK] for h in range(GLA_HEADS)], axis=1)
        o_ref[0, chunks[ci], :] = o_intra[ci] + inter

    @pl.when(step == pl.num_programs(1) - 1)
    def _():
        sfin_ref[0] = state


def _gla_call(p_b, cos, sin_signed, w2p, b2, s0, *, reverse, rope, dkh, dvh, name):
    b, t, _ = p_b.shape
    tg = min(GLA_TG, t)
    ng = t // tg
    lr_blk = (2 * dkh + 2 * dvh) // LANES
    dv = dvh // GLA_HEADS
    assert dv % LANES == 0

    def blk(i):
        return ng - 1 - i if reverse else i

    kern = functools.partial(_gla_kernel, reverse=reverse, rope=rope)
    return pl.pallas_call(
        kern,
        grid=(b, ng),
        in_specs=[
            pl.BlockSpec((1, tg, dkh), lambda n, i: (n, blk(i), 0)),
            pl.BlockSpec((1, tg, dkh), lambda n, i: (n, blk(i), 1)),
            pl.BlockSpec((1, tg, dvh), lambda n, i: (n, blk(i), 2 * dkh // dvh)),
            pl.BlockSpec((1, tg, LANES), lambda n, i: (n, blk(i), lr_blk)),
            pl.BlockSpec((tg, dkh), lambda n, i: (blk(i), 0)),
            pl.BlockSpec((tg, dkh), lambda n, i: (blk(i), 0)),
            pl.BlockSpec((LANES, dkh), lambda n, i: (0, 0)),
            pl.BlockSpec((1, dkh), lambda n, i: (0, 0)),
            pl.BlockSpec((1, dkh, dv), lambda n, i: (n, 0, 0)),
        ],
        out_specs=[
            pl.BlockSpec((1, tg, dvh), lambda n, i: (n, blk(i), 0)),
            pl.BlockSpec((1, dkh, dv), lambda n, i: (n, 0, 0)),
        ],
        out_shape=[
            jax.ShapeDtypeStruct((b, t, dvh), F32),
            jax.ShapeDtypeStruct((b, dkh, dv), F32),
        ],
        scratch_shapes=[pltpu.VMEM((dkh, dv), F32)],
        compiler_params=_params("parallel", "arbitrary"),
        name=name,
    )(p_b, p_b, p_b, p_b, cos, sin_signed, w2p, b2, s0)


def _rope_tables(l, dk):
    quarter = dk // 4
    pos = jnp.arange(l)
    rows_pos = (pos // GRID_W).astype(F32)
    cols_pos = (pos % GRID_W).astype(F32)
    inv = ROPE_BASE ** (-jnp.arange(0, 2 * quarter, 2, dtype=F32) / (2 * quarter))
    ang_r = rows_pos[:, None] * inv[None, :]
    ang_c = cols_pos[:, None] * inv[None, :]
    cos = jnp.concatenate([jnp.cos(ang_r)] * 2 + [jnp.cos(ang_c)] * 2, axis=1)
    sin = jnp.concatenate([-jnp.sin(ang_r), jnp.sin(ang_r), -jnp.sin(ang_c), jnp.sin(ang_c)], axis=1)
    return jnp.tile(cos, (1, GLA_HEADS)), jnp.tile(sin, (1, GLA_HEADS))


def _conv_kernel(a_ref, g_ref, z_ref, ap_ref, gp_ref, an_ref, gn_ref, w_ref, cb_ref, lg_ref, lb_ref,
                 o_ref, u_ref, us_ref):
    t = a_ref.shape[1]
    nshift = u_ref.shape[0] - SUBLANES
    step = pl.program_id(1)
    nstep = pl.num_programs(1)

    def glu(a, g):
        return a.astype(F32) * jax.nn.sigmoid(g.astype(F32))

    prev_ok = (step > 0).astype(F32)
    next_ok = (step < nstep - 1).astype(F32)
    u_ref[0:CONV_HALO, :] = glu(ap_ref[0], gp_ref[0]) * prev_ok
    u_ref[CONV_HALO:CONV_HALO + t, :] = glu(a_ref[0], g_ref[0])
    u_ref[CONV_HALO + t:2 * CONV_HALO + t, :] = glu(an_ref[0], gn_ref[0]) * next_ok

    for b in range(1, SUBLANES):
        us_ref[b - 1] = u_ref[b:b + nshift, :]

    off = CONV_HALO - CONV_K // 2
    for rc in range(t // CONV_RC):
        base = rc * CONV_RC
        acc = None
        for j in range(CONV_K):
            q8, b = divmod(off + j, SUBLANES)
            lo = base + q8 * SUBLANES
            src = u_ref[lo:lo + CONV_RC, :] if b == 0 else us_ref[b - 1, lo:lo + CONV_RC, :]
            term = src * w_ref[j:j + 1, :]
            acc = term if acc is None else acc + term
        acc = acc + cb_ref[...]
        mu = jnp.mean(acc, axis=-1, keepdims=True)
        xc = acc - mu
        var = jnp.mean(xc * xc, axis=-1, keepdims=True)
        y = xc * lax.rsqrt(var + LN_EPS) * lg_ref[...] + lb_ref[...]
        z = z_ref[0, base:base + CONV_RC, :].astype(F32)
        o_ref[0, base:base + CONV_RC, :] = (_silu(y) * _silu(z)).astype(o_ref.dtype)


def _conv_call(p, conv_w, conv_b, ln_g, ln_b, col0, d_c, name):
    b, t, _ = p.shape
    tt = min(CONV_T, t)
    nt = t // tt
    cb = col0 // d_c
    hb = tt // CONV_HALO
    nhalo = t // CONV_HALO
    w = jnp.zeros((CONV_K + 1, d_c), F32).at[:CONV_K].set(conv_w)

    def prev(i):
        return jnp.maximum(i * hb - 1, 0)

    def nxt(i):
        return jnp.minimum((i + 1) * hb, nhalo - 1)

    row = lambda v: v.reshape(1, d_c)
    return pl.pallas_call(
        _conv_kernel,
        grid=(b, nt),
        in_specs=[
            pl.BlockSpec((1, tt, d_c), lambda n, i: (n, i, cb)),
            pl.BlockSpec((1, tt, d_c), lambda n, i: (n, i, cb + 1)),
            pl.BlockSpec((1, tt, d_c), lambda n, i: (n, i, cb + 2)),
            pl.BlockSpec((1, CONV_HALO, d_c), lambda n, i: (n, prev(i), cb)),
            pl.BlockSpec((1, CONV_HALO, d_c), lambda n, i: (n, prev(i), cb + 1)),
            pl.BlockSpec((1, CONV_HALO, d_c), lambda n, i: (n, nxt(i), cb)),
            pl.BlockSpec((1, CONV_HALO, d_c), lambda n, i: (n, nxt(i), cb + 1)),
            pl.BlockSpec((CONV_K + 1, d_c), lambda n, i: (0, 0)),
            pl.BlockSpec((1, d_c), lambda n, i: (0, 0)),
            pl.BlockSpec((1, d_c), lambda n, i: (0, 0)),
            pl.BlockSpec((1, d_c), lambda n, i: (0, 0)),
        ],
        out_specs=pl.BlockSpec((1, tt, d_c), lambda n, i: (n, i, 0)),
        out_shape=jax.ShapeDtypeStruct((b, t, d_c), BF16),
        scratch_shapes=[pltpu.VMEM((tt + 2 * CONV_HALO, d_c), F32),
                        pltpu.VMEM((SUBLANES - 1, tt + 2 * CONV_HALO - SUBLANES, d_c), F32)],
        compiler_params=_params("parallel", "arbitrary"),
        name=name,
    )(p, p, p, p, p, p, p, w, row(conv_b), row(ln_g), row(ln_b))


def _out_kernel(oa_ref, of_ref, ob_ref, zb_ref, oc_ref, x_ref, gate_ref, gn_ref,
                wa_ref, wb_ref, wc_ref, lg_ref, lb_ref, *rest, alpha, emit_h):
    if emit_h:
        mod_ref, o_ref, h_ref, y0_ref, y1_ref = rest
    else:
        o_ref, y0_ref, y1_ref = rest
    tm = x_ref.shape[0]
    dvh = of_ref.shape[1]
    dv = dvh // GLA_HEADS
    step = pl.program_id(0)

    @pl.when(step == 0)
    def _():
        y1_ref[...] = jnp.zeros_like(y1_ref)

    def run(y_prev, y_cur):
        gate = gate_ref[0]
        if emit_h:
            shift = mod_ref[0, 0:1, :]
            scale1 = 1.0 + mod_ref[0, 1:2, :]
        for c in range(tm // LN_ROWS):
            sl = slice(c * LN_ROWS, (c + 1) * LN_ROWS)
            rv = alpha * x_ref[sl, :] + gate * y_prev[sl, :]
            xn = _layer_norm_rows(rv) * lg_ref[...] + lb_ref[...]
            o_ref[sl, :] = xn
            if emit_h:
                h_ref[sl, :] = (_layer_norm_rows(xn) * scale1 + shift).astype(BF16)

        s = of_ref[...] + ob_ref[...]
        parts = []
        for h in range(GLA_HEADS):
            sh = s[:, h * dv:(h + 1) * dv]
            ms = jnp.mean(sh * sh, axis=-1, keepdims=True)
            parts.append(sh * lax.rsqrt(ms + RMS_EPS))
        out_b = jnp.concatenate(parts, axis=1) * gn_ref[...] * _silu(zb_ref[...])
        y_cur[...] = (_dot(oa_ref[...], wa_ref[...]) + _dot(out_b.astype(BF16), wb_ref[...])
                      + _dot(oc_ref[...], wc_ref[...]))

    @pl.when(step % 2 == 0)
    def _():
        run(y1_ref, y0_ref)

    @pl.when(step % 2 == 1)
    def _():
        run(y0_ref, y1_ref)


def _out_call(oa, o_f, o_b, p_b, oc, x2, gate, gla_norm, w_out, ln_g, ln_b, next_mod, rows_per_gate, zb_col,
              alpha, name):
    m, d = x2.shape
    d_a, dvh, d_c = oa.shape[1], o_f.shape[1], oc.shape[1]
    tm = min(OUT_TM, m)
    per = rows_per_gate // tm
    emit_h = next_mod is not None
    wa = w_out[:d_a].astype(BF16)
    wb = w_out[d_a:d_a + dvh].astype(BF16)
    wc = w_out[d_a + dvh:].astype(BF16)
    gn = jnp.tile(gla_norm, GLA_HEADS).reshape(1, dvh)
    nt = m // tm
    cur = lambda i: jnp.minimum(i, nt - 1)
    prev = lambda i: jnp.maximum(i - 1, 0)
    full = lambda shape: pl.BlockSpec(shape, lambda i: (0,) * len(shape))
    rows = pl.BlockSpec((tm, d), lambda i: (prev(i), 0))
    in_specs = [
        pl.BlockSpec((tm, d_a), lambda i: (cur(i), 0)),
        pl.BlockSpec((tm, dvh), lambda i: (cur(i), 0)),
        pl.BlockSpec((tm, dvh), lambda i: (cur(i), 0)),
        pl.BlockSpec((tm, dvh), lambda i: (cur(i), zb_col // dvh)),
        pl.BlockSpec((tm, d_c), lambda i: (cur(i), 0)),
        rows,
        pl.BlockSpec((1, 1, d), lambda i: (prev(i) // per, 0, 0)),
        full((1, dvh)),
        full((d_a, d)),
        full((dvh, d)),
        full((d_c, d)),
        full((1, d)),
        full((1, d)),
    ]
    args = [oa, o_f, o_b, p_b, oc, x2, gate, gn, wa, wb, wc, ln_g.reshape(1, d), ln_b.reshape(1, d)]
    out_specs = rows
    out_shape = jax.ShapeDtypeStruct((m, d), F32)
    if emit_h:
        in_specs.append(pl.BlockSpec((1, 2, d), lambda i: (prev(i) // per, 0, 0)))
        args.append(next_mod)
        out_specs = [rows, rows]
        out_shape = [out_shape, jax.ShapeDtypeStruct((m, d), BF16)]
    return pl.pallas_call(
        functools.partial(_out_kernel, alpha=alpha, emit_h=emit_h),
        grid=(nt + 1,),
        in_specs=in_specs,
        out_specs=out_specs,
        out_shape=out_shape,
        scratch_shapes=[pltpu.VMEM((tm, d), F32), pltpu.VMEM((tm, d), F32)],
        compiler_params=_params("arbitrary"),
        name=name,
    )(*args)


def kernel(x, c, ctx, c_ctx, w_ada, b_ada, w_in, rpb, gla_w2, gla_b, gla_norm, conv_w, conv_b,
           conv_ln_g, conv_ln_b, w_out, post_ln_g, post_ln_b):
    b, l, d = x.shape
    lc = ctx.shape[1]
    depth = w_ada.shape[0]
    heads_a = rpb.shape[1]
    d_a = heads_a * NA_HEAD_DIM
    dkh = gla_w2.shape[-1]
    dvh = gla_norm.shape[-1] * GLA_HEADS
    d_c = conv_w.shape[-1]
    n_a = 4 * d_a
    n_b = 2 * dkh + 2 * dvh + 2 * GATE_RANK
    n_ac = n_a + 3 * d_c
    n_bp = -(-n_b // LANES) * LANES
    alpha = (2 * depth) ** 0.25
    assert w_in.shape[-1] == n_a + n_b + 3 * d_c and d_a + dvh + d_c == w_out.shape[1]

    cond = jnp.zeros((8, d), F32).at[:b].set(c).at[b].set(c_ctx)
    mod = _ada_call(cond, w_ada, b_ada)

    tn_ac = n_ac // 4
    b_span = -(-n_bp // tn_ac) * tn_ac
    w_all = jnp.concatenate(
        [w_in[:, :, n_a:n_a + n_b], jnp.zeros((depth, d, b_span - n_b), F32),
         w_in[:, :, :n_a], w_in[:, :, n_a + n_b:]], axis=2).astype(BF16)
    na_bias = _na_bias_table(rpb, l // GRID_W)

    cos, sin_signed = _rope_tables(l, dkh // GLA_HEADS)
    ones_c = jnp.ones((lc, dkh), F32)
    zeros_c = jnp.zeros((lc, dkh), F32)

    def layer_mods(layer):
        shift, scale, gate = jnp.split(mod[layer], 3, axis=-1)
        mod_lat = jnp.stack([shift[:b], scale[:b]], axis=1)
        mod_ctx = jnp.stack([shift[b:b + 1], scale[b:b + 1]], axis=1)
        return mod_lat, mod_ctx, gate[:b].reshape(b, 1, d), gate[b:b + 1].reshape(1, 1, d)

    x2 = x.reshape(b * l, d)
    cx2 = ctx.reshape(b * lc, d)
    mod_lat, mod_ctx, gate_lat, gate_ctx = layer_mods(0)
    h2 = _ln_mod_call(x2, mod_lat, l, "ln_mod")
    hc2 = _ln_mod_call(cx2, mod_ctx, b * lc, "ln_mod_ctx")
    for layer in range(depth):
        last = layer == depth - 1
        if not last:
            next_lat, next_ctx, next_gate_lat, next_gate_ctx = layer_mods(layer + 1)
        else:
            next_lat = next_ctx = None

        ac0 = b_span // tn_ac
        p_ac = _proj_call(h2, w_all, layer, ac0, 4, tn_ac, BF16, "proj_ac").reshape(b, l, n_ac)
        p_b = _proj_call(h2, w_all, layer, 0, 1, n_bp, F32, "proj_b").reshape(b, l, n_bp)
        pc_ac = _proj_call(hc2, w_all, layer, ac0, 4, tn_ac, BF16, "proj_ac_ctx").reshape(b, lc, n_ac)
        pc_b = _proj_call(hc2, w_all, layer, 0, 1, n_bp, F32, "proj_b_ctx").reshape(b, lc, n_bp)

        out_a = _na_call(p_ac, pc_ac, na_bias[layer], d_a)

        w2p = [jnp.zeros((LANES, dkh), F32).at[i * GATE_RANK:(i + 1) * GATE_RANK].set(gla_w2[layer, i])
               for i in range(2)]
        b2 = [gla_b[layer, i].reshape(1, dkh) for i in range(2)]
        s0 = jnp.zeros((b, dkh, dvh // GLA_HEADS), F32)
        gla = functools.partial(_gla_call, dkh=dkh, dvh=dvh)
        oc_f, s_f = gla(pc_b, ones_c, zeros_c, w2p[0], b2[0], s0, reverse=False, rope=False, name="gla_ctx_fwd")
        oc_b, s_b = gla(pc_b, ones_c, zeros_c, w2p[1], b2[1], s0, reverse=True, rope=False, name="gla_ctx_bwd")
        o_f, _ = gla(p_b, cos, sin_signed, w2p[0], b2[0], s_f, reverse=False, rope=True, name="gla_fwd")
        o_b, _ = gla(p_b, cos, sin_signed, w2p[1], b2[1], s_b, reverse=True, rope=True, name="gla_bwd")

        out_c = _conv_call(p_ac, conv_w[layer], conv_b[layer], conv_ln_g[layer], conv_ln_b[layer],
                           n_a, d_c, "conv")

        zb_col = 2 * dkh + dvh
        res = _out_call(out_a.reshape(b * l, d_a), o_f.reshape(b * l, dvh), o_b.reshape(b * l, dvh),
                        p_b.reshape(b * l, n_bp), out_c.reshape(b * l, d_c), x2, gate_lat,
                        gla_norm[layer], w_out[layer], post_ln_g[layer], post_ln_b[layer], next_lat,
                        l, zb_col, alpha, "out_proj")
        if last:
            x2 = res
        else:
            out_a_c = _ctx_attn_call(pc_ac, d_a)
            out_c_c = _conv_call(pc_ac, conv_w[layer], conv_b[layer], conv_ln_g[layer], conv_ln_b[layer],
                                 n_a, d_c, "conv_ctx")
            cx2, hc2 = _out_call(out_a_c.reshape(b * lc, d_a), oc_f.reshape(b * lc, dvh),
                                 oc_b.reshape(b * lc, dvh), pc_b.reshape(b * lc, n_bp),
                                 out_c_c.reshape(b * lc, d_c), cx2, gate_ctx, gla_norm[layer], w_out[layer],
                                 post_ln_g[layer], post_ln_b[layer], next_ctx, b * lc, zb_col, alpha,
                                 "out_proj_ctx")
            x2, h2 = res
            gate_lat, gate_ctx = next_gate_lat, next_gate_ctx
    return x2.reshape(b, l, d)
```

```python
import functools

import numpy as np
import jax
import jax.numpy as jnp
from jax import lax
from jax.experimental import pallas as pl
from jax.experimental.pallas import tpu as pltpu

F32 = jnp.float32
BF16 = jnp.bfloat16

GRID_W = 64
NA_HEAD_DIM = 64
NA_KH_MAX = 8
NA_KW = 16
GLA_HEADS = 4
GATE_RANK = 16
GATE_TAU = 16.0
GLA_CHUNK = 64
CONV_K = 31
ROPE_BASE = 10000.0
LN_EPS = 1e-5
RMS_EPS = 1e-6

LANES = 128
SUBLANES = 8
VMEM_LIMIT_BYTES = 56 * 1024 * 1024

NEG_BIG = -1e30
LOG2E = 1.4426950408889634

PROJ_TM = 1024
PROJ_TN = 1024
OUT_TM = 256
LN_TM = 512
LN_ROWS = 32
LN_UNROLL = 4
ATT_TQ_ROWS = 16
ATT_ITEM_ROWS = 2
GLA_TG = 256
CONV_T = 512
CONV_HALO = 16
CONV_RC = 64


def _silu(x):
    return x * jax.nn.sigmoid(x)


def _dot(a, b):
    return jnp.dot(a, b, preferred_element_type=F32)


def _dot_nt(a, b):
    return lax.dot_general(a, b, (((1,), (1,)), ((), ())), preferred_element_type=F32)


def _dot_tn(a, b):
    return lax.dot_general(a, b, (((0,), (0,)), ((), ())), preferred_element_type=F32)


def _shr(x, pow2):
    shift = pow2.bit_length() - 1
    assert 1 << shift == pow2
    return jnp.right_shift(x, shift)


def _split_hi_lo(x):
    hi = x.astype(BF16)
    lo = (x - hi.astype(F32)).astype(BF16)
    return hi, lo


def _params(*sem):
    return pltpu.CompilerParams(dimension_semantics=sem, vmem_limit_bytes=VMEM_LIMIT_BYTES)


def _ada_kernel(c_ref, w_ref, b_ref, o_ref):
    s = _silu(c_ref[...]).astype(BF16)
    o_ref[0] = _dot(s, w_ref[0].astype(BF16)) + b_ref[0]


def _ada_call(cond, w_ada, b_ada, tn=512):
    depth, d, n = w_ada.shape
    rows = cond.shape[0]
    return pl.pallas_call(
        _ada_kernel,
        grid=(depth, n // tn),
        in_specs=[
            pl.BlockSpec((rows, d), lambda l, j: (0, 0)),
            pl.BlockSpec((1, d, tn), lambda l, j: (l, 0, j)),
            pl.BlockSpec((1, 1, tn), lambda l, j: (l, 0, j)),
        ],
        out_specs=pl.BlockSpec((1, rows, tn), lambda l, j: (l, 0, j)),
        out_shape=jax.ShapeDtypeStruct((depth, rows, n), F32),
        compiler_params=_params("parallel", "parallel"),
        name="ada_mod",
    )(cond, w_ada, b_ada.reshape(depth, 1, n))


def _layer_norm_rows(v):
    mu = jnp.mean(v, axis=-1, keepdims=True)
    vc = v - mu
    var = jnp.mean(vc * vc, axis=-1, keepdims=True)
    return vc * lax.rsqrt(var + LN_EPS)


def _ln_mod_kernel(x_ref, mod_ref, h_ref):
    shift = mod_ref[0, 0:1, :]
    scale1 = 1.0 + mod_ref[0, 1:2, :]

    def body(i, carry):
        r0 = pl.multiple_of(i * LN_ROWS, LN_ROWS)
        hn = _layer_norm_rows(x_ref[pl.ds(r0, LN_ROWS), :])
        h_ref[pl.ds(r0, LN_ROWS), :] = (hn * scale1 + shift).astype(BF16)
        return carry

    lax.fori_loop(0, x_ref.shape[0] // LN_ROWS, body, 0, unroll=LN_UNROLL)


def _ln_mod_call(x2, mod, rows_per_mod, name):
    m, d = x2.shape
    tm = min(LN_TM, m)
    per = rows_per_mod // tm
    return pl.pallas_call(
        _ln_mod_kernel,
        grid=(m // tm,),
        in_specs=[
            pl.BlockSpec((tm, d), lambda i: (i, 0)),
            pl.BlockSpec((1, 2, d), lambda i: (i // per, 0, 0)),
        ],
        out_specs=pl.BlockSpec((tm, d), lambda i: (i, 0)),
        out_shape=jax.ShapeDtypeStruct((m, d), BF16),
        compiler_params=_params("parallel"),
        name=name,
    )(x2, mod)


def _matmul_kernel(h_ref, w_ref, o_ref, wb_ref):
    @pl.when(pl.program_id(1) == 0)
    def _():
        wb_ref[...] = w_ref[0].astype(BF16)

    o_ref[...] = _dot(h_ref[...], wb_ref[...]).astype(o_ref.dtype)


def _proj_call(h2, w3, layer, blk0, nblk, tn, out_dtype, name):
    m, d = h2.shape
    tm = min(PROJ_TM, m)
    return pl.pallas_call(
        _matmul_kernel,
        grid=(nblk, m // tm),
        in_specs=[
            pl.BlockSpec((tm, d), lambda j, i: (i, 0)),
            pl.BlockSpec((1, d, tn), lambda j, i: (layer, 0, blk0 + j)),
        ],
        out_specs=pl.BlockSpec((tm, tn), lambda j, i: (i, j)),
        out_shape=jax.ShapeDtypeStruct((m, nblk * tn), out_dtype),
        scratch_shapes=[pltpu.VMEM((d, tn), BF16)],
        compiler_params=_params("arbitrary", "arbitrary"),
        name=name,
    )(h2, w3)


def _softmax_pv(q, lane_lo, pieces):
    outs = []
    for h in range(2):
        keep = lane_lo if h == 0 else jnp.logical_not(lane_lo)
        qh = jnp.where(keep, q, jnp.zeros_like(q)) * jnp.asarray(NA_HEAD_DIM ** -0.5, BF16)
        scores = []
        for k, _, bias in pieces:
            s = _dot_nt(qh, k)
            if bias is not None:
                s = s + bias[h]
            scores.append(s)
        m = scores[0].max(axis=-1, keepdims=True)
        for s in scores[1:]:
            m = jnp.maximum(m, s.max(axis=-1, keepdims=True))
        den = None
        acc = None
        for s, (_, v, _) in zip(scores, pieces):
            p = jnp.exp(s - m)
            ps = p.sum(axis=-1, keepdims=True)
            pv = _dot(p.astype(BF16), v)
            den = ps if den is None else den + ps
            acc = pv if acc is None else acc + pv
        outs.append(acc / den)
    return jnp.where(lane_lo, outs[0], outs[1])


def _na_kernel(q_ref, k_ref, v_ref, z_ref, kc_ref, vc_ref, comp_ref, o_ref, vaug_ref, bias_ref, *, rows):
    tq = q_ref.shape[1]
    kh = min(NA_KH_MAX, rows)
    nkeys = kh * GRID_W
    span = vaug_ref.shape[1]
    qi = pl.program_id(2)

    @pl.when((pl.program_id(1) == 0) & (qi == 0))
    def _():
        for h in range(2):
            comp = comp_ref[0, h]
            for d0 in range(bias_ref.shape[0]):
                bias_ref[d0, h] = comp[:, d0 * GRID_W:d0 * GRID_W + nkeys]

    lane_lo = lax.broadcasted_iota(jnp.int32, (tq, LANES), 1) < NA_HEAD_DIM
    q = q_ref[0]
    kc = kc_ref[0]
    vc = vc_ref[0]
    span_row = jnp.clip(qi * ATT_TQ_ROWS - kh // 2, 0, rows - span // GRID_W)
    starts, offs, d0s = [], [], []
    for i in range(ATT_TQ_ROWS):
        r = qi * ATT_TQ_ROWS + i
        rs = jnp.clip(r - kh // 2, 0, rows - kh)
        d0s.append(rs - r + (NA_KH_MAX - 1))
        starts.append(pl.multiple_of(rs * GRID_W, GRID_W))
        offs.append(pl.multiple_of((rs - span_row) * GRID_W, GRID_W))
    v_span = v_ref[0, pl.ds(pl.multiple_of(span_row * GRID_W, GRID_W), span), :]
    one = jnp.ones((), BF16)

    def head_lanes(shape, h):
        lo = lax.broadcasted_iota(jnp.int32, shape, 1) < NA_HEAD_DIM
        return lo if h == 0 else jnp.logical_not(lo)

    qhs, vcs = [], []
    for h in range(2):
        keep = lane_lo if h == 0 else jnp.logical_not(lane_lo)
        qf = jnp.where(keep, q, jnp.zeros_like(q)).astype(F32) * (NA_HEAD_DIM ** -0.5 * LOG2E)
        qhs.append(qf.astype(BF16))
        vaug_ref[h] = jnp.where(head_lanes(v_span.shape, h), v_span, one)
        vcs.append(jnp.where(head_lanes(vc.shape, h), vc, one))

    def stage_scores(h, rows_i):
        qs = qhs[h][rows_i[0] * GRID_W:(rows_i[-1] + 1) * GRID_W]
        s_loc = jnp.concatenate(
            [_dot_nt(qhs[h][i * GRID_W:(i + 1) * GRID_W], k_ref[0, pl.ds(starts[i], nkeys), :])
             + bias_ref[d0s[i], h] for i in rows_i], axis=0)
        return h, s_loc, _dot_nt(qs, kc)

    def stage_softmax(h, s_loc, s_ctx):
        m = jnp.maximum(s_loc.max(axis=-1, keepdims=True), s_ctx.max(axis=-1, keepdims=True))
        return h, jnp.exp2(s_loc - m).astype(BF16), jnp.exp2(s_ctx - m).astype(BF16)

    def stage_values(h, p_loc, p_ctx, rows_i):
        acc = jnp.concatenate(
            [_dot(p_loc[j * GRID_W:(j + 1) * GRID_W], vaug_ref[h, pl.ds(offs[i], nkeys), :])
             for j, i in enumerate(rows_i)], axis=0)
        acc = acc + _dot(p_ctx, vcs[h])
        return acc / pltpu.roll(acc, NA_HEAD_DIM, 1)

    groups = [list(range(g, g + ATT_ITEM_ROWS)) for g in range(0, ATT_TQ_ROWS, ATT_ITEM_ROWS)]
    items = [(h, g) for h in range(2) for g in groups]
    scores, probs, outs = {}, {}, {}
    for t in range(len(items) + 2):
        if t < len(items):
            scores[t] = stage_scores(*items[t])
        if 0 <= t - 1 < len(items):
            probs[t - 1] = stage_softmax(*scores.pop(t - 1))
        if 0 <= t - 2 < len(items):
            outs[t - 2] = stage_values(*probs.pop(t - 2), items[t - 2][1])
    per_head = [jnp.concatenate([outs[t] for t, (hh, _) in enumerate(items) if hh == h], axis=0)
                for h in range(2)]
    o = jnp.where(lane_lo, per_head[0], per_head[1])
    o_ref[0] = (o * _silu(z_ref[0].astype(F32))).astype(o_ref.dtype)


def _na_bias_table(rpb, rows):
    depth, heads = rpb.shape[:2]
    kh = min(NA_KH_MAX, rows)
    nd = 2 * NA_KH_MAX - kh
    for r in range(rows):
        assert 0 <= int(np.clip(r - kh // 2, 0, rows - kh)) - r + (NA_KH_MAX - 1) < nd

    c = np.arange(GRID_W)[:, None]
    kc = np.arange(GRID_W)[None, :]
    cs = np.clip(c - NA_KW // 2, 0, GRID_W - NA_KW)
    valid_c = (kc >= cs) & (kc < cs + NA_KW)
    dcol = kc - c + (NA_KW - 1)
    sel_col = ((dcol[None] == np.arange(2 * NA_KW - 1)[:, None, None]) & valid_c[None]).astype(np.float32)
    comp = jnp.einsum("lhde,eck->lhcdk", rpb, sel_col, precision=lax.Precision.HIGHEST)
    comp = jnp.where(valid_c[:, None, :], comp * LOG2E, NEG_BIG)
    ncomp = (2 * NA_KH_MAX - 1) * GRID_W
    comp = comp.reshape(depth, heads // 2, 2, GRID_W, ncomp)
    return jnp.pad(comp, ((0, 0),) * 4 + ((0, -ncomp % LANES),), constant_values=NEG_BIG)


def _na_call(p_lat, p_ctx, bias, d_a):
    b, l, _ = p_lat.shape
    lc = p_ctx.shape[1]
    rows = l // GRID_W
    assert rows % ATT_TQ_ROWS == 0
    tq = ATT_TQ_ROWS * GRID_W
    nq = l // tq
    npair = d_a // LANES
    kh = min(NA_KH_MAX, rows)
    nd = 2 * NA_KH_MAX - kh
    ncomp = bias.shape[-1]
    span_rows = min(ATT_TQ_ROWS + kh - 1, rows)

    return pl.pallas_call(
        functools.partial(_na_kernel, rows=rows),
        grid=(npair, b, nq),
        in_specs=[
            pl.BlockSpec((1, tq, LANES), lambda p, i, q: (i, q, p)),
            pl.BlockSpec((1, l, LANES), lambda p, i, q: (i, 0, npair + p)),
            pl.BlockSpec((1, l, LANES), lambda p, i, q: (i, 0, 2 * npair + p)),
            pl.BlockSpec((1, tq, LANES), lambda p, i, q: (i, q, 3 * npair + p)),
            pl.BlockSpec((1, lc, LANES), lambda p, i, q: (i, 0, npair + p)),
            pl.BlockSpec((1, lc, LANES), lambda p, i, q: (i, 0, 2 * npair + p)),
            pl.BlockSpec((1, 2, GRID_W, ncomp), lambda p, i, q: (p, 0, 0, 0)),
        ],
        out_specs=pl.BlockSpec((1, tq, LANES), lambda p, i, q: (i, q, p)),
        out_shape=jax.ShapeDtypeStruct((b, l, d_a), BF16),
        scratch_shapes=[pltpu.VMEM((2, span_rows * GRID_W, LANES), BF16),
                        pltpu.VMEM((nd, 2, GRID_W, kh * GRID_W), F32)],
        compiler_params=_params("arbitrary", "arbitrary", "arbitrary"),
        name="na_attention",
    )(p_lat, p_lat, p_lat, p_lat, p_ctx, p_ctx, bias)


def _ctx_attn_kernel(q_ref, k_ref, v_ref, z_ref, o_ref):
    tq = q_ref.shape[1]
    lane_lo = lax.broadcasted_iota(jnp.int32, (tq, LANES), 1) < NA_HEAD_DIM
    o = _softmax_pv(q_ref[0], lane_lo, [(k_ref[0], v_ref[0], None)])
    o_ref[0] = (o * _silu(z_ref[0].astype(F32))).astype(o_ref.dtype)


def _ctx_attn_call(p_ctx, d_a):
    b, lc, _ = p_ctx.shape
    npair = d_a // LANES
    return pl.pallas_call(
        _ctx_attn_kernel,
        grid=(b, npair),
        in_specs=[
            pl.BlockSpec((1, lc, LANES), lambda i, p: (i, 0, p)),
            pl.BlockSpec((1, lc, LANES), lambda i, p: (i, 0, npair + p)),
            pl.BlockSpec((1, lc, LANES), lambda i, p: (i, 0, 2 * npair + p)),
            pl.BlockSpec((1, lc, LANES), lambda i, p: (i, 0, 3 * npair + p)),
        ],
        out_specs=pl.BlockSpec((1, lc, LANES), lambda i, p: (i, 0, p)),
        out_shape=jax.ShapeDtypeStruct((b, lc, d_a), BF16),
        compiler_params=_params("parallel", "parallel"),
        name="ctx_attention",
    )(p_ctx, p_ctx, p_ctx, p_ctx)


def _rope(x, cos, sin_signed):
    halves = []
    for j in range(x.shape[1] // LANES):
        xs = x[:, j * LANES:(j + 1) * LANES]
        lane = lax.broadcasted_iota(jnp.int32, xs.shape, 1)
        first = (lane & 16) == 0
        partner = jnp.where(first, pltpu.roll(xs, LANES - 16, 1), pltpu.roll(xs, 16, 1))
        halves.append(partner)
    partner = jnp.concatenate(halves, axis=1)
    return x * cos + partner * sin_signed


def _gla_kernel(q_ref, k_ref, v_ref, lr_ref, cos_ref, sin_ref, w2_ref, b2_ref, s0_ref,
                o_ref, sfin_ref, s_ref, *, reverse, rope):
    tg = q_ref.shape[1]
    dkh = q_ref.shape[2]
    dvh = v_ref.shape[2]
    dk = dkh // GLA_HEADS
    dv = dvh // GLA_HEADS
    nchunk = tg // GLA_CHUNK
    step = pl.program_id(1)

    @pl.when(step == 0)
    def _():
        s_ref[...] = s0_ref[0]

    lr_hi, lr_lo = _split_hi_lo(lr_ref[0])
    w_hi, w_lo = _split_hi_lo(w2_ref[...])
    logits = _dot(lr_hi, w_hi) + _dot(lr_lo, w_hi) + _dot(lr_hi, w_lo) + b2_ref[...]
    g = (jnp.minimum(logits, 0.0) - jnp.log1p(jnp.exp(-jnp.abs(logits)))) * (1.0 / GATE_TAU)

    r = lax.broadcasted_iota(jnp.int32, (tg, tg), 0)
    c = lax.broadcasted_iota(jnp.int32, (tg, tg), 1)
    same = _shr(r, GLA_CHUNK) == _shr(c, GLA_CHUNK)
    upto = (c >= r) if reverse else (c <= r)
    t_inc = jnp.where(same & upto, 1.0, 0.0).astype(BF16)
    g_hi, g_lo = _split_hi_lo(g)
    b_inc = _dot(t_inc, g_hi) + _dot(t_inc, g_lo)

    chunks = [slice(ci * GLA_CHUNK, (ci + 1) * GLA_CHUNK) for ci in range(nchunk)]
    end_row = 0 if reverse else GLA_CHUNK - 1
    b_end = [b_inc[sl][end_row:end_row + 1, :] for sl in chunks]
    b_rest = jnp.concatenate([be - b_inc[sl] for be, sl in zip(b_end, chunks)], axis=0)

    q = q_ref[0]
    k = k_ref[0]
    if rope:
        q = _rope(q, cos_ref[...], sin_ref[...])
        k = _rope(k, cos_ref[...], sin_ref[...])
    q = q * (dk ** -0.5)
    qt = (q * jnp.exp(b_inc)).astype(BF16)
    kt = (k * jnp.exp(-b_inc)).astype(BF16)
    kd = (k * jnp.exp(b_rest)).astype(BF16)
    vb = v_ref[0].astype(BF16)

    lane_head = _shr(lax.broadcasted_iota(jnp.int32, (GLA_CHUNK, dkh), 1), dk)
    hc = GLA_HEADS * GLA_CHUNK
    rk = _shr(lax.broadcasted_iota(jnp.int32, (hc, dkh), 0), GLA_CHUNK)
    ck = _shr(lax.broadcasted_iota(jnp.int32, (hc, dkh), 1), dk)
    mask_k = rk == ck
    rv = _shr(lax.broadcasted_iota(jnp.int32, (hc, dvh), 0), GLA_CHUNK)
    cv = _shr(lax.broadcasted_iota(jnp.int32, (hc, dvh), 1), dv)
    mask_v = rv == cv
    ti = lax.broadcasted_iota(jnp.int32, (GLA_CHUNK, dkh), 0)
    tj = lax.broadcasted_iota(jnp.int32, (GLA_CHUNK, dkh), 1) & (GLA_CHUNK - 1)
    causal = (tj >= ti) if reverse else (tj <= ti)
    zero = jnp.zeros((), BF16)

    def head_stack(x):
        return jnp.concatenate([jnp.where(lane_head == h, x, zero) for h in range(GLA_HEADS)], axis=0)

    o_intra, kv, decay, q_stack = [], [], [], []
    for sl, be in zip(chunks, b_end):
        k_blk = jnp.where(mask_k, jnp.concatenate([kt[sl]] * GLA_HEADS, axis=0), zero)
        att = jnp.where(causal, _dot_nt(qt[sl], k_blk), 0.0)
        v_blk = jnp.where(mask_v, jnp.concatenate([vb[sl]] * GLA_HEADS, axis=0), zero)
        o_intra.append(_dot(att.astype(BF16), v_blk))
        v_rows = jnp.concatenate([vb[sl][:, h * dv:(h + 1) * dv] for h in range(GLA_HEADS)], axis=0)
        kv.append(_dot_tn(head_stack(kd[sl]), v_rows))
        col = jnp.exp(jnp.transpose(jnp.broadcast_to(be, (LANES, dkh))))
        decay.append(jnp.concatenate([col] * (dv // LANES), axis=1))
        q_stack.append(head_stack(qt[sl]))

    order = list(range(nchunk - 1, -1, -1)) if reverse else list(range(nchunk))
    state = s_ref[...]
    states = {}
    for ci in order:
        states[ci] = state
        state = decay[ci] * state + kv[ci]
    s_ref[...] = state

    for ci in range(nchunk):
        inter = _dot(q_stack[ci], states[ci].astype(BF16))
        inter = jnp.concatenate([inter[h * GLA_CHUNK:(h + 1) * GLA_CHUNK] for h in range(GLA_HEADS)], axis=1)
        o_ref[0, chunks[ci], :] = o_intra[ci] + inter

    @pl.when(step == pl.num_programs(1) - 1)
    def _():
        sfin_ref[0] = state


def _gla_call(p_b, cos, sin_signed, w2p, b2, s0, *, reverse, rope, dkh, dvh, name):
    b, t, _ = p_b.shape
    tg = min(GLA_TG, t)
    ng = t // tg
    lr_blk = (2 * dkh + 2 * dvh) // LANES
    dv = dvh // GLA_HEADS
    assert dv % LANES == 0

    def blk(i):
        return ng - 1 - i if reverse else i

    kern = functools.partial(_gla_kernel, reverse=reverse, rope=rope)
    return pl.pallas_call(
        kern,
        grid=(b, ng),
        in_specs=[
            pl.BlockSpec((1, tg, dkh), lambda n, i: (n, blk(i), 0)),
            pl.BlockSpec((1, tg, dkh), lambda n, i: (n, blk(i), 1)),
            pl.BlockSpec((1, tg, dvh), lambda n, i: (n, blk(i), 2 * dkh // dvh)),
            pl.BlockSpec((1, tg, LANES), lambda n, i: (n, blk(i), lr_blk)),
            pl.BlockSpec((tg, dkh), lambda n, i: (blk(i), 0)),
            pl.BlockSpec((tg, dkh), lambda n, i: (blk(i), 0)),
            pl.BlockSpec((LANES, dkh), lambda n, i: (0, 0)),
            pl.BlockSpec((1, dkh), lambda n, i: (0, 0)),
            pl.BlockSpec((1, dkh, dv), lambda n, i: (n, 0, 0)),
        ],
        out_specs=[
            pl.BlockSpec((1, tg, dvh), lambda n, i: (n, blk(i), 0)),
            pl.BlockSpec((1, dkh, dv), lambda n, i: (n, 0, 0)),
        ],
        out_shape=[
            jax.ShapeDtypeStruct((b, t, dvh), F32),
            jax.ShapeDtypeStruct((b, dkh, dv), F32),
        ],
        scratch_shapes=[pltpu.VMEM((dkh, dv), F32)],
        compiler_params=_params("parallel", "arbitrary"),
        name=name,
    )(p_b, p_b, p_b, p_b, cos, sin_signed, w2p, b2, s0)


def _rope_tables(l, dk):
    quarter = dk // 4
    pos = jnp.arange(l)
    rows_pos = (pos // GRID_W).astype(F32)
    cols_pos = (pos % GRID_W).astype(F32)
    inv = ROPE_BASE ** (-jnp.arange(0, 2 * quarter, 2, dtype=F32) / (2 * quarter))
    ang_r = rows_pos[:, None] * inv[None, :]
    ang_c = cols_pos[:, None] * inv[None, :]
    cos = jnp.concatenate([jnp.cos(ang_r)] * 2 + [jnp.cos(ang_c)] * 2, axis=1)
    sin = jnp.concatenate([-jnp.sin(ang_r), jnp.sin(ang_r), -jnp.sin(ang_c), jnp.sin(ang_c)], axis=1)
    return jnp.tile(cos, (1, GLA_HEADS)), jnp.tile(sin, (1, GLA_HEADS))


def _conv_kernel(a_ref, g_ref, z_ref, ap_ref, gp_ref, an_ref, gn_ref, w_ref, cb_ref, lg_ref, lb_ref,
                 o_ref, u_ref, us_ref):
    t = a_ref.shape[1]
    nshift = u_ref.shape[0] - SUBLANES
    step = pl.program_id(1)
    nstep = pl.num_programs(1)

    def glu(a, g):
        return a.astype(F32) * jax.nn.sigmoid(g.astype(F32))

    prev_ok = (step > 0).astype(F32)
    next_ok = (step < nstep - 1).astype(F32)
    u_ref[0:CONV_HALO, :] = glu(ap_ref[0], gp_ref[0]) * prev_ok
    u_ref[CONV_HALO:CONV_HALO + t, :] = glu(a_ref[0], g_ref[0])
    u_ref[CONV_HALO + t:2 * CONV_HALO + t, :] = glu(an_ref[0], gn_ref[0]) * next_ok

    for b in range(1, SUBLANES):
        us_ref[b - 1] = u_ref[b:b + nshift, :]

    off = CONV_HALO - CONV_K // 2
    for rc in range(t // CONV_RC):
        base = rc * CONV_RC
        acc = None
        for j in range(CONV_K):
            q8, b = divmod(off + j, SUBLANES)
            lo = base + q8 * SUBLANES
            src = u_ref[lo:lo + CONV_RC, :] if b == 0 else us_ref[b - 1, lo:lo + CONV_RC, :]
            term = src * w_ref[j:j + 1, :]
            acc = term if acc is None else acc + term
        acc = acc + cb_ref[...]
        mu = jnp.mean(acc, axis=-1, keepdims=True)
        xc = acc - mu
        var = jnp.mean(xc * xc, axis=-1, keepdims=True)
        y = xc * lax.rsqrt(var + LN_EPS) * lg_ref[...] + lb_ref[...]
        z = z_ref[0, base:base + CONV_RC, :].astype(F32)
        o_ref[0, base:base + CONV_RC, :] = (_silu(y) * _silu(z)).astype(o_ref.dtype)


def _conv_call(p, conv_w, conv_b, ln_g, ln_b, col0, d_c, name):
    b, t, _ = p.shape
    tt = min(CONV_T, t)
    nt = t // tt
    cb = col0 // d_c
    hb = tt // CONV_HALO
    nhalo = t // CONV_HALO
    w = jnp.zeros((CONV_K + 1, d_c), F32).at[:CONV_K].set(conv_w)

    def prev(i):
        return jnp.maximum(i * hb - 1, 0)

    def nxt(i):
        return jnp.minimum((i + 1) * hb, nhalo - 1)

    row = lambda v: v.reshape(1, d_c)
    return pl.pallas_call(
        _conv_kernel,
        grid=(b, nt),
        in_specs=[
            pl.BlockSpec((1, tt, d_c), lambda n, i: (n, i, cb)),
            pl.BlockSpec((1, tt, d_c), lambda n, i: (n, i, cb + 1)),
            pl.BlockSpec((1, tt, d_c), lambda n, i: (n, i, cb + 2)),
            pl.BlockSpec((1, CONV_HALO, d_c), lambda n, i: (n, prev(i), cb)),
            pl.BlockSpec((1, CONV_HALO, d_c), lambda n, i: (n, prev(i), cb + 1)),
            pl.BlockSpec((1, CONV_HALO, d_c), lambda n, i: (n, nxt(i), cb)),
            pl.BlockSpec((1, CONV_HALO, d_c), lambda n, i: (n, nxt(i), cb + 1)),
            pl.BlockSpec((CONV_K + 1, d_c), lambda n, i: (0, 0)),
            pl.BlockSpec((1, d_c), lambda n, i: (0, 0)),
            pl.BlockSpec((1, d_c), lambda n, i: (0, 0)),
            pl.BlockSpec((1, d_c), lambda n, i: (0, 0)),
        ],
        out_specs=pl.BlockSpec((1, tt, d_c), lambda n, i: (n, i, 0)),
        out_shape=jax.ShapeDtypeStruct((b, t, d_c), BF16),
        scratch_shapes=[pltpu.VMEM((tt + 2 * CONV_HALO, d_c), F32),
                        pltpu.VMEM((SUBLANES - 1, tt + 2 * CONV_HALO - SUBLANES, d_c), F32)],
        compiler_params=_params("parallel", "arbitrary"),
        name=name,
    )(p, p, p, p, p, p, p, w, row(conv_b), row(ln_g), row(ln_b))


def _out_kernel(oa_ref, of_ref, ob_ref, zb_ref, oc_ref, x_ref, gate_ref, gn_ref,
                wa_ref, wb_ref, wc_ref, lg_ref, lb_ref, *rest, alpha, emit_h):
    if emit_h:
        mod_ref, o_ref, h_ref, y0_ref, y1_ref = rest
    else:
        o_ref, y0_ref, y1_ref = rest
    tm = x_ref.shape[0]
    dvh = of_ref.shape[1]
    dv = dvh // GLA_HEADS
    step = pl.program_id(0)

    @pl.when(step == 0)
    def _():
        y1_ref[...] = jnp.zeros_like(y1_ref)

    def run(y_prev, y_cur):
        gate = gate_ref[0]
        if emit_h:
            shift = mod_ref[0, 0:1, :]
            scale1 = 1.0 + mod_ref[0, 1:2, :]
        for c in range(tm // LN_ROWS):
            sl = slice(c * LN_ROWS, (c + 1) * LN_ROWS)
            rv = alpha * x_ref[sl, :] + gate * y_prev[sl, :]
            xn = _layer_norm_rows(rv) * lg_ref[...] + lb_ref[...]
            o_ref[sl, :] = xn
            if emit_h:
                h_ref[sl, :] = (_layer_norm_rows(xn) * scale1 + shift).astype(BF16)

        s = of_ref[...] + ob_ref[...]
        parts = []
        for h in range(GLA_HEADS):
            sh = s[:, h * dv:(h + 1) * dv]
            ms = jnp.mean(sh * sh, axis=-1, keepdims=True)
            parts.append(sh * lax.rsqrt(ms + RMS_EPS))
        out_b = jnp.concatenate(parts, axis=1) * gn_ref[...] * _silu(zb_ref[...])
        y_cur[...] = (_dot(oa_ref[...], wa_ref[...]) + _dot(out_b.astype(BF16), wb_ref[...])
                      + _dot(oc_ref[...], wc_ref[...]))

    @pl.when(step % 2 == 0)
    def _():
        run(y1_ref, y0_ref)

    @pl.when(step % 2 == 1)
    def _():
        run(y0_ref, y1_ref)


def _out_call(oa, o_f, o_b, p_b, oc, x2, gate, gla_norm, w_out, ln_g, ln_b, next_mod, rows_per_gate, zb_col,
              alpha, name):
    m, d = x2.shape
    d_a, dvh, d_c = oa.shape[1], o_f.shape[1], oc.shape[1]
    tm = min(OUT_TM, m)
    per = rows_per_gate // tm
    emit_h = next_mod is not None
    wa = w_out[:d_a].astype(BF16)
    wb = w_out[d_a:d_a + dvh].astype(BF16)
    wc = w_out[d_a + dvh:].astype(BF16)
    gn = jnp.tile(gla_norm, GLA_HEADS).reshape(1, dvh)
    nt = m // tm
    cur = lambda i: jnp.minimum(i, nt - 1)
    prev = lambda i: jnp.maximum(i - 1, 0)
    full = lambda shape: pl.BlockSpec(shape, lambda i: (0,) * len(shape))
    rows = pl.BlockSpec((tm, d), lambda i: (prev(i), 0))
    in_specs = [
        pl.BlockSpec((tm, d_a), lambda i: (cur(i), 0)),
        pl.BlockSpec((tm, dvh), lambda i: (cur(i), 0)),
        pl.BlockSpec((tm, dvh), lambda i: (cur(i), 0)),
        pl.BlockSpec((tm, dvh), lambda i: (cur(i), zb_col // dvh)),
        pl.BlockSpec((tm, d_c), lambda i: (cur(i), 0)),
        rows,
        pl.BlockSpec((1, 1, d), lambda i: (prev(i) // per, 0, 0)),
        full((1, dvh)),
        full((d_a, d)),
        full((dvh, d)),
        full((d_c, d)),
        full((1, d)),
        full((1, d)),
    ]
    args = [oa, o_f, o_b, p_b, oc, x2, gate, gn, wa, wb, wc, ln_g.reshape(1, d), ln_b.reshape(1, d)]
    out_specs = rows
    out_shape = jax.ShapeDtypeStruct((m, d), F32)
    if emit_h:
        in_specs.append(pl.BlockSpec((1, 2, d), lambda i: (prev(i) // per, 0, 0)))
        args.append(next_mod)
        out_specs = [rows, rows]
        out_shape = [out_shape, jax.ShapeDtypeStruct((m, d), BF16)]
    return pl.pallas_call(
        functools.partial(_out_kernel, alpha=alpha, emit_h=emit_h),
        grid=(nt + 1,),
        in_specs=in_specs,
        out_specs=out_specs,
        out_shape=out_shape,
        scratch_shapes=[pltpu.VMEM((tm, d), F32), pltpu.VMEM((tm, d), F32)],
        compiler_params=_params("arbitrary"),
        name=name,
    )(*args)


def kernel(x, c, ctx, c_ctx, w_ada, b_ada, w_in, rpb, gla_w2, gla_b, gla_norm, conv_w, conv_b,
           conv_ln_g, conv_ln_b, w_out, post_ln_g, post_ln_b):
    b, l, d = x.shape
    lc = ctx.shape[1]
    depth = w_ada.shape[0]
    heads_a = rpb.shape[1]
    d_a = heads_a * NA_HEAD_DIM
    dkh = gla_w2.shape[-1]
    dvh = gla_norm.shape[-1] * GLA_HEADS
    d_c = conv_w.shape[-1]
    n_a = 4 * d_a
    n_b = 2 * dkh + 2 * dvh + 2 * GATE_RANK
    n_c = 3 * d_c
    alpha = (2 * depth) ** 0.25
    assert w_in.shape[-1] == n_a + n_b + n_c and d_a + dvh + d_c == w_out.shape[1]
    tn_a = PROJ_TN
    tn_c = n_c // 2
    n_bp = -(-n_b // PROJ_TN) * PROJ_TN
    assert n_a % PROJ_TN == 0 and n_a + n_bp <= w_in.shape[-1] and tn_c % LANES == 0

    cond = jnp.zeros((8, d), F32).at[:b].set(c).at[b].set(c_ctx)
    mod = _ada_call(cond, w_ada, b_ada)

    w_c = w_in[:, :, n_a + n_b:]
    na_bias = _na_bias_table(rpb, l // GRID_W)

    cos, sin_signed = _rope_tables(l, dkh // GLA_HEADS)
    ones_c = jnp.ones((lc, dkh), F32)
    zeros_c = jnp.zeros((lc, dkh), F32)

    def layer_mods(layer):
        shift, scale, gate = jnp.split(mod[layer], 3, axis=-1)
        mod_lat = jnp.stack([shift[:b], scale[:b]], axis=1)
        mod_ctx = jnp.stack([shift[b:b + 1], scale[b:b + 1]], axis=1)
        return mod_lat, mod_ctx, gate[:b].reshape(b, 1, d), gate[b:b + 1].reshape(1, 1, d)

    x2 = x.reshape(b * l, d)
    cx2 = ctx.reshape(b * lc, d)
    mod_lat, mod_ctx, gate_lat, gate_ctx = layer_mods(0)
    h2 = _ln_mod_call(x2, mod_lat, l, "ln_mod")
    hc2 = _ln_mod_call(cx2, mod_ctx, b * lc, "ln_mod_ctx")
    for layer in range(depth):
        last = layer == depth - 1
        if not last:
            next_lat, next_ctx, next_gate_lat, next_gate_ctx = layer_mods(layer + 1)
        else:
            next_lat = next_ctx = None

        def project(hh, rows_n, tag):
            pa = _proj_call(hh, w_in, layer, 0, n_a // tn_a, tn_a, BF16, "proj_a" + tag)
            pb = _proj_call(hh, w_in, layer, n_a // PROJ_TN, n_bp // PROJ_TN, PROJ_TN, F32, "proj_b" + tag)
            pc = _proj_call(hh, w_c, layer, 0, n_c // tn_c, tn_c, BF16, "proj_c" + tag)
            return pa.reshape(b, rows_n, n_a), pb.reshape(b, rows_n, n_bp), pc.reshape(b, rows_n, n_c)

        p_a, p_b, p_c = project(h2, l, "")
        pc_a, pc_b, pc_c = project(hc2, lc, "_ctx")

        out_a = _na_call(p_a, pc_a, na_bias[layer], d_a)

        w2p = [jnp.zeros((LANES, dkh), F32).at[i * GATE_RANK:(i + 1) * GATE_RANK].set(gla_w2[layer, i])
               for i in range(2)]
        b2 = [gla_b[layer, i].reshape(1, dkh) for i in range(2)]
        s0 = jnp.zeros((b, dkh, dvh // GLA_HEADS), F32)
        gla = functools.partial(_gla_call, dkh=dkh, dvh=dvh)
        oc_f, s_f = gla(pc_b, ones_c, zeros_c, w2p[0], b2[0], s0, reverse=False, rope=False, name="gla_ctx_fwd")
        oc_b, s_b = gla(pc_b, ones_c, zeros_c, w2p[1], b2[1], s0, reverse=True, rope=False, name="gla_ctx_bwd")
        o_f, _ = gla(p_b, cos, sin_signed, w2p[0], b2[0], s_f, reverse=False, rope=True, name="gla_fwd")
        o_b, _ = gla(p_b, cos, sin_signed, w2p[1], b2[1], s_b, reverse=True, rope=True, name="gla_bwd")

        out_c = _conv_call(p_c, conv_w[layer], conv_b[layer], conv_ln_g[layer], conv_ln_b[layer],
                           0, d_c, "conv")

        zb_col = 2 * dkh + dvh
        res = _out_call(out_a.reshape(b * l, d_a), o_f.reshape(b * l, dvh), o_b.reshape(b * l, dvh),
                        p_b.reshape(b * l, n_bp), out_c.reshape(b * l, d_c), x2, gate_lat,
                        gla_norm[layer], w_out[layer], post_ln_g[layer], post_ln_b[layer], next_lat,
                        l, zb_col, alpha, "out_proj")
        if last:
            x2 = res
        else:
            out_a_c = _ctx_attn_call(pc_a, d_a)
            out_c_c = _conv_call(pc_c, conv_w[layer], conv_b[layer], conv_ln_g[layer], conv_ln_b[layer],
                                 0, d_c, "conv_ctx")
            cx2, hc2 = _out_call(out_a_c.reshape(b * lc, d_a), oc_f.reshape(b * lc, dvh),
                                 oc_b.reshape(b * lc, dvh), pc_b.reshape(b * lc, n_bp),
                                 out_c_c.reshape(b * lc, d_c), cx2, gate_ctx, gla_norm[layer], w_out[layer],
                                 post_ln_g[layer], post_ln_b[layer], next_ctx, b * lc, zb_col, alpha,
                                 "out_proj_ctx")
            x2, h2 = res
            gate_lat, gate_ctx = next_gate_lat, next_gate_ctx
    return x2.reshape(b, l, d)
```

```python
import functools

import numpy as np
import jax
import jax.numpy as jnp
from jax import lax
from jax.experimental import pallas as pl
from jax.experimental.pallas import tpu as pltpu

F32 = jnp.float32
BF16 = jnp.bfloat16

GRID_W = 64
NA_HEAD_DIM = 64
NA_KH_MAX = 8
NA_KW = 16
GLA_HEADS = 4
GATE_RANK = 16
GATE_TAU = 16.0
GLA_CHUNK = 64
CONV_K = 31
ROPE_BASE = 10000.0
LN_EPS = 1e-5
RMS_EPS = 1e-6

LANES = 128
SUBLANES = 8
VMEM_LIMIT_BYTES = 56 * 1024 * 1024

NEG_BIG = -1e30
LOG2E = 1.4426950408889634

PROJ_TM = 1024
PROJ_TN = 1024
OUT_TM = 512
LN_TM = 512
LN_ROWS = 32
OUT_LN_ROWS = 16
LN_UNROLL = 4
ATT_TQ_ROWS = 16
ATT_ITEM_ROWS = 2
GLA_TG = 1024
GLA_CUMSUM_ROWS = 256
CONV_T = 512
CONV_HALO = 16
CONV_RC = 64


def _silu(x):
    return x * jax.nn.sigmoid(x)


def _dot(a, b):
    return jnp.dot(a, b, preferred_element_type=F32)


def _dot_nt(a, b):
    return lax.dot_general(a, b, (((1,), (1,)), ((), ())), preferred_element_type=F32)


def _dot_tn(a, b):
    return lax.dot_general(a, b, (((0,), (0,)), ((), ())), preferred_element_type=F32)


def _shr(x, pow2):
    shift = pow2.bit_length() - 1
    assert 1 << shift == pow2
    return jnp.right_shift(x, shift)


def _split_hi_lo(x):
    hi = x.astype(BF16)
    lo = (x - hi.astype(F32)).astype(BF16)
    return hi, lo


def _params(*sem, flags=None):
    return pltpu.CompilerParams(dimension_semantics=sem, vmem_limit_bytes=VMEM_LIMIT_BYTES, flags=flags)


def _ada_kernel(c_ref, w_ref, b_ref, o_ref):
    s = _silu(c_ref[...]).astype(BF16)
    o_ref[0] = _dot(s, w_ref[0].astype(BF16)) + b_ref[0]


def _ada_call(cond, w_ada, b_ada, tn=512):
    depth, d, n = w_ada.shape
    rows = cond.shape[0]
    return pl.pallas_call(
        _ada_kernel,
        grid=(depth, n // tn),
        in_specs=[
            pl.BlockSpec((rows, d), lambda l, j: (0, 0)),
            pl.BlockSpec((1, d, tn), lambda l, j: (l, 0, j)),
            pl.BlockSpec((1, 1, tn), lambda l, j: (l, 0, j)),
        ],
        out_specs=pl.BlockSpec((1, rows, tn), lambda l, j: (l, 0, j)),
        out_shape=jax.ShapeDtypeStruct((depth, rows, n), F32),
        compiler_params=_params("parallel", "parallel"),
        name="ada_mod",
    )(cond, w_ada, b_ada.reshape(depth, 1, n))


def _layer_norm_rows(v):
    mu = jnp.mean(v, axis=-1, keepdims=True)
    vc = v - mu
    var = jnp.mean(vc * vc, axis=-1, keepdims=True)
    return vc * lax.rsqrt(var + LN_EPS)


def _ln_mod_kernel(x_ref, mod_ref, h_ref):
    shift = mod_ref[0, 0:1, :]
    scale1 = 1.0 + mod_ref[0, 1:2, :]

    def body(i, carry):
        r0 = pl.multiple_of(i * LN_ROWS, LN_ROWS)
        hn = _layer_norm_rows(x_ref[pl.ds(r0, LN_ROWS), :])
        h_ref[pl.ds(r0, LN_ROWS), :] = (hn * scale1 + shift).astype(BF16)
        return carry

    lax.fori_loop(0, x_ref.shape[0] // LN_ROWS, body, 0, unroll=LN_UNROLL)


def _ln_mod_call(x2, mod, rows_per_mod, name):
    m, d = x2.shape
    tm = min(LN_TM, m)
    per = rows_per_mod // tm
    return pl.pallas_call(
        _ln_mod_kernel,
        grid=(m // tm,),
        in_specs=[
            pl.BlockSpec((tm, d), lambda i: (i, 0)),
            pl.BlockSpec((1, 2, d), lambda i: (i // per, 0, 0)),
        ],
        out_specs=pl.BlockSpec((tm, d), lambda i: (i, 0)),
        out_shape=jax.ShapeDtypeStruct((m, d), BF16),
        compiler_params=_params("parallel"),
        name=name,
    )(x2, mod)


def _matmul_kernel(h_ref, w_ref, o_ref, wb_ref):
    @pl.when(pl.program_id(1) == 0)
    def _():
        wb_ref[...] = w_ref[0].astype(BF16)

    o_ref[...] = _dot(h_ref[...], wb_ref[...]).astype(o_ref.dtype)


def _matmul_shift_kernel(h_ref, wa_ref, wb_ref, o_ref, wbf_ref, *, shift):
    @pl.when(pl.program_id(1) == 0)
    def _():
        tn = wbf_ref.shape[1]
        wbf_ref[:, :tn - shift] = wa_ref[0][:, shift:].astype(BF16)
        wbf_ref[:, tn - shift:] = wb_ref[0][:, :shift].astype(BF16)

    o_ref[...] = _dot(h_ref[...], wbf_ref[...]).astype(o_ref.dtype)


def _proj_call(h2, w3, layer, col0, ncols, tn, out_dtype, name):
    m, d = h2.shape
    tm = min(PROJ_TM, m)
    blk0, shift = divmod(col0, tn)
    nblk = ncols // tn
    assert ncols % tn == 0 and col0 + ncols <= w3.shape[2]
    w_spec = lambda extra: pl.BlockSpec((1, d, tn), lambda j, i: (layer, 0, blk0 + extra + j))
    if shift:
        kern, w_specs, w_args = functools.partial(_matmul_shift_kernel, shift=shift), [w_spec(0), w_spec(1)], [w3, w3]
    else:
        kern, w_specs, w_args = _matmul_kernel, [w_spec(0)], [w3]
    return pl.pallas_call(
        kern,
        grid=(nblk, m // tm),
        in_specs=[pl.BlockSpec((tm, d), lambda j, i: (i, 0))] + w_specs,
        out_specs=pl.BlockSpec((tm, tn), lambda j, i: (i, j)),
        out_shape=jax.ShapeDtypeStruct((m, ncols), out_dtype),
        scratch_shapes=[pltpu.VMEM((d, tn), BF16)],
        compiler_params=_params("arbitrary", "arbitrary"),
        name=name,
    )(h2, *w_args)


def _softmax_pv(q, lane_lo, pieces):
    outs = []
    for h in range(2):
        keep = lane_lo if h == 0 else jnp.logical_not(lane_lo)
        qh = jnp.where(keep, q, jnp.zeros_like(q)) * jnp.asarray(NA_HEAD_DIM ** -0.5, BF16)
        scores = []
        for k, _, bias in pieces:
            s = _dot_nt(qh, k)
            if bias is not None:
                s = s + bias[h]
            scores.append(s)
        m = scores[0].max(axis=-1, keepdims=True)
        for s in scores[1:]:
            m = jnp.maximum(m, s.max(axis=-1, keepdims=True))
        den = None
        acc = None
        for s, (_, v, _) in zip(scores, pieces):
            p = jnp.exp(s - m)
            ps = p.sum(axis=-1, keepdims=True)
            pv = _dot(p.astype(BF16), v)
            den = ps if den is None else den + ps
            acc = pv if acc is None else acc + pv
        outs.append(acc / den)
    return jnp.where(lane_lo, outs[0], outs[1])


def _na_kernel(q_ref, k_ref, v_ref, z_ref, kc_ref, vc_ref, comp_ref, o_ref, vaug_ref, bias_ref, *, rows):
    tq = q_ref.shape[1]
    kh = min(NA_KH_MAX, rows)
    nkeys = kh * GRID_W
    span = vaug_ref.shape[1]
    qi = pl.program_id(2)

    @pl.when((pl.program_id(1) == 0) & (qi == 0))
    def _():
        for h in range(2):
            comp = comp_ref[0, h]
            for d0 in range(bias_ref.shape[0]):
                bias_ref[d0, h] = comp[:, d0 * GRID_W:d0 * GRID_W + nkeys]

    lane_lo = lax.broadcasted_iota(jnp.int32, (tq, LANES), 1) < NA_HEAD_DIM
    q = q_ref[0]
    kc = kc_ref[0]
    vc = vc_ref[0]
    span_row = jnp.clip(qi * ATT_TQ_ROWS - kh // 2, 0, rows - span // GRID_W)
    starts, offs, d0s = [], [], []
    for i in range(ATT_TQ_ROWS):
        r = qi * ATT_TQ_ROWS + i
        rs = jnp.clip(r - kh // 2, 0, rows - kh)
        d0s.append(rs - r + (NA_KH_MAX - 1))
        starts.append(pl.multiple_of(rs * GRID_W, GRID_W))
        offs.append(pl.multiple_of((rs - span_row) * GRID_W, GRID_W))
    v_span = v_ref[0, pl.ds(pl.multiple_of(span_row * GRID_W, GRID_W), span), :]
    one = jnp.ones((), BF16)

    def head_lanes(shape, h):
        lo = lax.broadcasted_iota(jnp.int32, shape, 1) < NA_HEAD_DIM
        return lo if h == 0 else jnp.logical_not(lo)

    qhs, vcs = [], []
    for h in range(2):
        keep = lane_lo if h == 0 else jnp.logical_not(lane_lo)
        qf = jnp.where(keep, q, jnp.zeros_like(q)).astype(F32) * (NA_HEAD_DIM ** -0.5 * LOG2E)
        qhs.append(qf.astype(BF16))
        vaug_ref[h] = jnp.where(head_lanes(v_span.shape, h), v_span, one)
        vcs.append(jnp.where(head_lanes(vc.shape, h), vc, one))

    def stage_scores(h, rows_i):
        qs = qhs[h][rows_i[0] * GRID_W:(rows_i[-1] + 1) * GRID_W]
        s_loc = jnp.concatenate(
            [_dot_nt(qhs[h][i * GRID_W:(i + 1) * GRID_W], k_ref[0, pl.ds(starts[i], nkeys), :])
             + bias_ref[d0s[i], h] for i in rows_i], axis=0)
        return h, s_loc, _dot_nt(qs, kc)

    def stage_softmax(h, s_loc, s_ctx):
        m = jnp.maximum(s_loc.max(axis=-1, keepdims=True), s_ctx.max(axis=-1, keepdims=True))
        return h, jnp.exp2(s_loc - m).astype(BF16), jnp.exp2(s_ctx - m).astype(BF16)

    def stage_values(h, p_loc, p_ctx, rows_i):
        acc = jnp.concatenate(
            [_dot(p_loc[j * GRID_W:(j + 1) * GRID_W], vaug_ref[h, pl.ds(offs[i], nkeys), :])
             for j, i in enumerate(rows_i)], axis=0)
        acc = acc + _dot(p_ctx, vcs[h])
        return acc / pltpu.roll(acc, NA_HEAD_DIM, 1)

    groups = [list(range(g, g + ATT_ITEM_ROWS)) for g in range(0, ATT_TQ_ROWS, ATT_ITEM_ROWS)]
    items = [(h, g) for h in range(2) for g in groups]
    scores, probs, outs = {}, {}, {}
    for t in range(len(items) + 2):
        if t < len(items):
            scores[t] = stage_scores(*items[t])
        if 0 <= t - 1 < len(items):
            probs[t - 1] = stage_softmax(*scores.pop(t - 1))
        if 0 <= t - 2 < len(items):
            outs[t - 2] = stage_values(*probs.pop(t - 2), items[t - 2][1])
    per_head = [jnp.concatenate([outs[t] for t, (hh, _) in enumerate(items) if hh == h], axis=0)
                for h in range(2)]
    o = jnp.where(lane_lo, per_head[0], per_head[1])
    o_ref[0] = (o * _silu(z_ref[0].astype(F32))).astype(o_ref.dtype)


def _na_bias_table(rpb, rows):
    depth, heads = rpb.shape[:2]
    kh = min(NA_KH_MAX, rows)
    nd = 2 * NA_KH_MAX - kh
    for r in range(rows):
        assert 0 <= int(np.clip(r - kh // 2, 0, rows - kh)) - r + (NA_KH_MAX - 1) < nd

    c = np.arange(GRID_W)[:, None]
    kc = np.arange(GRID_W)[None, :]
    cs = np.clip(c - NA_KW // 2, 0, GRID_W - NA_KW)
    valid_c = (kc >= cs) & (kc < cs + NA_KW)
    dcol = kc - c + (NA_KW - 1)
    sel_col = ((dcol[None] == np.arange(2 * NA_KW - 1)[:, None, None]) & valid_c[None]).astype(np.float32)
    comp = jnp.einsum("lhde,eck->lhcdk", rpb, sel_col, precision=lax.Precision.HIGHEST)
    comp = jnp.where(valid_c[:, None, :], comp * LOG2E, NEG_BIG)
    ncomp = (2 * NA_KH_MAX - 1) * GRID_W
    comp = comp.reshape(depth, heads // 2, 2, GRID_W, ncomp)
    return jnp.pad(comp, ((0, 0),) * 4 + ((0, -ncomp % LANES),), constant_values=NEG_BIG)


def _na_call(p_lat, p_ctx, bias, d_a):
    b, l, _ = p_lat.shape
    lc = p_ctx.shape[1]
    rows = l // GRID_W
    assert rows % ATT_TQ_ROWS == 0
    tq = ATT_TQ_ROWS * GRID_W
    nq = l // tq
    npair = d_a // LANES
    kh = min(NA_KH_MAX, rows)
    nd = 2 * NA_KH_MAX - kh
    ncomp = bias.shape[-1]
    span_rows = min(ATT_TQ_ROWS + kh - 1, rows)

    return pl.pallas_call(
        functools.partial(_na_kernel, rows=rows),
        grid=(npair, b, nq),
        in_specs=[
            pl.BlockSpec((1, tq, LANES), lambda p, i, q: (i, q, p)),
            pl.BlockSpec((1, l, LANES), lambda p, i, q: (i, 0, npair + p)),
            pl.BlockSpec((1, l, LANES), lambda p, i, q: (i, 0, 2 * npair + p)),
            pl.BlockSpec((1, tq, LANES), lambda p, i, q: (i, q, 3 * npair + p)),
            pl.BlockSpec((1, lc, LANES), lambda p, i, q: (i, 0, npair + p)),
            pl.BlockSpec((1, lc, LANES), lambda p, i, q: (i, 0, 2 * npair + p)),
            pl.BlockSpec((1, 2, GRID_W, ncomp), lambda p, i, q: (p, 0, 0, 0)),
        ],
        out_specs=pl.BlockSpec((1, tq, LANES), lambda p, i, q: (i, q, p)),
        out_shape=jax.ShapeDtypeStruct((b, l, d_a), BF16),
        scratch_shapes=[pltpu.VMEM((2, span_rows * GRID_W, LANES), BF16),
                        pltpu.VMEM((nd, 2, GRID_W, kh * GRID_W), F32)],
        compiler_params=_params("arbitrary", "arbitrary", "arbitrary"),
        name="na_attention",
    )(p_lat, p_lat, p_lat, p_lat, p_ctx, p_ctx, bias)


def _ctx_attn_kernel(q_ref, k_ref, v_ref, z_ref, o_ref):
    tq = q_ref.shape[1]
    lane_lo = lax.broadcasted_iota(jnp.int32, (tq, LANES), 1) < NA_HEAD_DIM
    o = _softmax_pv(q_ref[0], lane_lo, [(k_ref[0], v_ref[0], None)])
    o_ref[0] = (o * _silu(z_ref[0].astype(F32))).astype(o_ref.dtype)


def _ctx_attn_call(p_ctx, d_a):
    b, lc, _ = p_ctx.shape
    npair = d_a // LANES
    return pl.pallas_call(
        _ctx_attn_kernel,
        grid=(b, npair),
        in_specs=[
            pl.BlockSpec((1, lc, LANES), lambda i, p: (i, 0, p)),
            pl.BlockSpec((1, lc, LANES), lambda i, p: (i, 0, npair + p)),
            pl.BlockSpec((1, lc, LANES), lambda i, p: (i, 0, 2 * npair + p)),
            pl.BlockSpec((1, lc, LANES), lambda i, p: (i, 0, 3 * npair + p)),
        ],
        out_specs=pl.BlockSpec((1, lc, LANES), lambda i, p: (i, 0, p)),
        out_shape=jax.ShapeDtypeStruct((b, lc, d_a), BF16),
        compiler_params=_params("parallel", "parallel"),
        name="ctx_attention",
    )(p_ctx, p_ctx, p_ctx, p_ctx)


def _rope(x, cos, sin_signed):
    halves = []
    for j in range(x.shape[1] // LANES):
        xs = x[:, j * LANES:(j + 1) * LANES]
        lane = lax.broadcasted_iota(jnp.int32, xs.shape, 1)
        first = (lane & 16) == 0
        partner = jnp.where(first, pltpu.roll(xs, LANES - 16, 1), pltpu.roll(xs, 16, 1))
        halves.append(partner)
    partner = jnp.concatenate(halves, axis=1)
    return x * cos + partner * sin_signed


def _gla_kernel(q_ref, k_ref, v_ref, lr_ref, cos_ref, sin_ref, w2_ref, b2_ref, s0_ref,
                o_ref, sfin_ref, s_ref, *, reverse, rope):
    tg = q_ref.shape[1]
    dkh = q_ref.shape[2]
    dvh = v_ref.shape[2]
    dk = dkh // GLA_HEADS
    dv = dvh // GLA_HEADS
    nchunk = tg // GLA_CHUNK
    step = pl.program_id(1)

    @pl.when(step == 0)
    def _():
        s_ref[...] = s0_ref[0]

    lr_hi, lr_lo = _split_hi_lo(lr_ref[0])
    w_hi, w_lo = _split_hi_lo(w2_ref[...])
    logits = _dot(lr_hi, w_hi) + _dot(lr_lo, w_hi) + _dot(lr_hi, w_lo) + b2_ref[...]
    g = (jnp.minimum(logits, 0.0) - jnp.log1p(jnp.exp(-jnp.abs(logits)))) * (1.0 / GATE_TAU)

    tcs = min(tg, GLA_CUMSUM_ROWS)
    r = lax.broadcasted_iota(jnp.int32, (tcs, tcs), 0)
    c = lax.broadcasted_iota(jnp.int32, (tcs, tcs), 1)
    same = _shr(r, GLA_CHUNK) == _shr(c, GLA_CHUNK)
    upto = (c >= r) if reverse else (c <= r)
    t_inc = jnp.where(same & upto, 1.0, 0.0).astype(BF16)
    g_hi, g_lo = _split_hi_lo(g)
    b_inc = jnp.concatenate(
        [_dot(t_inc, g_hi[s0:s0 + tcs]) + _dot(t_inc, g_lo[s0:s0 + tcs]) for s0 in range(0, tg, tcs)], axis=0)

    chunks = [slice(ci * GLA_CHUNK, (ci + 1) * GLA_CHUNK) for ci in range(nchunk)]
    end_row = 0 if reverse else GLA_CHUNK - 1
    b_end = [b_inc[sl][end_row:end_row + 1, :] for sl in chunks]
    b_rest = jnp.concatenate([be - b_inc[sl] for be, sl in zip(b_end, chunks)], axis=0)

    q = q_ref[0]
    k = k_ref[0]
    if rope:
        q = _rope(q, cos_ref[...], sin_ref[...])
        k = _rope(k, cos_ref[...], sin_ref[...])
    q = q * (dk ** -0.5)
    qt = (q * jnp.exp(b_inc)).astype(BF16)
    kt = (k * jnp.exp(-b_inc)).astype(BF16)
    kd = (k * jnp.exp(b_rest)).astype(BF16)
    vb = v_ref[0].astype(BF16)

    lane_head = _shr(lax.broadcasted_iota(jnp.int32, (GLA_CHUNK, dkh), 1), dk)
    hc = GLA_HEADS * GLA_CHUNK
    rk = _shr(lax.broadcasted_iota(jnp.int32, (hc, dkh), 0), GLA_CHUNK)
    ck = _shr(lax.broadcasted_iota(jnp.int32, (hc, dkh), 1), dk)
    mask_k = rk == ck
    rv = _shr(lax.broadcasted_iota(jnp.int32, (hc, dvh), 0), GLA_CHUNK)
    cv = _shr(lax.broadcasted_iota(jnp.int32, (hc, dvh), 1), dv)
    mask_v = rv == cv
    ti = lax.broadcasted_iota(jnp.int32, (GLA_CHUNK, dkh), 0)
    tj = lax.broadcasted_iota(jnp.int32, (GLA_CHUNK, dkh), 1) & (GLA_CHUNK - 1)
    causal = (tj >= ti) if reverse else (tj <= ti)
    zero = jnp.zeros((), BF16)

    def head_stack(x):
        return jnp.concatenate([jnp.where(lane_head == h, x, zero) for h in range(GLA_HEADS)], axis=0)

    o_intra, kv, decay, q_stack = [], [], [], []
    for sl, be in zip(chunks, b_end):
        k_blk = jnp.where(mask_k, jnp.concatenate([kt[sl]] * GLA_HEADS, axis=0), zero)
        att = jnp.where(causal, _dot_nt(qt[sl], k_blk), 0.0)
        v_blk = jnp.where(mask_v, jnp.concatenate([vb[sl]] * GLA_HEADS, axis=0), zero)
        o_intra.append(_dot(att.astype(BF16), v_blk))
        v_rows = jnp.concatenate([vb[sl][:, h * dv:(h + 1) * dv] for h in range(GLA_HEADS)], axis=0)
        kv.append(_dot_tn(head_stack(kd[sl]), v_rows))
        col = jnp.exp(jnp.transpose(jnp.broadcast_to(be, (LANES, dkh))))
        decay.append(jnp.concatenate([col] * (dv // LANES), axis=1))
        q_stack.append(head_stack(qt[sl]))

    order = list(range(nchunk - 1, -1, -1)) if reverse else list(range(nchunk))
    state = s_ref[...]
    states = {}
    for ci in order:
        states[ci] = state
        state = decay[ci] * state + kv[ci]
    s_ref[...] = state

    for ci in range(nchunk):
        inter = _dot(q_stack[ci], states[ci].astype(BF16))
        inter = jnp.concatenate([inter[h * GLA_CHUNK:(h + 1) * GLA_CHUNK] for h in range(GLA_HEADS)], axis=1)
        o_ref[0, chunks[ci], :] = o_intra[ci] + inter

    @pl.when(step == pl.num_programs(1) - 1)
    def _():
        sfin_ref[0] = state


def _gla_call(p_b, cos, sin_signed, w2p, b2, s0, *, reverse, rope, dkh, dvh, name):
    b, t, _ = p_b.shape
    tg = min(GLA_TG, t)
    ng = t // tg
    lr_blk = (2 * dkh + 2 * dvh) // LANES
    dv = dvh // GLA_HEADS
    assert dv % LANES == 0

    def blk(i):
        return ng - 1 - i if reverse else i

    kern = functools.partial(_gla_kernel, reverse=reverse, rope=rope)
    return pl.pallas_call(
        kern,
        grid=(b, ng),
        in_specs=[
            pl.BlockSpec((1, tg, dkh), lambda n, i: (n, blk(i), 0)),
            pl.BlockSpec((1, tg, dkh), lambda n, i: (n, blk(i), 1)),
            pl.BlockSpec((1, tg, dvh), lambda n, i: (n, blk(i), 2 * dkh // dvh)),
            pl.BlockSpec((1, tg, LANES), lambda n, i: (n, blk(i), lr_blk)),
            pl.BlockSpec((tg, dkh), lambda n, i: (blk(i), 0)),
            pl.BlockSpec((tg, dkh), lambda n, i: (blk(i), 0)),
            pl.BlockSpec((LANES, dkh), lambda n, i: (0, 0)),
            pl.BlockSpec((1, dkh), lambda n, i: (0, 0)),
            pl.BlockSpec((1, dkh, dv), lambda n, i: (n, 0, 0)),
        ],
        out_specs=[
            pl.BlockSpec((1, tg, dvh), lambda n, i: (n, blk(i), 0)),
            pl.BlockSpec((1, dkh, dv), lambda n, i: (n, 0, 0)),
        ],
        out_shape=[
            jax.ShapeDtypeStruct((b, t, dvh), F32),
            jax.ShapeDtypeStruct((b, dkh, dv), F32),
        ],
        scratch_shapes=[pltpu.VMEM((dkh, dv), F32)],
        compiler_params=_params("parallel", "arbitrary"),
        name=name,
    )(p_b, p_b, p_b, p_b, cos, sin_signed, w2p, b2, s0)


def _rope_tables(l, dk):
    quarter = dk // 4
    pos = jnp.arange(l)
    rows_pos = (pos // GRID_W).astype(F32)
    cols_pos = (pos % GRID_W).astype(F32)
    inv = ROPE_BASE ** (-jnp.arange(0, 2 * quarter, 2, dtype=F32) / (2 * quarter))
    ang_r = rows_pos[:, None] * inv[None, :]
    ang_c = cols_pos[:, None] * inv[None, :]
    cos = jnp.concatenate([jnp.cos(ang_r)] * 2 + [jnp.cos(ang_c)] * 2, axis=1)
    sin = jnp.concatenate([-jnp.sin(ang_r), jnp.sin(ang_r), -jnp.sin(ang_c), jnp.sin(ang_c)], axis=1)
    return jnp.tile(cos, (1, GLA_HEADS)), jnp.tile(sin, (1, GLA_HEADS))


def _conv_kernel(a_ref, g_ref, z_ref, ap_ref, gp_ref, an_ref, gn_ref, w_ref, cb_ref, lg_ref, lb_ref,
                 o_ref, u_ref, us_ref):
    t = a_ref.shape[1]
    nshift = u_ref.shape[0] - SUBLANES
    step = pl.program_id(1)
    nstep = pl.num_programs(1)

    def glu(a, g):
        return a.astype(F32) * jax.nn.sigmoid(g.astype(F32))

    prev_ok = (step > 0).astype(F32)
    next_ok = (step < nstep - 1).astype(F32)
    u_ref[0:CONV_HALO, :] = glu(ap_ref[0], gp_ref[0]) * prev_ok
    u_ref[CONV_HALO:CONV_HALO + t, :] = glu(a_ref[0], g_ref[0])
    u_ref[CONV_HALO + t:2 * CONV_HALO + t, :] = glu(an_ref[0], gn_ref[0]) * next_ok

    for b in range(1, SUBLANES):
        us_ref[b - 1] = u_ref[b:b + nshift, :]

    off = CONV_HALO - CONV_K // 2
    for rc in range(t // CONV_RC):
        base = rc * CONV_RC
        acc = None
        for j in range(CONV_K):
            q8, b = divmod(off + j, SUBLANES)
            lo = base + q8 * SUBLANES
            src = u_ref[lo:lo + CONV_RC, :] if b == 0 else us_ref[b - 1, lo:lo + CONV_RC, :]
            term = src * w_ref[j:j + 1, :]
            acc = term if acc is None else acc + term
        acc = acc + cb_ref[...]
        mu = jnp.mean(acc, axis=-1, keepdims=True)
        xc = acc - mu
        var = jnp.mean(xc * xc, axis=-1, keepdims=True)
        y = xc * lax.rsqrt(var + LN_EPS) * lg_ref[...] + lb_ref[...]
        z = z_ref[0, base:base + CONV_RC, :].astype(F32)
        o_ref[0, base:base + CONV_RC, :] = (_silu(y) * _silu(z)).astype(o_ref.dtype)


def _conv_call(p, conv_w, conv_b, ln_g, ln_b, col0, d_c, name):
    b, t, _ = p.shape
    tt = min(CONV_T, t)
    nt = t // tt
    cb = col0 // d_c
    hb = tt // CONV_HALO
    nhalo = t // CONV_HALO
    w = jnp.zeros((CONV_K + 1, d_c), F32).at[:CONV_K].set(conv_w)

    def prev(i):
        return jnp.maximum(i * hb - 1, 0)

    def nxt(i):
        return jnp.minimum((i + 1) * hb, nhalo - 1)

    row = lambda v: v.reshape(1, d_c)
    return pl.pallas_call(
        _conv_kernel,
        grid=(b, nt),
        in_specs=[
            pl.BlockSpec((1, tt, d_c), lambda n, i: (n, i, cb)),
            pl.BlockSpec((1, tt, d_c), lambda n, i: (n, i, cb + 1)),
            pl.BlockSpec((1, tt, d_c), lambda n, i: (n, i, cb + 2)),
            pl.BlockSpec((1, CONV_HALO, d_c), lambda n, i: (n, prev(i), cb)),
            pl.BlockSpec((1, CONV_HALO, d_c), lambda n, i: (n, prev(i), cb + 1)),
            pl.BlockSpec((1, CONV_HALO, d_c), lambda n, i: (n, nxt(i), cb)),
            pl.BlockSpec((1, CONV_HALO, d_c), lambda n, i: (n, nxt(i), cb + 1)),
            pl.BlockSpec((CONV_K + 1, d_c), lambda n, i: (0, 0)),
            pl.BlockSpec((1, d_c), lambda n, i: (0, 0)),
            pl.BlockSpec((1, d_c), lambda n, i: (0, 0)),
            pl.BlockSpec((1, d_c), lambda n, i: (0, 0)),
        ],
        out_specs=pl.BlockSpec((1, tt, d_c), lambda n, i: (n, i, 0)),
        out_shape=jax.ShapeDtypeStruct((b, t, d_c), BF16),
        scratch_shapes=[pltpu.VMEM((tt + 2 * CONV_HALO, d_c), F32),
                        pltpu.VMEM((SUBLANES - 1, tt + 2 * CONV_HALO - SUBLANES, d_c), F32)],
        compiler_params=_params("parallel", "arbitrary"),
        name=name,
    )(p, p, p, p, p, p, p, w, row(conv_b), row(ln_g), row(ln_b))


def _out_kernel(oa_ref, of_ref, ob_ref, zb_ref, oc_ref, x_ref, gate_ref, gn_ref,
                w_ref, lg_ref, lb_ref, *rest, alpha, emit_h):
    if emit_h:
        mod_ref, o_ref, h_ref, y0_ref, y1_ref, lhs_ref = rest
    else:
        o_ref, y0_ref, y1_ref, lhs_ref = rest
    tm = x_ref.shape[0]
    d_a = oa_ref.shape[1]
    dvh = of_ref.shape[1]
    dv = dvh // GLA_HEADS
    step = pl.program_id(0)

    @pl.when(step == 0)
    def _():
        y1_ref[...] = jnp.zeros_like(y1_ref)

    def run(y_prev, y_cur):
        gate = gate_ref[0]
        if emit_h:
            shift = mod_ref[0, 0:1, :]
            scale1 = 1.0 + mod_ref[0, 1:2, :]
        for c in range(tm // OUT_LN_ROWS):
            sl = slice(c * OUT_LN_ROWS, (c + 1) * OUT_LN_ROWS)
            rv = alpha * x_ref[sl, :] + gate * y_prev[sl, :]
            xn = _layer_norm_rows(rv) * lg_ref[...] + lb_ref[...]
            o_ref[sl, :] = xn
            if emit_h:
                h_ref[sl, :] = (_layer_norm_rows(xn) * scale1 + shift).astype(BF16)

        s = of_ref[...] + ob_ref[...]
        parts = []
        for h in range(GLA_HEADS):
            sh = s[:, h * dv:(h + 1) * dv]
            ms = jnp.mean(sh * sh, axis=-1, keepdims=True)
            parts.append(sh * lax.rsqrt(ms + RMS_EPS))
        out_b = jnp.concatenate(parts, axis=1) * gn_ref[...] * _silu(zb_ref[...])
        lhs_ref[:, :d_a] = oa_ref[...]
        lhs_ref[:, d_a:d_a + dvh] = out_b.astype(BF16)
        lhs_ref[:, d_a + dvh:] = oc_ref[...]
        y_cur[...] = _dot(lhs_ref[...], w_ref[...])

    @pl.when(step % 2 == 0)
    def _():
        run(y1_ref, y0_ref)

    @pl.when(step % 2 == 1)
    def _():
        run(y0_ref, y1_ref)


def _out_call(oa, o_f, o_b, p_b, oc, x2, gate, gla_norm, w_out, ln_g, ln_b, next_mod, rows_per_gate, zb_col,
              alpha, name):
    m, d = x2.shape
    d_a, dvh, d_c = oa.shape[1], o_f.shape[1], oc.shape[1]
    tm = min(OUT_TM, m)
    per = rows_per_gate // tm
    emit_h = next_mod is not None
    w_bf = w_out.astype(BF16)
    gn = jnp.tile(gla_norm, GLA_HEADS).reshape(1, dvh)
    nt = m // tm
    cur = lambda i: jnp.minimum(i, nt - 1)
    prev = lambda i: jnp.maximum(i - 1, 0)
    full = lambda shape: pl.BlockSpec(shape, lambda i: (0,) * len(shape))
    rows = pl.BlockSpec((tm, d), lambda i: (prev(i), 0))
    in_specs = [
        pl.BlockSpec((tm, d_a), lambda i: (cur(i), 0)),
        pl.BlockSpec((tm, dvh), lambda i: (cur(i), 0)),
        pl.BlockSpec((tm, dvh), lambda i: (cur(i), 0)),
        pl.BlockSpec((tm, dvh), lambda i: (cur(i), zb_col // dvh)),
        pl.BlockSpec((tm, d_c), lambda i: (cur(i), 0)),
        rows,
        pl.BlockSpec((1, 1, d), lambda i: (prev(i) // per, 0, 0)),
        full((1, dvh)),
        full((d_a + dvh + d_c, d)),
        full((1, d)),
        full((1, d)),
    ]
    args = [oa, o_f, o_b, p_b, oc, x2, gate, gn, w_bf, ln_g.reshape(1, d), ln_b.reshape(1, d)]
    out_specs = rows
    out_shape = jax.ShapeDtypeStruct((m, d), F32)
    if emit_h:
        in_specs.append(pl.BlockSpec((1, 2, d), lambda i: (prev(i) // per, 0, 0)))
        args.append(next_mod)
        out_specs = [rows, rows]
        out_shape = [out_shape, jax.ShapeDtypeStruct((m, d), BF16)]
    return pl.pallas_call(
        functools.partial(_out_kernel, alpha=alpha, emit_h=emit_h),
        grid=(nt + 1,),
        in_specs=in_specs,
        out_specs=out_specs,
        out_shape=out_shape,
        scratch_shapes=[pltpu.VMEM((tm, d), F32), pltpu.VMEM((tm, d), F32),
                        pltpu.VMEM((tm, d_a + dvh + d_c), BF16)],
        compiler_params=_params("arbitrary"),
        name=name,
    )(*args)


def kernel(x, c, ctx, c_ctx, w_ada, b_ada, w_in, rpb, gla_w2, gla_b, gla_norm, conv_w, conv_b,
           conv_ln_g, conv_ln_b, w_out, post_ln_g, post_ln_b):
    b, l, d = x.shape
    lc = ctx.shape[1]
    depth = w_ada.shape[0]
    heads_a = rpb.shape[1]
    d_a = heads_a * NA_HEAD_DIM
    dkh = gla_w2.shape[-1]
    dvh = gla_norm.shape[-1] * GLA_HEADS
    d_c = conv_w.shape[-1]
    n_a = 4 * d_a
    n_b = 2 * dkh + 2 * dvh + 2 * GATE_RANK
    n_c = 3 * d_c
    alpha = (2 * depth) ** 0.25
    assert w_in.shape[-1] == n_a + n_b + n_c and d_a + dvh + d_c == w_out.shape[1]
    tn_c = n_c // 3
    n_bp = -(-n_b // PROJ_TN) * PROJ_TN
    assert n_a % PROJ_TN == 0 and n_a + n_bp <= w_in.shape[-1] and tn_c % LANES == 0

    cond = jnp.zeros((8, d), F32).at[:b].set(c).at[b].set(c_ctx)
    mod = _ada_call(cond, w_ada, b_ada)

    na_bias = _na_bias_table(rpb, l // GRID_W)

    cos, sin_signed = _rope_tables(l, dkh // GLA_HEADS)
    ones_c = jnp.ones((lc, dkh), F32)
    zeros_c = jnp.zeros((lc, dkh), F32)

    def layer_mods(layer):
        shift, scale, gate = jnp.split(mod[layer], 3, axis=-1)
        mod_lat = jnp.stack([shift[:b], scale[:b]], axis=1)
        mod_ctx = jnp.stack([shift[b:b + 1], scale[b:b + 1]], axis=1)
        return mod_lat, mod_ctx, gate[:b].reshape(b, 1, d), gate[b:b + 1].reshape(1, 1, d)

    x2 = x.reshape(b * l, d)
    cx2 = ctx.reshape(b * lc, d)
    mod_lat, mod_ctx, gate_lat, gate_ctx = layer_mods(0)
    h2 = _ln_mod_call(x2, mod_lat, l, "ln_mod")
    hc2 = _ln_mod_call(cx2, mod_ctx, b * lc, "ln_mod_ctx")
    for layer in range(depth):
        last = layer == depth - 1
        if not last:
            next_lat, next_ctx, next_gate_lat, next_gate_ctx = layer_mods(layer + 1)
        else:
            next_lat = next_ctx = None

        def project(hh, rows_n, tag):
            pa = _proj_call(hh, w_in, layer, 0, n_a, PROJ_TN, BF16, "proj_a" + tag)
            pb = _proj_call(hh, w_in, layer, n_a, n_bp, PROJ_TN, F32, "proj_b" + tag)
            pc = _proj_call(hh, w_in, layer, n_a + n_b, n_c, tn_c, BF16, "proj_c" + tag)
            return pa.reshape(b, rows_n, n_a), pb.reshape(b, rows_n, n_bp), pc.reshape(b, rows_n, n_c)

        p_a, p_b, p_c = project(h2, l, "")
        pc_a, pc_b, pc_c = project(hc2, lc, "_ctx")

        out_a = _na_call(p_a, pc_a, na_bias[layer], d_a)

        w2p = [jnp.zeros((LANES, dkh), F32).at[i * GATE_RANK:(i + 1) * GATE_RANK].set(gla_w2[layer, i])
               for i in range(2)]
        b2 = [gla_b[layer, i].reshape(1, dkh) for i in range(2)]
        s0 = jnp.zeros((b, dkh, dvh // GLA_HEADS), F32)
        gla = functools.partial(_gla_call, dkh=dkh, dvh=dvh)
        oc_f, s_f = gla(pc_b, ones_c, zeros_c, w2p[0], b2[0], s0, reverse=False, rope=False, name="gla_ctx_fwd")
        oc_b, s_b = gla(pc_b, ones_c, zeros_c, w2p[1], b2[1], s0, reverse=True, rope=False, name="gla_ctx_bwd")
        o_f, _ = gla(p_b, cos, sin_signed, w2p[0], b2[0], s_f, reverse=False, rope=True, name="gla_fwd")
        o_b, _ = gla(p_b, cos, sin_signed, w2p[1], b2[1], s_b, reverse=True, rope=True, name="gla_bwd")

        out_c = _conv_call(p_c, conv_w[layer], conv_b[layer], conv_ln_g[layer], conv_ln_b[layer],
                           0, d_c, "conv")

        zb_col = 2 * dkh + dvh
        res = _out_call(out_a.reshape(b * l, d_a), o_f.reshape(b * l, dvh), o_b.reshape(b * l, dvh),
                        p_b.reshape(b * l, n_bp), out_c.reshape(b * l, d_c), x2, gate_lat,
                        gla_norm[layer], w_out[layer], post_ln_g[layer], post_ln_b[layer], next_lat,
                        l, zb_col, alpha, "out_proj")
        if last:
            x2 = res
        else:
            out_a_c = _ctx_attn_call(pc_a, d_a)
            out_c_c = _conv_call(pc_c, conv_w[layer], conv_b[layer], conv_ln_g[layer], conv_ln_b[layer],
                                 0, d_c, "conv_ctx")
            cx2, hc2 = _out_call(out_a_c.reshape(b * lc, d_a), oc_f.reshape(b * lc, dvh),
                                 oc_b.reshape(b * lc, dvh), pc_b.reshape(b * lc, n_bp),
                                 out_c_c.reshape(b * lc, d_c), cx2, gate_ctx, gla_norm[layer], w_out[layer],
                                 post_ln_g[layer], post_ln_b[layer], next_ctx, b * lc, zb_col, alpha,
                                 "out_proj_ctx")
            x2, h2 = res
            gate_lat, gate_ctx = next_gate_lat, next_gate_ctx
    return x2.reshape(b, l, d)
```

```python
import functools
import math

import numpy as np
import jax
import jax.numpy as jnp
from jax import lax
from jax.experimental import pallas as pl
from jax.experimental.pallas import tpu as pltpu

F32 = jnp.float32
BF16 = jnp.bfloat16

GRID_W = 64
NA_HEAD_DIM = 64
NA_KH_MAX = 8
NA_KW = 16
GLA_HEADS = 4
GATE_RANK = 16
GATE_TAU = 16.0
GLA_CHUNK = 64
CONV_K = 31
ROPE_BASE = 10000.0
LN_EPS = 1e-5
RMS_EPS = 1e-6

LANES = 128
SUBLANES = 8
VMEM_LIMIT_BYTES = 56 * 1024 * 1024

NEG_BIG = -1e30
LOG2E = 1.4426950408889634

PROJ_TM = 1024
PROJ_TN = 1024
PROJ_WT_ROWS = 256
OUT_TM = 512
LN_TM = 512
LN_ROWS = 32
OUT_LN_ROWS = 16
LN_UNROLL = 4
ATT_TQ_ROWS = 16
ATT_ITEM_ROWS = 2
ATT_SKEW = 1
GLA_TG = 1024
GLA_CUMSUM_ROWS = 256
CONV_T = 512
CONV_HALO = 16
CONV_RC = 64


def _silu(x):
    return x * jax.nn.sigmoid(x)


def _dot(a, b):
    return jnp.dot(a, b, preferred_element_type=F32)


def _dot_nt(a, b):
    return lax.dot_general(a, b, (((1,), (1,)), ((), ())), preferred_element_type=F32)


def _dot_tn(a, b):
    return lax.dot_general(a, b, (((0,), (0,)), ((), ())), preferred_element_type=F32)


def _shr(x, pow2):
    shift = pow2.bit_length() - 1
    assert 1 << shift == pow2
    return jnp.right_shift(x, shift)


def _split_hi_lo(x):
    hi = x.astype(BF16)
    lo = (x - hi.astype(F32)).astype(BF16)
    return hi, lo


def _params(*sem):
    return pltpu.CompilerParams(dimension_semantics=sem, vmem_limit_bytes=VMEM_LIMIT_BYTES)


def _ada_kernel(c_ref, w_ref, b_ref, o_ref):
    s = _silu(c_ref[...]).astype(BF16)
    o_ref[0] = _dot(s, w_ref[0].astype(BF16)) + b_ref[0]


def _ada_call(cond, w_ada, b_ada, tn=512):
    depth, d, n = w_ada.shape
    rows = cond.shape[0]
    return pl.pallas_call(
        _ada_kernel,
        grid=(depth, n // tn),
        in_specs=[
            pl.BlockSpec((rows, d), lambda l, j: (0, 0)),
            pl.BlockSpec((1, d, tn), lambda l, j: (l, 0, j)),
            pl.BlockSpec((1, 1, tn), lambda l, j: (l, 0, j)),
        ],
        out_specs=pl.BlockSpec((1, rows, tn), lambda l, j: (l, 0, j)),
        out_shape=jax.ShapeDtypeStruct((depth, rows, n), F32),
        compiler_params=_params("parallel", "parallel"),
        name="ada_mod",
    )(cond, w_ada, b_ada.reshape(depth, 1, n))


def _layer_norm_rows(v):
    mu = jnp.mean(v, axis=-1, keepdims=True)
    vc = v - mu
    var = jnp.mean(vc * vc, axis=-1, keepdims=True)
    return vc * lax.rsqrt(var + LN_EPS)


def _ln_mod_kernel(x_ref, mod_ref, h_ref):
    shift = mod_ref[0, 0:1, :]
    scale1 = 1.0 + mod_ref[0, 1:2, :]

    def body(i, carry):
        r0 = pl.multiple_of(i * LN_ROWS, LN_ROWS)
        hn = _layer_norm_rows(x_ref[pl.ds(r0, LN_ROWS), :])
        h_ref[pl.ds(r0, LN_ROWS), :] = (hn * scale1 + shift).astype(BF16)
        return carry

    lax.fori_loop(0, x_ref.shape[0] // LN_ROWS, body, 0, unroll=LN_UNROLL)


def _ln_mod_call(x2, mod, rows_per_mod, name):
    m, d = x2.shape
    tm = min(LN_TM, m)
    per = rows_per_mod // tm
    return pl.pallas_call(
        _ln_mod_kernel,
        grid=(m // tm,),
        in_specs=[
            pl.BlockSpec((tm, d), lambda i: (i, 0)),
            pl.BlockSpec((1, 2, d), lambda i: (i // per, 0, 0)),
        ],
        out_specs=pl.BlockSpec((tm, d), lambda i: (i, 0)),
        out_shape=jax.ShapeDtypeStruct((m, d), BF16),
        compiler_params=_params("parallel"),
        name=name,
    )(x2, mod)


def _matmul_kernel(h_ref, *refs, shift):
    *w_refs, o_ref, wb_ref = refs

    @pl.when(pl.program_id(1) == 0)
    def _():
        tn = wb_ref.shape[1]
        step_rows = math.gcd(tn, PROJ_WT_ROWS)
        for c0 in range(0, tn, step_rows):
            lo, hi = c0 + shift, c0 + shift + step_rows
            pieces = []
            if lo < tn:
                pieces.append(w_refs[0][0, lo:min(hi, tn), :])
            if hi > tn:
                pieces.append(w_refs[1][0, max(lo, tn) - tn:hi - tn, :])
            rows_f32 = pieces[0] if len(pieces) == 1 else jnp.concatenate(pieces, axis=0)
            wb_ref[:, c0:c0 + step_rows] = jnp.transpose(rows_f32).astype(BF16)

    o_ref[...] = _dot(h_ref[...], wb_ref[...]).astype(o_ref.dtype)


def _proj_call(h2, w_t, layer, col0, ncols, tn, out_dtype, name):
    m, d = h2.shape
    tm = min(PROJ_TM, m)
    blk0, shift = divmod(col0, tn)
    nblk = ncols // tn
    assert ncols % tn == 0 and col0 + ncols <= w_t.shape[1] and shift % SUBLANES == 0
    w_spec = lambda extra: pl.BlockSpec((1, tn, d), lambda j, i: (layer, blk0 + extra + j, 0))
    w_specs = [w_spec(0), w_spec(1)] if shift else [w_spec(0)]
    return pl.pallas_call(
        functools.partial(_matmul_kernel, shift=shift),
        grid=(nblk, m // tm),
        in_specs=[pl.BlockSpec((tm, d), lambda j, i: (i, 0))] + w_specs,
        out_specs=pl.BlockSpec((tm, tn), lambda j, i: (i, j)),
        out_shape=jax.ShapeDtypeStruct((m, ncols), out_dtype),
        scratch_shapes=[pltpu.VMEM((d, tn), BF16)],
        compiler_params=_params("arbitrary", "arbitrary"),
        name=name,
    )(h2, *([w_t] * len(w_specs)))


def _softmax_pv(q, lane_lo, pieces):
    outs = []
    for h in range(2):
        keep = lane_lo if h == 0 else jnp.logical_not(lane_lo)
        qh = jnp.where(keep, q, jnp.zeros_like(q)) * jnp.asarray(NA_HEAD_DIM ** -0.5, BF16)
        scores = []
        for k, _, bias in pieces:
            s = _dot_nt(qh, k)
            if bias is not None:
                s = s + bias[h]
            scores.append(s)
        m = scores[0].max(axis=-1, keepdims=True)
        for s in scores[1:]:
            m = jnp.maximum(m, s.max(axis=-1, keepdims=True))
        den = None
        acc = None
        for s, (_, v, _) in zip(scores, pieces):
            p = jnp.exp(s - m)
            ps = p.sum(axis=-1, keepdims=True)
            pv = _dot(p.astype(BF16), v)
            den = ps if den is None else den + ps
            acc = pv if acc is None else acc + pv
        outs.append(acc / den)
    return jnp.where(lane_lo, outs[0], outs[1])


def _na_kernel(q_ref, k_ref, v_ref, z_ref, kc_ref, vc_ref, comp_ref, o_ref, vaug_ref, bias_ref, *, rows):
    tq = q_ref.shape[1]
    kh = min(NA_KH_MAX, rows)
    nkeys = kh * GRID_W
    span = vaug_ref.shape[1]
    qi = pl.program_id(2)

    @pl.when((pl.program_id(1) == 0) & (qi == 0))
    def _():
        for h in range(2):
            comp = comp_ref[0, h]
            for d0 in range(bias_ref.shape[0]):
                bias_ref[d0, h] = comp[:, d0 * GRID_W:d0 * GRID_W + nkeys]

    lane_lo = lax.broadcasted_iota(jnp.int32, (tq, LANES), 1) < NA_HEAD_DIM
    q = q_ref[0]
    kc = kc_ref[0]
    vc = vc_ref[0]
    span_row = jnp.clip(qi * ATT_TQ_ROWS - kh // 2, 0, rows - span // GRID_W)
    starts, offs, d0s = [], [], []
    for i in range(ATT_TQ_ROWS):
        r = qi * ATT_TQ_ROWS + i
        rs = jnp.clip(r - kh // 2, 0, rows - kh)
        d0s.append(rs - r + (NA_KH_MAX - 1))
        starts.append(pl.multiple_of(rs * GRID_W, GRID_W))
        offs.append(pl.multiple_of((rs - span_row) * GRID_W, GRID_W))
    v_span = v_ref[0, pl.ds(pl.multiple_of(span_row * GRID_W, GRID_W), span), :]
    one = jnp.ones((), BF16)

    def head_lanes(shape, h):
        lo = lax.broadcasted_iota(jnp.int32, shape, 1) < NA_HEAD_DIM
        return lo if h == 0 else jnp.logical_not(lo)

    qhs, vcs = [], []
    for h in range(2):
        keep = lane_lo if h == 0 else jnp.logical_not(lane_lo)
        qf = jnp.where(keep, q, jnp.zeros_like(q)).astype(F32) * (NA_HEAD_DIM ** -0.5 * LOG2E)
        qhs.append(qf.astype(BF16))
        vaug_ref[h] = jnp.where(head_lanes(v_span.shape, h), v_span, one)
        vcs.append(jnp.where(head_lanes(vc.shape, h), vc, one))

    def stage_scores(h, rows_i):
        qs = qhs[h][rows_i[0] * GRID_W:(rows_i[-1] + 1) * GRID_W]
        s_loc = jnp.concatenate(
            [_dot_nt(qhs[h][i * GRID_W:(i + 1) * GRID_W], k_ref[0, pl.ds(starts[i], nkeys), :])
             + bias_ref[d0s[i], h] for i in rows_i], axis=0)
        return h, s_loc, _dot_nt(qs, kc)

    def stage_softmax(h, s_loc, s_ctx):
        m = jnp.maximum(s_loc.max(axis=-1, keepdims=True), s_ctx.max(axis=-1, keepdims=True))
        return h, jnp.exp2(s_loc - m).astype(BF16), jnp.exp2(s_ctx - m).astype(BF16)

    def stage_values(h, p_loc, p_ctx, rows_i):
        acc = jnp.concatenate(
            [_dot(p_loc[j * GRID_W:(j + 1) * GRID_W], vaug_ref[h, pl.ds(offs[i], nkeys), :])
             for j, i in enumerate(rows_i)], axis=0)
        acc = acc + _dot(p_ctx, vcs[h])
        return acc / pltpu.roll(acc, NA_HEAD_DIM, 1)

    groups = [list(range(g, g + ATT_ITEM_ROWS)) for g in range(0, ATT_TQ_ROWS, ATT_ITEM_ROWS)]
    items = [(h, g) for h in range(2) for g in groups]
    scores, probs, outs = {}, {}, {}
    sk = ATT_SKEW
    for t in range(len(items) + 2 * sk):
        if t < len(items):
            scores[t] = stage_scores(*items[t])
        if 0 <= t - sk < len(items):
            probs[t - sk] = stage_softmax(*scores.pop(t - sk))
        if 0 <= t - 2 * sk < len(items):
            outs[t - 2 * sk] = stage_values(*probs.pop(t - 2 * sk), items[t - 2 * sk][1])
    per_head = [jnp.concatenate([outs[t] for t, (hh, _) in enumerate(items) if hh == h], axis=0)
                for h in range(2)]
    o = jnp.where(lane_lo, per_head[0], per_head[1])
    o_ref[0] = (o * _silu(z_ref[0].astype(F32))).astype(o_ref.dtype)


def _na_bias_table(rpb, rows):
    depth, heads = rpb.shape[:2]
    kh = min(NA_KH_MAX, rows)
    nd = 2 * NA_KH_MAX - kh
    for r in range(rows):
        assert 0 <= int(np.clip(r - kh // 2, 0, rows - kh)) - r + (NA_KH_MAX - 1) < nd

    c = np.arange(GRID_W)[:, None]
    kc = np.arange(GRID_W)[None, :]
    cs = np.clip(c - NA_KW // 2, 0, GRID_W - NA_KW)
    valid_c = (kc >= cs) & (kc < cs + NA_KW)
    dcol = kc - c + (NA_KW - 1)
    sel_col = ((dcol[None] == np.arange(2 * NA_KW - 1)[:, None, None]) & valid_c[None]).astype(np.float32)
    comp = jnp.einsum("lhde,eck->lhcdk", rpb, sel_col, precision=lax.Precision.HIGHEST)
    comp = jnp.where(valid_c[:, None, :], comp * LOG2E, NEG_BIG)
    ncomp = (2 * NA_KH_MAX - 1) * GRID_W
    comp = comp.reshape(depth, heads // 2, 2, GRID_W, ncomp)
    return jnp.pad(comp, ((0, 0),) * 4 + ((0, -ncomp % LANES),), constant_values=NEG_BIG)


def _na_call(p_lat, p_ctx, bias, d_a):
    b, l, _ = p_lat.shape
    lc = p_ctx.shape[1]
    rows = l // GRID_W
    assert rows % ATT_TQ_ROWS == 0
    tq = ATT_TQ_ROWS * GRID_W
    nq = l // tq
    npair = d_a // LANES
    kh = min(NA_KH_MAX, rows)
    nd = 2 * NA_KH_MAX - kh
    ncomp = bias.shape[-1]
    span_rows = min(ATT_TQ_ROWS + kh - 1, rows)

    return pl.pallas_call(
        functools.partial(_na_kernel, rows=rows),
        grid=(npair, b, nq),
        in_specs=[
            pl.BlockSpec((1, tq, LANES), lambda p, i, q: (i, q, p)),
            pl.BlockSpec((1, l, LANES), lambda p, i, q: (i, 0, npair + p)),
            pl.BlockSpec((1, l, LANES), lambda p, i, q: (i, 0, 2 * npair + p)),
            pl.BlockSpec((1, tq, LANES), lambda p, i, q: (i, q, 3 * npair + p)),
            pl.BlockSpec((1, lc, LANES), lambda p, i, q: (i, 0, npair + p)),
            pl.BlockSpec((1, lc, LANES), lambda p, i, q: (i, 0, 2 * npair + p)),
            pl.BlockSpec((1, 2, GRID_W, ncomp), lambda p, i, q: (p, 0, 0, 0)),
        ],
        out_specs=pl.BlockSpec((1, tq, LANES), lambda p, i, q: (i, q, p)),
        out_shape=jax.ShapeDtypeStruct((b, l, d_a), BF16),
        scratch_shapes=[pltpu.VMEM((2, span_rows * GRID_W, LANES), BF16),
                        pltpu.VMEM((nd, 2, GRID_W, kh * GRID_W), F32)],
        compiler_params=_params("arbitrary", "arbitrary", "arbitrary"),
        name="na_attention",
    )(p_lat, p_lat, p_lat, p_lat, p_ctx, p_ctx, bias)


def _ctx_attn_kernel(q_ref, k_ref, v_ref, z_ref, o_ref):
    tq = q_ref.shape[1]
    lane_lo = lax.broadcasted_iota(jnp.int32, (tq, LANES), 1) < NA_HEAD_DIM
    o = _softmax_pv(q_ref[0], lane_lo, [(k_ref[0], v_ref[0], None)])
    o_ref[0] = (o * _silu(z_ref[0].astype(F32))).astype(o_ref.dtype)


def _ctx_attn_call(p_ctx, d_a):
    b, lc, _ = p_ctx.shape
    npair = d_a // LANES
    return pl.pallas_call(
        _ctx_attn_kernel,
        grid=(b, npair),
        in_specs=[
            pl.BlockSpec((1, lc, LANES), lambda i, p: (i, 0, p)),
            pl.BlockSpec((1, lc, LANES), lambda i, p: (i, 0, npair + p)),
            pl.BlockSpec((1, lc, LANES), lambda i, p: (i, 0, 2 * npair + p)),
            pl.BlockSpec((1, lc, LANES), lambda i, p: (i, 0, 3 * npair + p)),
        ],
        out_specs=pl.BlockSpec((1, lc, LANES), lambda i, p: (i, 0, p)),
        out_shape=jax.ShapeDtypeStruct((b, lc, d_a), BF16),
        compiler_params=_params("parallel", "parallel"),
        name="ctx_attention",
    )(p_ctx, p_ctx, p_ctx, p_ctx)


def _rope(x, cos, sin_signed):
    halves = []
    for j in range(x.shape[1] // LANES):
        xs = x[:, j * LANES:(j + 1) * LANES]
        lane = lax.broadcasted_iota(jnp.int32, xs.shape, 1)
        first = (lane & 16) == 0
        partner = jnp.where(first, pltpu.roll(xs, LANES - 16, 1), pltpu.roll(xs, 16, 1))
        halves.append(partner)
    partner = jnp.concatenate(halves, axis=1)
    return x * cos + partner * sin_signed


def _gla_kernel(q_ref, k_ref, v_ref, lr_ref, cos_ref, sin_ref, w2_ref, b2_ref, s0_ref,
                o_ref, sfin_ref, s_ref, *, reverse, rope):
    tg = q_ref.shape[1]
    dkh = q_ref.shape[2]
    dvh = v_ref.shape[2]
    dk = dkh // GLA_HEADS
    dv = dvh // GLA_HEADS
    nchunk = tg // GLA_CHUNK
    step = pl.program_id(1)

    @pl.when(step == 0)
    def _():
        s_ref[...] = s0_ref[0]

    lr_hi, lr_lo = _split_hi_lo(lr_ref[0])
    w_hi, w_lo = _split_hi_lo(w2_ref[...])
    logits = _dot(lr_hi, w_hi) + _dot(lr_lo, w_hi) + _dot(lr_hi, w_lo) + b2_ref[...]
    g = (jnp.minimum(logits, 0.0) - jnp.log1p(jnp.exp(-jnp.abs(logits)))) * (1.0 / GATE_TAU)

    tcs = min(tg, GLA_CUMSUM_ROWS)
    r = lax.broadcasted_iota(jnp.int32, (tcs, tcs), 0)
    c = lax.broadcasted_iota(jnp.int32, (tcs, tcs), 1)
    same = _shr(r, GLA_CHUNK) == _shr(c, GLA_CHUNK)
    upto = (c >= r) if reverse else (c <= r)
    t_inc = jnp.where(same & upto, 1.0, 0.0).astype(BF16)
    g_hi, g_lo = _split_hi_lo(g)
    b_inc = jnp.concatenate(
        [_dot(t_inc, g_hi[s0:s0 + tcs]) + _dot(t_inc, g_lo[s0:s0 + tcs]) for s0 in range(0, tg, tcs)], axis=0)

    chunks = [slice(ci * GLA_CHUNK, (ci + 1) * GLA_CHUNK) for ci in range(nchunk)]
    end_row = 0 if reverse else GLA_CHUNK - 1
    b_end = [b_inc[sl][end_row:end_row + 1, :] for sl in chunks]
    b_rest = jnp.concatenate([be - b_inc[sl] for be, sl in zip(b_end, chunks)], axis=0)

    q = q_ref[0]
    k = k_ref[0]
    if rope:
        q = _rope(q, cos_ref[...], sin_ref[...])
        k = _rope(k, cos_ref[...], sin_ref[...])
    q = q * (dk ** -0.5)
    qt = (q * jnp.exp(b_inc)).astype(BF16)
    kt = (k * jnp.exp(-b_inc)).astype(BF16)
    kd = (k * jnp.exp(b_rest)).astype(BF16)
    vb = v_ref[0].astype(BF16)

    lane_head = _shr(lax.broadcasted_iota(jnp.int32, (GLA_CHUNK, dkh), 1), dk)
    hc = GLA_HEADS * GLA_CHUNK
    rk = _shr(lax.broadcasted_iota(jnp.int32, (hc, dkh), 0), GLA_CHUNK)
    ck = _shr(lax.broadcasted_iota(jnp.int32, (hc, dkh), 1), dk)
    mask_k = rk == ck
    rv = _shr(lax.broadcasted_iota(jnp.int32, (hc, dvh), 0), GLA_CHUNK)
    cv = _shr(lax.broadcasted_iota(jnp.int32, (hc, dvh), 1), dv)
    mask_v = rv == cv
    ti = lax.broadcasted_iota(jnp.int32, (GLA_CHUNK, dkh), 0)
    tj = lax.broadcasted_iota(jnp.int32, (GLA_CHUNK, dkh), 1) & (GLA_CHUNK - 1)
    causal = (tj >= ti) if reverse else (tj <= ti)
    zero = jnp.zeros((), BF16)

    def head_stack(x):
        return jnp.concatenate([jnp.where(lane_head == h, x, zero) for h in range(GLA_HEADS)], axis=0)

    o_intra, kv, decay, q_stack = [], [], [], []
    for sl, be in zip(chunks, b_end):
        k_blk = jnp.where(mask_k, jnp.concatenate([kt[sl]] * GLA_HEADS, axis=0), zero)
        att = jnp.where(causal, _dot_nt(qt[sl], k_blk), 0.0)
        v_blk = jnp.where(mask_v, jnp.concatenate([vb[sl]] * GLA_HEADS, axis=0), zero)
        o_intra.append(_dot(att.astype(BF16), v_blk))
        v_rows = jnp.concatenate([vb[sl][:, h * dv:(h + 1) * dv] for h in range(GLA_HEADS)], axis=0)
        kv.append(_dot_tn(head_stack(kd[sl]), v_rows))
        col = jnp.exp(jnp.transpose(jnp.broadcast_to(be, (LANES, dkh))))
        decay.append(jnp.concatenate([col] * (dv // LANES), axis=1))
        q_stack.append(head_stack(qt[sl]))

    order = list(range(nchunk - 1, -1, -1)) if reverse else list(range(nchunk))
    state = s_ref[...]
    states = {}
    for ci in order:
        states[ci] = state
        state = decay[ci] * state + kv[ci]
    s_ref[...] = state

    for ci in range(nchunk):
        inter = _dot(q_stack[ci], states[ci].astype(BF16))
        inter = jnp.concatenate([inter[h * GLA_CHUNK:(h + 1) * GLA_CHUNK] for h in range(GLA_HEADS)], axis=1)
        o_ref[0, chunks[ci], :] = o_intra[ci] + inter

    @pl.when(step == pl.num_programs(1) - 1)
    def _():
        sfin_ref[0] = state


def _gla_call(p_b, cos, sin_signed, w2p, b2, s0, *, reverse, rope, dkh, dvh, name):
    b, t, _ = p_b.shape
    tg = min(GLA_TG, t)
    ng = t // tg
    lr_blk = (2 * dkh + 2 * dvh) // LANES
    dv = dvh // GLA_HEADS
    assert dv % LANES == 0

    def blk(i):
        return ng - 1 - i if reverse else i

    kern = functools.partial(_gla_kernel, reverse=reverse, rope=rope)
    return pl.pallas_call(
        kern,
        grid=(b, ng),
        in_specs=[
            pl.BlockSpec((1, tg, dkh), lambda n, i: (n, blk(i), 0)),
            pl.BlockSpec((1, tg, dkh), lambda n, i: (n, blk(i), 1)),
            pl.BlockSpec((1, tg, dvh), lambda n, i: (n, blk(i), 2 * dkh // dvh)),
            pl.BlockSpec((1, tg, LANES), lambda n, i: (n, blk(i), lr_blk)),
            pl.BlockSpec((tg, dkh), lambda n, i: (blk(i), 0)),
            pl.BlockSpec((tg, dkh), lambda n, i: (blk(i), 0)),
            pl.BlockSpec((LANES, dkh), lambda n, i: (0, 0)),
            pl.BlockSpec((1, dkh), lambda n, i: (0, 0)),
            pl.BlockSpec((1, dkh, dv), lambda n, i: (n, 0, 0)),
        ],
        out_specs=[
            pl.BlockSpec((1, tg, dvh), lambda n, i: (n, blk(i), 0)),
            pl.BlockSpec((1, dkh, dv), lambda n, i: (n, 0, 0)),
        ],
        out_shape=[
            jax.ShapeDtypeStruct((b, t, dvh), F32),
            jax.ShapeDtypeStruct((b, dkh, dv), F32),
        ],
        scratch_shapes=[pltpu.VMEM((dkh, dv), F32)],
        compiler_params=_params("parallel", "arbitrary"),
        name=name,
    )(p_b, p_b, p_b, p_b, cos, sin_signed, w2p, b2, s0)


def _rope_tables(l, dk):
    quarter = dk // 4
    pos = jnp.arange(l)
    rows_pos = (pos // GRID_W).astype(F32)
    cols_pos = (pos % GRID_W).astype(F32)
    inv = ROPE_BASE ** (-jnp.arange(0, 2 * quarter, 2, dtype=F32) / (2 * quarter))
    ang_r = rows_pos[:, None] * inv[None, :]
    ang_c = cols_pos[:, None] * inv[None, :]
    cos = jnp.concatenate([jnp.cos(ang_r)] * 2 + [jnp.cos(ang_c)] * 2, axis=1)
    sin = jnp.concatenate([-jnp.sin(ang_r), jnp.sin(ang_r), -jnp.sin(ang_c), jnp.sin(ang_c)], axis=1)
    return jnp.tile(cos, (1, GLA_HEADS)), jnp.tile(sin, (1, GLA_HEADS))


def _conv_kernel(a_ref, g_ref, z_ref, ap_ref, gp_ref, an_ref, gn_ref, w_ref, cb_ref, lg_ref, lb_ref,
                 o_ref, u_ref, us_ref):
    t = a_ref.shape[1]
    nshift = u_ref.shape[0] - SUBLANES
    step = pl.program_id(1)
    nstep = pl.num_programs(1)

    def glu(a, g):
        return a.astype(F32) * jax.nn.sigmoid(g.astype(F32))

    prev_ok = (step > 0).astype(F32)
    next_ok = (step < nstep - 1).astype(F32)
    u_ref[0:CONV_HALO, :] = glu(ap_ref[0], gp_ref[0]) * prev_ok
    u_ref[CONV_HALO:CONV_HALO + t, :] = glu(a_ref[0], g_ref[0])
    u_ref[CONV_HALO + t:2 * CONV_HALO + t, :] = glu(an_ref[0], gn_ref[0]) * next_ok

    for b in range(1, SUBLANES):
        us_ref[b - 1] = u_ref[b:b + nshift, :]

    off = CONV_HALO - CONV_K // 2
    for rc in range(t // CONV_RC):
        base = rc * CONV_RC
        acc = None
        for j in range(CONV_K):
            q8, b = divmod(off + j, SUBLANES)
            lo = base + q8 * SUBLANES
            src = u_ref[lo:lo + CONV_RC, :] if b == 0 else us_ref[b - 1, lo:lo + CONV_RC, :]
            term = src * w_ref[j:j + 1, :]
            acc = term if acc is None else acc + term
        acc = acc + cb_ref[...]
        mu = jnp.mean(acc, axis=-1, keepdims=True)
        xc = acc - mu
        var = jnp.mean(xc * xc, axis=-1, keepdims=True)
        y = xc * lax.rsqrt(var + LN_EPS) * lg_ref[...] + lb_ref[...]
        z = z_ref[0, base:base + CONV_RC, :].astype(F32)
        o_ref[0, base:base + CONV_RC, :] = (_silu(y) * _silu(z)).astype(o_ref.dtype)


def _conv_call(p, conv_w, conv_b, ln_g, ln_b, col0, d_c, name):
    b, t, _ = p.shape
    tt = min(CONV_T, t)
    nt = t // tt
    cb = col0 // d_c
    hb = tt // CONV_HALO
    nhalo = t // CONV_HALO
    w = jnp.zeros((CONV_K + 1, d_c), F32).at[:CONV_K].set(conv_w)

    def prev(i):
        return jnp.maximum(i * hb - 1, 0)

    def nxt(i):
        return jnp.minimum((i + 1) * hb, nhalo - 1)

    row = lambda v: v.reshape(1, d_c)
    return pl.pallas_call(
        _conv_kernel,
        grid=(b, nt),
        in_specs=[
            pl.BlockSpec((1, tt, d_c), lambda n, i: (n, i, cb)),
            pl.BlockSpec((1, tt, d_c), lambda n, i: (n, i, cb + 1)),
            pl.BlockSpec((1, tt, d_c), lambda n, i: (n, i, cb + 2)),
            pl.BlockSpec((1, CONV_HALO, d_c), lambda n, i: (n, prev(i), cb)),
            pl.BlockSpec((1, CONV_HALO, d_c), lambda n, i: (n, prev(i), cb + 1)),
            pl.BlockSpec((1, CONV_HALO, d_c), lambda n, i: (n, nxt(i), cb)),
            pl.BlockSpec((1, CONV_HALO, d_c), lambda n, i: (n, nxt(i), cb + 1)),
            pl.BlockSpec((CONV_K + 1, d_c), lambda n, i: (0, 0)),
            pl.BlockSpec((1, d_c), lambda n, i: (0, 0)),
            pl.BlockSpec((1, d_c), lambda n, i: (0, 0)),
            pl.BlockSpec((1, d_c), lambda n, i: (0, 0)),
        ],
        out_specs=pl.BlockSpec((1, tt, d_c), lambda n, i: (n, i, 0)),
        out_shape=jax.ShapeDtypeStruct((b, t, d_c), BF16),
        scratch_shapes=[pltpu.VMEM((tt + 2 * CONV_HALO, d_c), F32),
                        pltpu.VMEM((SUBLANES - 1, tt + 2 * CONV_HALO - SUBLANES, d_c), F32)],
        compiler_params=_params("parallel", "arbitrary"),
        name=name,
    )(p, p, p, p, p, p, p, w, row(conv_b), row(ln_g), row(ln_b))


def _out_kernel(oa_ref, of_ref, ob_ref, zb_ref, oc_ref, x_ref, gate_ref, gn_ref,
                w_ref, lg_ref, lb_ref, *rest, alpha, emit_h):
    if emit_h:
        mod_ref, o_ref, h_ref, y0_ref, y1_ref, lhs_ref = rest
    else:
        o_ref, y0_ref, y1_ref, lhs_ref = rest
    tm = x_ref.shape[0]
    d_a = oa_ref.shape[1]
    dvh = of_ref.shape[1]
    dv = dvh // GLA_HEADS
    step = pl.program_id(0)

    @pl.when(step == 0)
    def _():
        y1_ref[...] = jnp.zeros_like(y1_ref)

    def run(y_prev, y_cur):
        gate = gate_ref[0]
        if emit_h:
            shift = mod_ref[0, 0:1, :]
            scale1 = 1.0 + mod_ref[0, 1:2, :]
        for c in range(tm // OUT_LN_ROWS):
            sl = slice(c * OUT_LN_ROWS, (c + 1) * OUT_LN_ROWS)
            rv = alpha * x_ref[sl, :] + gate * y_prev[sl, :]
            xn = _layer_norm_rows(rv) * lg_ref[...] + lb_ref[...]
            o_ref[sl, :] = xn
            if emit_h:
                h_ref[sl, :] = (_layer_norm_rows(xn) * scale1 + shift).astype(BF16)

        s = of_ref[...] + ob_ref[...]
        parts = []
        for h in range(GLA_HEADS):
            sh = s[:, h * dv:(h + 1) * dv]
            ms = jnp.mean(sh * sh, axis=-1, keepdims=True)
            parts.append(sh * lax.rsqrt(ms + RMS_EPS))
        out_b = jnp.concatenate(parts, axis=1) * gn_ref[...] * _silu(zb_ref[...])
        lhs_ref[:, :d_a] = oa_ref[...]
        lhs_ref[:, d_a:d_a + dvh] = out_b.astype(BF16)
        lhs_ref[:, d_a + dvh:] = oc_ref[...]
        y_cur[...] = _dot(lhs_ref[...], w_ref[...])

    @pl.when(step % 2 == 0)
    def _():
        run(y1_ref, y0_ref)

    @pl.when(step % 2 == 1)
    def _():
        run(y0_ref, y1_ref)


def _out_call(oa, o_f, o_b, p_b, oc, x2, gate, gla_norm, w_out, ln_g, ln_b, next_mod, rows_per_gate, zb_col,
              alpha, name):
    m, d = x2.shape
    d_a, dvh, d_c = oa.shape[1], o_f.shape[1], oc.shape[1]
    tm = min(OUT_TM, m)
    per = rows_per_gate // tm
    emit_h = next_mod is not None
    w_bf = w_out.astype(BF16)
    gn = jnp.tile(gla_norm, GLA_HEADS).reshape(1, dvh)
    nt = m // tm
    cur = lambda i: jnp.minimum(i, nt - 1)
    prev = lambda i: jnp.maximum(i - 1, 0)
    full = lambda shape: pl.BlockSpec(shape, lambda i: (0,) * len(shape))
    rows = pl.BlockSpec((tm, d), lambda i: (prev(i), 0))
    in_specs = [
        pl.BlockSpec((tm, d_a), lambda i: (cur(i), 0)),
        pl.BlockSpec((tm, dvh), lambda i: (cur(i), 0)),
        pl.BlockSpec((tm, dvh), lambda i: (cur(i), 0)),
        pl.BlockSpec((tm, dvh), lambda i: (cur(i), zb_col // dvh)),
        pl.BlockSpec((tm, d_c), lambda i: (cur(i), 0)),
        rows,
        pl.BlockSpec((1, 1, d), lambda i: (prev(i) // per, 0, 0)),
        full((1, dvh)),
        full((d_a + dvh + d_c, d)),
        full((1, d)),
        full((1, d)),
    ]
    args = [oa, o_f, o_b, p_b, oc, x2, gate, gn, w_bf, ln_g.reshape(1, d), ln_b.reshape(1, d)]
    out_specs = rows
    out_shape = jax.ShapeDtypeStruct((m, d), F32)
    if emit_h:
        in_specs.append(pl.BlockSpec((1, 2, d), lambda i: (prev(i) // per, 0, 0)))
        args.append(next_mod)
        out_specs = [rows, rows]
        out_shape = [out_shape, jax.ShapeDtypeStruct((m, d), BF16)]
    return pl.pallas_call(
        functools.partial(_out_kernel, alpha=alpha, emit_h=emit_h),
        grid=(nt + 1,),
        in_specs=in_specs,
        out_specs=out_specs,
        out_shape=out_shape,
        scratch_shapes=[pltpu.VMEM((tm, d), F32), pltpu.VMEM((tm, d), F32),
                        pltpu.VMEM((tm, d_a + dvh + d_c), BF16)],
        compiler_params=_params("arbitrary"),
        name=name,
    )(*args)


def kernel(x, c, ctx, c_ctx, w_ada, b_ada, w_in, rpb, gla_w2, gla_b, gla_norm, conv_w, conv_b,
           conv_ln_g, conv_ln_b, w_out, post_ln_g, post_ln_b):
    b, l, d = x.shape
    lc = ctx.shape[1]
    depth = w_ada.shape[0]
    heads_a = rpb.shape[1]
    d_a = heads_a * NA_HEAD_DIM
    dkh = gla_w2.shape[-1]
    dvh = gla_norm.shape[-1] * GLA_HEADS
    d_c = conv_w.shape[-1]
    n_a = 4 * d_a
    n_b = 2 * dkh + 2 * dvh + 2 * GATE_RANK
    n_c = 3 * d_c
    alpha = (2 * depth) ** 0.25
    assert w_in.shape[-1] == n_a + n_b + n_c and d_a + dvh + d_c == w_out.shape[1]
    tn_c = n_c // 3
    n_bp = -(-n_b // (2 * LANES)) * (2 * LANES)
    tn_b = n_bp // 2
    assert n_a + n_bp <= w_in.shape[-1] and tn_c % LANES == 0 and tn_b % LANES == 0

    cond = jnp.zeros((8, d), F32).at[:b].set(c).at[b].set(c_ctx)
    mod = _ada_call(cond, w_ada, b_ada)

    na_bias = _na_bias_table(rpb, l // GRID_W)
    w_in_t = jnp.swapaxes(w_in, 1, 2)

    cos, sin_signed = _rope_tables(l, dkh // GLA_HEADS)
    ones_c = jnp.ones((lc, dkh), F32)
    zeros_c = jnp.zeros((lc, dkh), F32)

    def layer_mods(layer):
        shift, scale, gate = jnp.split(mod[layer], 3, axis=-1)
        mod_lat = jnp.stack([shift[:b], scale[:b]], axis=1)
        mod_ctx = jnp.stack([shift[b:b + 1], scale[b:b + 1]], axis=1)
        return mod_lat, mod_ctx, gate[:b].reshape(b, 1, d), gate[b:b + 1].reshape(1, 1, d)

    x2 = x.reshape(b * l, d)
    cx2 = ctx.reshape(b * lc, d)
    mod_lat, mod_ctx, gate_lat, gate_ctx = layer_mods(0)
    h2 = _ln_mod_call(x2, mod_lat, l, "ln_mod")
    hc2 = _ln_mod_call(cx2, mod_ctx, b * lc, "ln_mod_ctx")
    for layer in range(depth):
        last = layer == depth - 1
        if not last:
            next_lat, next_ctx, next_gate_lat, next_gate_ctx = layer_mods(layer + 1)
        else:
            next_lat = next_ctx = None

        def project(hh, rows_n, tag):
            pa = _proj_call(hh, w_in_t, layer, 0, n_a, PROJ_TN, BF16, "proj_a" + tag)
            pb = _proj_call(hh, w_in_t, layer, n_a, n_bp, tn_b, F32, "proj_b" + tag)
            pc = _proj_call(hh, w_in_t, layer, n_a + n_b, n_c, tn_c, BF16, "proj_c" + tag)
            return pa.reshape(b, rows_n, n_a), pb.reshape(b, rows_n, n_bp), pc.reshape(b, rows_n, n_c)

        p_a, p_b, p_c = project(h2, l, "")
        pc_a, pc_b, pc_c = project(hc2, lc, "_ctx")

        out_a = _na_call(p_a, pc_a, na_bias[layer], d_a)

        w2p = [jnp.zeros((LANES, dkh), F32).at[i * GATE_RANK:(i + 1) * GATE_RANK].set(gla_w2[layer, i])
               for i in range(2)]
        b2 = [gla_b[layer, i].reshape(1, dkh) for i in range(2)]
        s0 = jnp.zeros((b, dkh, dvh // GLA_HEADS), F32)
        gla = functools.partial(_gla_call, dkh=dkh, dvh=dvh)
        oc_f, s_f = gla(pc_b, ones_c, zeros_c, w2p[0], b2[0], s0, reverse=False, rope=False, name="gla_ctx_fwd")
        oc_b, s_b = gla(pc_b, ones_c, zeros_c, w2p[1], b2[1], s0, reverse=True, rope=False, name="gla_ctx_bwd")
        o_f, _ = gla(p_b, cos, sin_signed, w2p[0], b2[0], s_f, reverse=False, rope=True, name="gla_fwd")
        o_b, _ = gla(p_b, cos, sin_signed, w2p[1], b2[1], s_b, reverse=True, rope=True, name="gla_bwd")

        out_c = _conv_call(p_c, conv_w[layer], conv_b[layer], conv_ln_g[layer], conv_ln_b[layer],
                           0, d_c, "conv")

        zb_col = 2 * dkh + dvh
        res = _out_call(out_a.reshape(b * l, d_a), o_f.reshape(b * l, dvh), o_b.reshape(b * l, dvh),
                        p_b.reshape(b * l, n_bp), out_c.reshape(b * l, d_c), x2, gate_lat,
                        gla_norm[layer], w_out[layer], post_ln_g[layer], post_ln_b[layer], next_lat,
                        l, zb_col, alpha, "out_proj")
        if last:
            x2 = res
        else:
            out_a_c = _ctx_attn_call(pc_a, d_a)
            out_c_c = _conv_call(pc_c, conv_w[layer], conv_b[layer], conv_ln_g[layer], conv_ln_b[layer],
                                 0, d_c, "conv_ctx")
            cx2, hc2 = _out_call(out_a_c.reshape(b * lc, d_a), oc_f.reshape(b * lc, dvh),
                                 oc_b.reshape(b * lc, dvh), pc_b.reshape(b * lc, n_bp),
                                 out_c_c.reshape(b * lc, d_c), cx2, gate_ctx, gla_norm[layer], w_out[layer],
                                 post_ln_g[layer], post_ln_b[layer], next_ctx, b * lc, zb_col, alpha,
                                 "out_proj_ctx")
            x2, h2 = res
            gate_lat, gate_ctx = next_gate_lat, next_gate_ctx
    return x2.reshape(b, l, d)
```

```python
import functools
import itertools
import math

import numpy as np
import jax
import jax.numpy as jnp
from jax import lax
from jax.experimental import pallas as pl
from jax.experimental.pallas import tpu as pltpu

F32 = jnp.float32
BF16 = jnp.bfloat16

GRID_W = 64
NA_HEAD_DIM = 64
NA_KH_MAX = 8
NA_KW = 16
GLA_HEADS = 4
GATE_RANK = 16
GATE_TAU = 16.0
GLA_CHUNK = 64
CONV_K = 31
ROPE_BASE = 10000.0
LN_EPS = 1e-5
RMS_EPS = 1e-6

LANES = 128
SUBLANES = 8
VMEM_LIMIT_BYTES = 56 * 1024 * 1024

NEG_BIG = -1e30
LOG2E = 1.4426950408889634

PROJ_TM = 1024
PROJ_TN = 1024
PROJ_WT_ROWS = 256
OUT_TM = 512
LN_TM = 512
LN_ROWS = 32
OUT_LN_ROWS = 16
LN_UNROLL = 4
ATT_TQ_ROWS = 16
ATT_ITEM_ROWS = 2
ATT_SKEW = 1
GLA_TG = 512
GLA_CUMSUM_ROWS = 256
CONV_T = 512
CONV_HALO = 16
CONV_RC = 64


def _silu(x):
    return x * jax.nn.sigmoid(x)


def _dot(a, b):
    return jnp.dot(a, b, preferred_element_type=F32)


def _dot_nt(a, b):
    return lax.dot_general(a, b, (((1,), (1,)), ((), ())), preferred_element_type=F32)


def _dot_tn(a, b):
    return lax.dot_general(a, b, (((0,), (0,)), ((), ())), preferred_element_type=F32)


def _shr(x, pow2):
    shift = pow2.bit_length() - 1
    assert 1 << shift == pow2
    return jnp.right_shift(x, shift)


def _split_hi_lo(x):
    hi = x.astype(BF16)
    lo = (x - hi.astype(F32)).astype(BF16)
    return hi, lo


def _params(*sem):
    return pltpu.CompilerParams(dimension_semantics=sem, vmem_limit_bytes=VMEM_LIMIT_BYTES)


def _ada_kernel(c_ref, w_ref, b_ref, o_ref):
    s = _silu(c_ref[...]).astype(BF16)
    o_ref[0] = _dot(s, w_ref[0].astype(BF16)) + b_ref[0]


def _ada_call(cond, w_ada, b_ada, tn=512):
    depth, d, n = w_ada.shape
    rows = cond.shape[0]
    return pl.pallas_call(
        _ada_kernel,
        grid=(depth, n // tn),
        in_specs=[
            pl.BlockSpec((rows, d), lambda l, j: (0, 0)),
            pl.BlockSpec((1, d, tn), lambda l, j: (l, 0, j)),
            pl.BlockSpec((1, 1, tn), lambda l, j: (l, 0, j)),
        ],
        out_specs=pl.BlockSpec((1, rows, tn), lambda l, j: (l, 0, j)),
        out_shape=jax.ShapeDtypeStruct((depth, rows, n), F32),
        compiler_params=_params("parallel", "parallel"),
        name="ada_mod",
    )(cond, w_ada, b_ada.reshape(depth, 1, n))


def _layer_norm_rows(v):
    mu = jnp.mean(v, axis=-1, keepdims=True)
    vc = v - mu
    var = jnp.mean(vc * vc, axis=-1, keepdims=True)
    return vc * lax.rsqrt(var + LN_EPS)


def _ln_mod_kernel(x_ref, mod_ref, h_ref):
    shift = mod_ref[0, 0:1, :]
    scale1 = 1.0 + mod_ref[0, 1:2, :]

    def body(i, carry):
        r0 = pl.multiple_of(i * LN_ROWS, LN_ROWS)
        hn = _layer_norm_rows(x_ref[pl.ds(r0, LN_ROWS), :])
        h_ref[pl.ds(r0, LN_ROWS), :] = (hn * scale1 + shift).astype(BF16)
        return carry

    lax.fori_loop(0, x_ref.shape[0] // LN_ROWS, body, 0, unroll=LN_UNROLL)


def _ln_mod_call(x2, mod, rows_per_mod, name):
    m, d = x2.shape
    tm = min(LN_TM, m)
    per = rows_per_mod // tm
    return pl.pallas_call(
        _ln_mod_kernel,
        grid=(m // tm,),
        in_specs=[
            pl.BlockSpec((tm, d), lambda i: (i, 0)),
            pl.BlockSpec((1, 2, d), lambda i: (i // per, 0, 0)),
        ],
        out_specs=pl.BlockSpec((tm, d), lambda i: (i, 0)),
        out_shape=jax.ShapeDtypeStruct((m, d), BF16),
        compiler_params=_params("parallel"),
        name=name,
    )(x2, mod)


def _matmul_kernel(h_ref, *refs, shift):
    *w_refs, o_ref, wb_ref = refs

    @pl.when(pl.program_id(1) == 0)
    def _():
        tn = wb_ref.shape[1]
        step_rows = math.gcd(tn, PROJ_WT_ROWS)
        for c0 in range(0, tn, step_rows):
            lo, hi = c0 + shift, c0 + shift + step_rows
            pieces = []
            if lo < tn:
                pieces.append(w_refs[0][0, lo:min(hi, tn), :])
            if hi > tn:
                pieces.append(w_refs[1][0, max(lo, tn) - tn:hi - tn, :])
            rows_f32 = pieces[0] if len(pieces) == 1 else jnp.concatenate(pieces, axis=0)
            wb_ref[:, c0:c0 + step_rows] = jnp.transpose(rows_f32).astype(BF16)

    o_ref[...] = _dot(h_ref[...], wb_ref[...]).astype(o_ref.dtype)


def _proj_call(h2, w_t, layer, col0, ncols, tn, out_dtype, name):
    m, d = h2.shape
    tm = min(PROJ_TM, m)
    blk0, shift = divmod(col0, tn)
    nblk = ncols // tn
    assert ncols % tn == 0 and col0 + ncols <= w_t.shape[1] and shift % SUBLANES == 0
    w_spec = lambda extra: pl.BlockSpec((1, tn, d), lambda j, i: (layer, blk0 + extra + j, 0))
    w_specs = [w_spec(0), w_spec(1)] if shift else [w_spec(0)]
    return pl.pallas_call(
        functools.partial(_matmul_kernel, shift=shift),
        grid=(nblk, m // tm),
        in_specs=[pl.BlockSpec((tm, d), lambda j, i: (i, 0))] + w_specs,
        out_specs=pl.BlockSpec((tm, tn), lambda j, i: (i, j)),
        out_shape=jax.ShapeDtypeStruct((m, ncols), out_dtype),
        scratch_shapes=[pltpu.VMEM((d, tn), BF16)],
        compiler_params=_params("arbitrary", "arbitrary"),
        name=name,
    )(h2, *([w_t] * len(w_specs)))


def _softmax_pv(q, lane_lo, pieces):
    outs = []
    for h in range(2):
        keep = lane_lo if h == 0 else jnp.logical_not(lane_lo)
        qh = jnp.where(keep, q, jnp.zeros_like(q)) * jnp.asarray(NA_HEAD_DIM ** -0.5, BF16)
        scores = []
        for k, _, bias in pieces:
            s = _dot_nt(qh, k)
            if bias is not None:
                s = s + bias[h]
            scores.append(s)
        m = scores[0].max(axis=-1, keepdims=True)
        for s in scores[1:]:
            m = jnp.maximum(m, s.max(axis=-1, keepdims=True))
        den = None
        acc = None
        for s, (_, v, _) in zip(scores, pieces):
            p = jnp.exp(s - m)
            ps = p.sum(axis=-1, keepdims=True)
            pv = _dot(p.astype(BF16), v)
            den = ps if den is None else den + ps
            acc = pv if acc is None else acc + pv
        outs.append(acc / den)
    return jnp.where(lane_lo, outs[0], outs[1])


def _na_kernel(q_ref, k_ref, v_ref, z_ref, kc_ref, vc_ref, comp_ref, o_ref, vaug_ref, bias_ref, *, rows):
    tq = q_ref.shape[1]
    kh = min(NA_KH_MAX, rows)
    nkeys = kh * GRID_W
    span = vaug_ref.shape[1]
    qi = pl.program_id(2)

    @pl.when((pl.program_id(1) == 0) & (qi == 0))
    def _():
        for h in range(2):
            comp = comp_ref[0, h]
            for d0 in range(bias_ref.shape[0]):
                bias_ref[d0, h] = comp[:, d0 * GRID_W:d0 * GRID_W + nkeys]

    lane_lo = lax.broadcasted_iota(jnp.int32, (tq, LANES), 1) < NA_HEAD_DIM
    q = q_ref[0]
    kc = kc_ref[0]
    vc = vc_ref[0]
    span_row = jnp.clip(qi * ATT_TQ_ROWS - kh // 2, 0, rows - span // GRID_W)
    starts, offs, d0s = [], [], []
    for i in range(ATT_TQ_ROWS):
        r = qi * ATT_TQ_ROWS + i
        rs = jnp.clip(r - kh // 2, 0, rows - kh)
        d0s.append(rs - r + (NA_KH_MAX - 1))
        starts.append(pl.multiple_of(rs * GRID_W, GRID_W))
        offs.append(pl.multiple_of((rs - span_row) * GRID_W, GRID_W))
    v_span = v_ref[0, pl.ds(pl.multiple_of(span_row * GRID_W, GRID_W), span), :]
    one = jnp.ones((), BF16)

    def head_lanes(shape, h):
        lo = lax.broadcasted_iota(jnp.int32, shape, 1) < NA_HEAD_DIM
        return lo if h == 0 else jnp.logical_not(lo)

    qhs, vcs = [], []
    for h in range(2):
        keep = lane_lo if h == 0 else jnp.logical_not(lane_lo)
        qf = jnp.where(keep, q, jnp.zeros_like(q)).astype(F32) * (NA_HEAD_DIM ** -0.5 * LOG2E)
        qhs.append(qf.astype(BF16))
        vaug_ref[h] = jnp.where(head_lanes(v_span.shape, h), v_span, one)
        vcs.append(jnp.where(head_lanes(vc.shape, h), vc, one))

    def stage_scores(h, rows_i):
        qs = qhs[h][rows_i[0] * GRID_W:(rows_i[-1] + 1) * GRID_W]
        s_loc = jnp.concatenate(
            [_dot_nt(qhs[h][i * GRID_W:(i + 1) * GRID_W], k_ref[0, pl.ds(starts[i], nkeys), :])
             + bias_ref[d0s[i], h] for i in rows_i], axis=0)
        return h, s_loc, _dot_nt(qs, kc)

    def stage_softmax(h, s_loc, s_ctx):
        m = jnp.maximum(s_loc.max(axis=-1, keepdims=True), s_ctx.max(axis=-1, keepdims=True))
        return h, jnp.exp2(s_loc - m).astype(BF16), jnp.exp2(s_ctx - m).astype(BF16)

    def stage_values(h, p_loc, p_ctx, rows_i):
        acc = jnp.concatenate(
            [_dot(p_loc[j * GRID_W:(j + 1) * GRID_W], vaug_ref[h, pl.ds(offs[i], nkeys), :])
             for j, i in enumerate(rows_i)], axis=0)
        acc = acc + _dot(p_ctx, vcs[h])
        return acc / pltpu.roll(acc, NA_HEAD_DIM, 1)

    groups = [list(range(g, g + ATT_ITEM_ROWS)) for g in range(0, ATT_TQ_ROWS, ATT_ITEM_ROWS)]
    items = [(h, g) for h in range(2) for g in groups]
    scores, probs, outs = {}, {}, {}
    sk = ATT_SKEW
    for t in range(len(items) + 2 * sk):
        if t < len(items):
            scores[t] = stage_scores(*items[t])
        if 0 <= t - sk < len(items):
            probs[t - sk] = stage_softmax(*scores.pop(t - sk))
        if 0 <= t - 2 * sk < len(items):
            outs[t - 2 * sk] = stage_values(*probs.pop(t - 2 * sk), items[t - 2 * sk][1])
    per_head = [jnp.concatenate([outs[t] for t, (hh, _) in enumerate(items) if hh == h], axis=0)
                for h in range(2)]
    o = jnp.where(lane_lo, per_head[0], per_head[1])
    o_ref[0] = (o * _silu(z_ref[0].astype(F32))).astype(o_ref.dtype)


def _na_bias_table(rpb, rows):
    depth, heads = rpb.shape[:2]
    kh = min(NA_KH_MAX, rows)
    nd = 2 * NA_KH_MAX - kh
    for r in range(rows):
        assert 0 <= int(np.clip(r - kh // 2, 0, rows - kh)) - r + (NA_KH_MAX - 1) < nd

    c = np.arange(GRID_W)[:, None]
    kc = np.arange(GRID_W)[None, :]
    cs = np.clip(c - NA_KW // 2, 0, GRID_W - NA_KW)
    valid_c = (kc >= cs) & (kc < cs + NA_KW)
    dcol = kc - c + (NA_KW - 1)
    sel_col = ((dcol[None] == np.arange(2 * NA_KW - 1)[:, None, None]) & valid_c[None]).astype(np.float32)
    comp = jnp.einsum("lhde,eck->lhcdk", rpb, sel_col, precision=lax.Precision.HIGHEST)
    comp = jnp.where(valid_c[:, None, :], comp * LOG2E, NEG_BIG)
    ncomp = (2 * NA_KH_MAX - 1) * GRID_W
    comp = comp.reshape(depth, heads // 2, 2, GRID_W, ncomp)
    return jnp.pad(comp, ((0, 0),) * 4 + ((0, -ncomp % LANES),), constant_values=NEG_BIG)


def _na_call(p_lat, p_ctx, bias, d_a):
    b, l, _ = p_lat.shape
    lc = p_ctx.shape[1]
    rows = l // GRID_W
    assert rows % ATT_TQ_ROWS == 0
    tq = ATT_TQ_ROWS * GRID_W
    nq = l // tq
    npair = d_a // LANES
    kh = min(NA_KH_MAX, rows)
    nd = 2 * NA_KH_MAX - kh
    ncomp = bias.shape[-1]
    span_rows = min(ATT_TQ_ROWS + kh - 1, rows)

    return pl.pallas_call(
        functools.partial(_na_kernel, rows=rows),
        grid=(npair, b, nq),
        in_specs=[
            pl.BlockSpec((1, tq, LANES), lambda p, i, q: (i, q, p)),
            pl.BlockSpec((1, l, LANES), lambda p, i, q: (i, 0, npair + p)),
            pl.BlockSpec((1, l, LANES), lambda p, i, q: (i, 0, 2 * npair + p)),
            pl.BlockSpec((1, tq, LANES), lambda p, i, q: (i, q, 3 * npair + p)),
            pl.BlockSpec((1, lc, LANES), lambda p, i, q: (i, 0, npair + p)),
            pl.BlockSpec((1, lc, LANES), lambda p, i, q: (i, 0, 2 * npair + p)),
            pl.BlockSpec((1, 2, GRID_W, ncomp), lambda p, i, q: (p, 0, 0, 0)),
        ],
        out_specs=pl.BlockSpec((1, tq, LANES), lambda p, i, q: (i, q, p)),
        out_shape=jax.ShapeDtypeStruct((b, l, d_a), BF16),
        scratch_shapes=[pltpu.VMEM((2, span_rows * GRID_W, LANES), BF16),
                        pltpu.VMEM((nd, 2, GRID_W, kh * GRID_W), F32)],
        compiler_params=_params("arbitrary", "arbitrary", "arbitrary"),
        name="na_attention",
    )(p_lat, p_lat, p_lat, p_lat, p_ctx, p_ctx, bias)


def _ctx_attn_kernel(q_ref, k_ref, v_ref, z_ref, o_ref):
    tq = q_ref.shape[1]
    lane_lo = lax.broadcasted_iota(jnp.int32, (tq, LANES), 1) < NA_HEAD_DIM
    o = _softmax_pv(q_ref[0], lane_lo, [(k_ref[0], v_ref[0], None)])
    o_ref[0] = (o * _silu(z_ref[0].astype(F32))).astype(o_ref.dtype)


def _ctx_attn_call(p_ctx, d_a):
    b, lc, _ = p_ctx.shape
    npair = d_a // LANES
    return pl.pallas_call(
        _ctx_attn_kernel,
        grid=(b, npair),
        in_specs=[
            pl.BlockSpec((1, lc, LANES), lambda i, p: (i, 0, p)),
            pl.BlockSpec((1, lc, LANES), lambda i, p: (i, 0, npair + p)),
            pl.BlockSpec((1, lc, LANES), lambda i, p: (i, 0, 2 * npair + p)),
            pl.BlockSpec((1, lc, LANES), lambda i, p: (i, 0, 3 * npair + p)),
        ],
        out_specs=pl.BlockSpec((1, lc, LANES), lambda i, p: (i, 0, p)),
        out_shape=jax.ShapeDtypeStruct((b, lc, d_a), BF16),
        compiler_params=_params("parallel", "parallel"),
        name="ctx_attention",
    )(p_ctx, p_ctx, p_ctx, p_ctx)


def _rope(x, cos, sin_signed):
    halves = []
    for j in range(x.shape[1] // LANES):
        xs = x[:, j * LANES:(j + 1) * LANES]
        lane = lax.broadcasted_iota(jnp.int32, xs.shape, 1)
        first = (lane & 16) == 0
        partner = jnp.where(first, pltpu.roll(xs, LANES - 16, 1), pltpu.roll(xs, 16, 1))
        halves.append(partner)
    partner = jnp.concatenate(halves, axis=1)
    return x * cos + partner * sin_signed


def _gla_direction(q_ref, k_ref, v_ref, lr_ref, cos_ref, sin_ref, w2_ref, b2_ref, s0_ref,
                   o_ref, sfin_ref, s_ref, *, reverse, rope):
    tg = q_ref.shape[1]
    dkh = q_ref.shape[2]
    dvh = v_ref.shape[2]
    dk = dkh // GLA_HEADS
    dv = dvh // GLA_HEADS
    nchunk = tg // GLA_CHUNK
    step = pl.program_id(1)

    @pl.when(step == 0)
    def _():
        s_ref[...] = s0_ref[0]

    lr_hi, lr_lo = _split_hi_lo(lr_ref[0])
    w_hi, w_lo = _split_hi_lo(w2_ref[...])
    logits = _dot(lr_hi, w_hi) + _dot(lr_lo, w_hi) + _dot(lr_hi, w_lo) + b2_ref[...]
    g = (jnp.minimum(logits, 0.0) - jnp.log1p(jnp.exp(-jnp.abs(logits)))) * (1.0 / GATE_TAU)

    tcs = min(tg, GLA_CUMSUM_ROWS)
    r = lax.broadcasted_iota(jnp.int32, (tcs, tcs), 0)
    c = lax.broadcasted_iota(jnp.int32, (tcs, tcs), 1)
    same = _shr(r, GLA_CHUNK) == _shr(c, GLA_CHUNK)
    upto = (c >= r) if reverse else (c <= r)
    t_inc = jnp.where(same & upto, 1.0, 0.0).astype(BF16)
    g_hi, g_lo = _split_hi_lo(g)
    b_inc = jnp.concatenate(
        [_dot(t_inc, g_hi[s0:s0 + tcs]) + _dot(t_inc, g_lo[s0:s0 + tcs]) for s0 in range(0, tg, tcs)], axis=0)

    chunks = [slice(ci * GLA_CHUNK, (ci + 1) * GLA_CHUNK) for ci in range(nchunk)]
    end_row = 0 if reverse else GLA_CHUNK - 1
    b_end = [b_inc[sl][end_row:end_row + 1, :] for sl in chunks]
    b_rest = jnp.concatenate([be - b_inc[sl] for be, sl in zip(b_end, chunks)], axis=0)

    q = q_ref[0]
    k = k_ref[0]
    if rope:
        q = _rope(q, cos_ref[...], sin_ref[...])
        k = _rope(k, cos_ref[...], sin_ref[...])
    q = q * (dk ** -0.5)
    qt = (q * jnp.exp(b_inc)).astype(BF16)
    kt = (k * jnp.exp(-b_inc)).astype(BF16)
    kd = (k * jnp.exp(b_rest)).astype(BF16)
    vb = v_ref[0].astype(BF16)
    yield

    lane_head = _shr(lax.broadcasted_iota(jnp.int32, (GLA_CHUNK, dkh), 1), dk)
    hc = GLA_HEADS * GLA_CHUNK
    rk = _shr(lax.broadcasted_iota(jnp.int32, (hc, dkh), 0), GLA_CHUNK)
    ck = _shr(lax.broadcasted_iota(jnp.int32, (hc, dkh), 1), dk)
    mask_k = rk == ck
    rv = _shr(lax.broadcasted_iota(jnp.int32, (hc, dvh), 0), GLA_CHUNK)
    cv = _shr(lax.broadcasted_iota(jnp.int32, (hc, dvh), 1), dv)
    mask_v = rv == cv
    ti = lax.broadcasted_iota(jnp.int32, (GLA_CHUNK, dkh), 0)
    tj = lax.broadcasted_iota(jnp.int32, (GLA_CHUNK, dkh), 1) & (GLA_CHUNK - 1)
    causal = (tj >= ti) if reverse else (tj <= ti)
    zero = jnp.zeros((), BF16)

    def head_stack(x):
        return jnp.concatenate([jnp.where(lane_head == h, x, zero) for h in range(GLA_HEADS)], axis=0)

    o_intra, kv, decay, q_stack = [], [], [], []
    for sl, be in zip(chunks, b_end):
        k_blk = jnp.where(mask_k, jnp.concatenate([kt[sl]] * GLA_HEADS, axis=0), zero)
        att = jnp.where(causal, _dot_nt(qt[sl], k_blk), 0.0)
        v_blk = jnp.where(mask_v, jnp.concatenate([vb[sl]] * GLA_HEADS, axis=0), zero)
        o_intra.append(_dot(att.astype(BF16), v_blk))
        v_rows = jnp.concatenate([vb[sl][:, h * dv:(h + 1) * dv] for h in range(GLA_HEADS)], axis=0)
        kv.append(_dot_tn(v_rows, head_stack(kd[sl])))
        decay.append(jnp.exp(be))
        q_stack.append(head_stack(qt[sl]))
    yield

    order = list(range(nchunk - 1, -1, -1)) if reverse else list(range(nchunk))
    state = s_ref[...]
    states = {}
    for ci in order:
        states[ci] = state
        state = decay[ci] * state + kv[ci]
    s_ref[...] = state

    for ci in range(nchunk):
        inter = _dot_nt(q_stack[ci], states[ci].astype(BF16))
        inter = jnp.concatenate([inter[h * GLA_CHUNK:(h + 1) * GLA_CHUNK] for h in range(GLA_HEADS)], axis=1)
        o_ref[0, chunks[ci], :] = o_intra[ci] + inter

    @pl.when(step == pl.num_programs(1) - 1)
    def _():
        sfin_ref[0] = state


def _gla_kernel(*refs, rope):
    n_in = len(refs) // 2 - 3
    ins_f, ins_b = refs[:n_in], refs[n_in:2 * n_in]
    o_f, sfin_f, o_b, sfin_b, s_f, s_b = refs[2 * n_in:]
    fwd = _gla_direction(*ins_f, o_f, sfin_f, s_f, reverse=False, rope=rope)
    bwd = _gla_direction(*ins_b, o_b, sfin_b, s_b, reverse=True, rope=rope)
    for _ in itertools.zip_longest(fwd, bwd):
        pass


def _gla_call(p_b, cos, sin_signed, w2p, b2, s0, *, rope, dkh, dvh, name):
    b, t, _ = p_b.shape
    tg = min(GLA_TG, t)
    ng = t // tg
    lr_blk = (2 * dkh + 2 * dvh) // LANES
    dv = dvh // GLA_HEADS
    assert dv % LANES == 0

    def in_specs(blk):
        return [
            pl.BlockSpec((1, tg, dkh), lambda n, i: (n, blk(i), 0)),
            pl.BlockSpec((1, tg, dkh), lambda n, i: (n, blk(i), 1)),
            pl.BlockSpec((1, tg, dvh), lambda n, i: (n, blk(i), 2 * dkh // dvh)),
            pl.BlockSpec((1, tg, LANES), lambda n, i: (n, blk(i), lr_blk)),
            pl.BlockSpec((tg, dkh), lambda n, i: (blk(i), 0)),
            pl.BlockSpec((tg, dkh), lambda n, i: (blk(i), 0)),
            pl.BlockSpec((LANES, dkh), lambda n, i: (0, 0)),
            pl.BlockSpec((1, dkh), lambda n, i: (0, 0)),
            pl.BlockSpec((1, dv, dkh), lambda n, i: (n, 0, 0)),
        ]

    def out_specs(blk):
        return [
            pl.BlockSpec((1, tg, dvh), lambda n, i: (n, blk(i), 0)),
            pl.BlockSpec((1, dv, dkh), lambda n, i: (n, 0, 0)),
        ]

    fwd_blk = lambda i: i
    bwd_blk = lambda i: ng - 1 - i
    out_shape = [jax.ShapeDtypeStruct((b, t, dvh), F32), jax.ShapeDtypeStruct((b, dv, dkh), F32)]
    args = lambda d: [p_b, p_b, p_b, p_b, cos, sin_signed, w2p[d], b2[d], s0[d]]
    return pl.pallas_call(
        functools.partial(_gla_kernel, rope=rope),
        grid=(b, ng),
        in_specs=in_specs(fwd_blk) + in_specs(bwd_blk),
        out_specs=out_specs(fwd_blk) + out_specs(bwd_blk),
        out_shape=out_shape + out_shape,
        scratch_shapes=[pltpu.VMEM((dv, dkh), F32), pltpu.VMEM((dv, dkh), F32)],
        compiler_params=_params("parallel", "arbitrary"),
        name=name,
    )(*args(0), *args(1))


def _rope_tables(l, dk):
    quarter = dk // 4
    pos = jnp.arange(l)
    rows_pos = (pos // GRID_W).astype(F32)
    cols_pos = (pos % GRID_W).astype(F32)
    inv = ROPE_BASE ** (-jnp.arange(0, 2 * quarter, 2, dtype=F32) / (2 * quarter))
    ang_r = rows_pos[:, None] * inv[None, :]
    ang_c = cols_pos[:, None] * inv[None, :]
    cos = jnp.concatenate([jnp.cos(ang_r)] * 2 + [jnp.cos(ang_c)] * 2, axis=1)
    sin = jnp.concatenate([-jnp.sin(ang_r), jnp.sin(ang_r), -jnp.sin(ang_c), jnp.sin(ang_c)], axis=1)
    return jnp.tile(cos, (1, GLA_HEADS)), jnp.tile(sin, (1, GLA_HEADS))


def _conv_kernel(a_ref, g_ref, z_ref, ap_ref, gp_ref, an_ref, gn_ref, w_ref, cb_ref, lg_ref, lb_ref,
                 o_ref, u_ref, us_ref):
    t = a_ref.shape[1]
    nshift = u_ref.shape[0] - SUBLANES
    step = pl.program_id(1)
    nstep = pl.num_programs(1)

    def glu(a, g):
        return a.astype(F32) * jax.nn.sigmoid(g.astype(F32))

    prev_ok = (step > 0).astype(F32)
    next_ok = (step < nstep - 1).astype(F32)
    u_ref[0:CONV_HALO, :] = glu(ap_ref[0], gp_ref[0]) * prev_ok
    u_ref[CONV_HALO:CONV_HALO + t, :] = glu(a_ref[0], g_ref[0])
    u_ref[CONV_HALO + t:2 * CONV_HALO + t, :] = glu(an_ref[0], gn_ref[0]) * next_ok

    for b in range(1, SUBLANES):
        us_ref[b - 1] = u_ref[b:b + nshift, :]

    off = CONV_HALO - CONV_K // 2
    for rc in range(t // CONV_RC):
        base = rc * CONV_RC
        acc = None
        for j in range(CONV_K):
            q8, b = divmod(off + j, SUBLANES)
            lo = base + q8 * SUBLANES
            src = u_ref[lo:lo + CONV_RC, :] if b == 0 else us_ref[b - 1, lo:lo + CONV_RC, :]
            term = src * w_ref[j:j + 1, :]
            acc = term if acc is None else acc + term
        acc = acc + cb_ref[...]
        mu = jnp.mean(acc, axis=-1, keepdims=True)
        xc = acc - mu
        var = jnp.mean(xc * xc, axis=-1, keepdims=True)
        y = xc * lax.rsqrt(var + LN_EPS) * lg_ref[...] + lb_ref[...]
        z = z_ref[0, base:base + CONV_RC, :].astype(F32)
        o_ref[0, base:base + CONV_RC, :] = (_silu(y) * _silu(z)).astype(o_ref.dtype)


def _conv_call(p, conv_w, conv_b, ln_g, ln_b, col0, d_c, name):
    b, t, _ = p.shape
    tt = min(CONV_T, t)
    nt = t // tt
    cb = col0 // d_c
    hb = tt // CONV_HALO
    nhalo = t // CONV_HALO
    w = jnp.zeros((CONV_K + 1, d_c), F32).at[:CONV_K].set(conv_w)

    def prev(i):
        return jnp.maximum(i * hb - 1, 0)

    def nxt(i):
        return jnp.minimum((i + 1) * hb, nhalo - 1)

    row = lambda v: v.reshape(1, d_c)
    return pl.pallas_call(
        _conv_kernel,
        grid=(b, nt),
        in_specs=[
            pl.BlockSpec((1, tt, d_c), lambda n, i: (n, i, cb)),
            pl.BlockSpec((1, tt, d_c), lambda n, i: (n, i, cb + 1)),
            pl.BlockSpec((1, tt, d_c), lambda n, i: (n, i, cb + 2)),
            pl.BlockSpec((1, CONV_HALO, d_c), lambda n, i: (n, prev(i), cb)),
            pl.BlockSpec((1, CONV_HALO, d_c), lambda n, i: (n, prev(i), cb + 1)),
            pl.BlockSpec((1, CONV_HALO, d_c), lambda n, i: (n, nxt(i), cb)),
            pl.BlockSpec((1, CONV_HALO, d_c), lambda n, i: (n, nxt(i), cb + 1)),
            pl.BlockSpec((CONV_K + 1, d_c), lambda n, i: (0, 0)),
            pl.BlockSpec((1, d_c), lambda n, i: (0, 0)),
            pl.BlockSpec((1, d_c), lambda n, i: (0, 0)),
            pl.BlockSpec((1, d_c), lambda n, i: (0, 0)),
        ],
        out_specs=pl.BlockSpec((1, tt, d_c), lambda n, i: (n, i, 0)),
        out_shape=jax.ShapeDtypeStruct((b, t, d_c), BF16),
        scratch_shapes=[pltpu.VMEM((tt + 2 * CONV_HALO, d_c), F32),
                        pltpu.VMEM((SUBLANES - 1, tt + 2 * CONV_HALO - SUBLANES, d_c), F32)],
        compiler_params=_params("parallel", "arbitrary"),
        name=name,
    )(p, p, p, p, p, p, p, w, row(conv_b), row(ln_g), row(ln_b))


def _out_kernel(oa_ref, of_ref, ob_ref, zb_ref, oc_ref, x_ref, gate_ref, gn_ref,
                w_ref, lg_ref, lb_ref, *rest, alpha, emit_h):
    if emit_h:
        mod_ref, o_ref, h_ref, y0_ref, y1_ref, lhs_ref = rest
    else:
        o_ref, y0_ref, y1_ref, lhs_ref = rest
    tm = x_ref.shape[0]
    d_a = oa_ref.shape[1]
    dvh = of_ref.shape[1]
    dv = dvh // GLA_HEADS
    step = pl.program_id(0)

    @pl.when(step == 0)
    def _():
        y1_ref[...] = jnp.zeros_like(y1_ref)

    def run(y_prev, y_cur):
        gate = gate_ref[0]
        if emit_h:
            shift = mod_ref[0, 0:1, :]
            scale1 = 1.0 + mod_ref[0, 1:2, :]
        for c in range(tm // OUT_LN_ROWS):
            sl = slice(c * OUT_LN_ROWS, (c + 1) * OUT_LN_ROWS)
            rv = alpha * x_ref[sl, :] + gate * y_prev[sl, :]
            xn = _layer_norm_rows(rv) * lg_ref[...] + lb_ref[...]
            o_ref[sl, :] = xn
            if emit_h:
                h_ref[sl, :] = (_layer_norm_rows(xn) * scale1 + shift).astype(BF16)

        s = of_ref[...] + ob_ref[...]
        parts = []
        for h in range(GLA_HEADS):
            sh = s[:, h * dv:(h + 1) * dv]
            ms = jnp.mean(sh * sh, axis=-1, keepdims=True)
            parts.append(sh * lax.rsqrt(ms + RMS_EPS))
        out_b = jnp.concatenate(parts, axis=1) * gn_ref[...] * _silu(zb_ref[...])
        lhs_ref[:, :d_a] = oa_ref[...]
        lhs_ref[:, d_a:d_a + dvh] = out_b.astype(BF16)
        lhs_ref[:, d_a + dvh:] = oc_ref[...]
        y_cur[...] = _dot(lhs_ref[...], w_ref[...])

    @pl.when(step % 2 == 0)
    def _():
        run(y1_ref, y0_ref)

    @pl.when(step % 2 == 1)
    def _():
        run(y0_ref, y1_ref)


def _out_call(oa, o_f, o_b, p_b, oc, x2, gate, gla_norm, w_out, ln_g, ln_b, next_mod, rows_per_gate, zb_col,
              alpha, name):
    m, d = x2.shape
    d_a, dvh, d_c = oa.shape[1], o_f.shape[1], oc.shape[1]
    tm = min(OUT_TM, m)
    per = rows_per_gate // tm
    emit_h = next_mod is not None
    w_bf = w_out.astype(BF16)
    gn = jnp.tile(gla_norm, GLA_HEADS).reshape(1, dvh)
    nt = m // tm
    cur = lambda i: jnp.minimum(i, nt - 1)
    prev = lambda i: jnp.maximum(i - 1, 0)
    full = lambda shape: pl.BlockSpec(shape, lambda i: (0,) * len(shape))
    rows = pl.BlockSpec((tm, d), lambda i: (prev(i), 0))
    in_specs = [
        pl.BlockSpec((tm, d_a), lambda i: (cur(i), 0)),
        pl.BlockSpec((tm, dvh), lambda i: (cur(i), 0)),
        pl.BlockSpec((tm, dvh), lambda i: (cur(i), 0)),
        pl.BlockSpec((tm, dvh), lambda i: (cur(i), zb_col // dvh)),
        pl.BlockSpec((tm, d_c), lambda i: (cur(i), 0)),
        rows,
        pl.BlockSpec((1, 1, d), lambda i: (prev(i) // per, 0, 0)),
        full((1, dvh)),
        full((d_a + dvh + d_c, d)),
        full((1, d)),
        full((1, d)),
    ]
    args = [oa, o_f, o_b, p_b, oc, x2, gate, gn, w_bf, ln_g.reshape(1, d), ln_b.reshape(1, d)]
    out_specs = rows
    out_shape = jax.ShapeDtypeStruct((m, d), F32)
    if emit_h:
        in_specs.append(pl.BlockSpec((1, 2, d), lambda i: (prev(i) // per, 0, 0)))
        args.append(next_mod)
        out_specs = [rows, rows]
        out_shape = [out_shape, jax.ShapeDtypeStruct((m, d), BF16)]
    return pl.pallas_call(
        functools.partial(_out_kernel, alpha=alpha, emit_h=emit_h),
        grid=(nt + 1,),
        in_specs=in_specs,
        out_specs=out_specs,
        out_shape=out_shape,
        scratch_shapes=[pltpu.VMEM((tm, d), F32), pltpu.VMEM((tm, d), F32),
                        pltpu.VMEM((tm, d_a + dvh + d_c), BF16)],
        compiler_params=_params("arbitrary"),
        name=name,
    )(*args)


def kernel(x, c, ctx, c_ctx, w_ada, b_ada, w_in, rpb, gla_w2, gla_b, gla_norm, conv_w, conv_b,
           conv_ln_g, conv_ln_b, w_out, post_ln_g, post_ln_b):
    b, l, d = x.shape
    lc = ctx.shape[1]
    depth = w_ada.shape[0]
    heads_a = rpb.shape[1]
    d_a = heads_a * NA_HEAD_DIM
    dkh = gla_w2.shape[-1]
    dvh = gla_norm.shape[-1] * GLA_HEADS
    d_c = conv_w.shape[-1]
    n_a = 4 * d_a
    n_b = 2 * dkh + 2 * dvh + 2 * GATE_RANK
    n_c = 3 * d_c
    alpha = (2 * depth) ** 0.25
    assert w_in.shape[-1] == n_a + n_b + n_c and d_a + dvh + d_c == w_out.shape[1]
    tn_c = n_c // 2
    tn_b = PROJ_TN
    n_bp = -(-n_b // tn_b) * tn_b
    assert n_a + n_bp <= w_in.shape[-1] and tn_c % LANES == 0

    cond = jnp.zeros((8, d), F32).at[:b].set(c).at[b].set(c_ctx)
    mod = _ada_call(cond, w_ada, b_ada)

    na_bias = _na_bias_table(rpb, l // GRID_W)
    w_in_t = jnp.swapaxes(w_in, 1, 2)

    cos, sin_signed = _rope_tables(l, dkh // GLA_HEADS)
    ones_c = jnp.ones((lc, dkh), F32)
    zeros_c = jnp.zeros((lc, dkh), F32)

    def layer_mods(layer):
        shift, scale, gate = jnp.split(mod[layer], 3, axis=-1)
        mod_lat = jnp.stack([shift[:b], scale[:b]], axis=1)
        mod_ctx = jnp.stack([shift[b:b + 1], scale[b:b + 1]], axis=1)
        return mod_lat, mod_ctx, gate[:b].reshape(b, 1, d), gate[b:b + 1].reshape(1, 1, d)

    x2 = x.reshape(b * l, d)
    cx2 = ctx.reshape(b * lc, d)
    mod_lat, mod_ctx, gate_lat, gate_ctx = layer_mods(0)
    h2 = _ln_mod_call(x2, mod_lat, l, "ln_mod")
    hc2 = _ln_mod_call(cx2, mod_ctx, b * lc, "ln_mod_ctx")
    for layer in range(depth):
        last = layer == depth - 1
        if not last:
            next_lat, next_ctx, next_gate_lat, next_gate_ctx = layer_mods(layer + 1)
        else:
            next_lat = next_ctx = None

        def project(hh, rows_n, tag):
            pa = _proj_call(hh, w_in_t, layer, 0, n_a, PROJ_TN, BF16, "proj_a" + tag)
            pb = _proj_call(hh, w_in_t, layer, n_a, n_bp, tn_b, F32, "proj_b" + tag)
            pc = _proj_call(hh, w_in_t, layer, n_a + n_b, n_c, tn_c, BF16, "proj_c" + tag)
            return pa.reshape(b, rows_n, n_a), pb.reshape(b, rows_n, n_bp), pc.reshape(b, rows_n, n_c)

        p_a, p_b, p_c = project(h2, l, "")
        pc_a, pc_b, pc_c = project(hc2, lc, "_ctx")

        out_a = _na_call(p_a, pc_a, na_bias[layer], d_a)

        w2p = [jnp.zeros((LANES, dkh), F32).at[i * GATE_RANK:(i + 1) * GATE_RANK].set(gla_w2[layer, i])
               for i in range(2)]
        b2 = [gla_b[layer, i].reshape(1, dkh) for i in range(2)]
        s0 = jnp.zeros((b, dvh // GLA_HEADS, dkh), F32)
        gla = functools.partial(_gla_call, dkh=dkh, dvh=dvh)
        oc_f, s_f, oc_b, s_b = gla(pc_b, ones_c, zeros_c, w2p, b2, (s0, s0), rope=False, name="gla_ctx")
        o_f, _, o_b, _ = gla(p_b, cos, sin_signed, w2p, b2, (s_f, s_b), rope=True, name="gla")

        out_c = _conv_call(p_c, conv_w[layer], conv_b[layer], conv_ln_g[layer], conv_ln_b[layer],
                           0, d_c, "conv")

        zb_col = 2 * dkh + dvh
        res = _out_call(out_a.reshape(b * l, d_a), o_f.reshape(b * l, dvh), o_b.reshape(b * l, dvh),
                        p_b.reshape(b * l, n_bp), out_c.reshape(b * l, d_c), x2, gate_lat,
                        gla_norm[layer], w_out[layer], post_ln_g[layer], post_ln_b[layer], next_lat,
                        l, zb_col, alpha, "out_proj")
        if last:
            x2 = res
        else:
            out_a_c = _ctx_attn_call(pc_a, d_a)
            out_c_c = _conv_call(pc_c, conv_w[layer], conv_b[layer], conv_ln_g[layer], conv_ln_b[layer],
                                 0, d_c, "conv_ctx")
            cx2, hc2 = _out_call(out_a_c.reshape(b * lc, d_a), oc_f.reshape(b * lc, dvh),
                                 oc_b.reshape(b * lc, dvh), pc_b.reshape(b * lc, n_bp),
                                 out_c_c.reshape(b * lc, d_c), cx2, gate_ctx, gla_norm[layer], w_out[layer],
                                 post_ln_g[layer], post_ln_b[layer], next_ctx, b * lc, zb_col, alpha,
                                 "out_proj_ctx")
            x2, h2 = res
            gate_lat, gate_ctx = next_gate_lat, next_gate_ctx
    return x2.reshape(b, l, d)
```

```python
import functools
import math

import numpy as np
import jax
import jax.numpy as jnp
from jax import lax
from jax.experimental import pallas as pl
from jax.experimental.pallas import tpu as pltpu

F32 = jnp.float32
BF16 = jnp.bfloat16

GRID_W = 64
NA_HEAD_DIM = 64
NA_KH_MAX = 8
NA_KW = 16
GLA_HEADS = 4
GATE_RANK = 16
GATE_TAU = 16.0
GLA_CHUNK = 64
CONV_K = 31
ROPE_BASE = 10000.0
LN_EPS = 1e-5
RMS_EPS = 1e-6

LANES = 128
SUBLANES = 8
VMEM_LIMIT_BYTES = 56 * 1024 * 1024

NEG_BIG = -1e30
LOG2E = 1.4426950408889634

PROJ_TM = 1024
PROJ_TN = 1024
PROJ_WT_ROWS = 256
OUT_TM = 512
LN_TM = 1024
LN_ROWS = 32
OUT_LN_ROWS = 16
LN_UNROLL = 4
ATT_TQ_ROWS = 16
ATT_ITEM_ROWS = 2
ATT_SKEW = 1
GLA_TG = 1024
GLA_CUMSUM_ROWS = 256
CONV_T = 1024
CONV_HALO = 16
CONV_RC = 64


def _silu(x):
    return x * jax.nn.sigmoid(x)


def _dot(a, b):
    return jnp.dot(a, b, preferred_element_type=F32)


def _dot_nt(a, b):
    return lax.dot_general(a, b, (((1,), (1,)), ((), ())), preferred_element_type=F32)


def _dot_tn(a, b):
    return lax.dot_general(a, b, (((0,), (0,)), ((), ())), preferred_element_type=F32)


def _shr(x, pow2):
    shift = pow2.bit_length() - 1
    assert 1 << shift == pow2
    return jnp.right_shift(x, shift)


def _split_hi_lo(x):
    hi = x.astype(BF16)
    lo = (x - hi.astype(F32)).astype(BF16)
    return hi, lo


def _params(*sem):
    return pltpu.CompilerParams(dimension_semantics=sem, vmem_limit_bytes=VMEM_LIMIT_BYTES)


def _ada_kernel(c_ref, w_ref, b_ref, o_ref):
    s = _silu(c_ref[...]).astype(BF16)
    o_ref[0] = _dot(s, w_ref[0].astype(BF16)) + b_ref[0]


def _ada_call(cond, w_ada, b_ada, tn=1536):
    depth, d, n = w_ada.shape
    rows = cond.shape[0]
    return pl.pallas_call(
        _ada_kernel,
        grid=(depth, n // tn),
        in_specs=[
            pl.BlockSpec((rows, d), lambda l, j: (0, 0)),
            pl.BlockSpec((1, d, tn), lambda l, j: (l, 0, j)),
            pl.BlockSpec((1, 1, tn), lambda l, j: (l, 0, j)),
        ],
        out_specs=pl.BlockSpec((1, rows, tn), lambda l, j: (l, 0, j)),
        out_shape=jax.ShapeDtypeStruct((depth, rows, n), F32),
        compiler_params=_params("parallel", "parallel"),
        name="ada_mod",
    )(cond, w_ada, b_ada.reshape(depth, 1, n))


def _layer_norm_rows(v):
    mu = jnp.mean(v, axis=-1, keepdims=True)
    vc = v - mu
    var = jnp.mean(vc * vc, axis=-1, keepdims=True)
    return vc * lax.rsqrt(var + LN_EPS)


def _ln_mod_kernel(x_ref, mod_ref, h_ref):
    shift = mod_ref[0, 0:1, :]
    scale1 = 1.0 + mod_ref[0, 1:2, :]

    def body(i, carry):
        r0 = pl.multiple_of(i * LN_ROWS, LN_ROWS)
        hn = _layer_norm_rows(x_ref[pl.ds(r0, LN_ROWS), :])
        h_ref[pl.ds(r0, LN_ROWS), :] = (hn * scale1 + shift).astype(BF16)
        return carry

    lax.fori_loop(0, x_ref.shape[0] // LN_ROWS, body, 0, unroll=LN_UNROLL)


def _ln_mod_call(x2, mod, rows_per_mod, name):
    m, d = x2.shape
    tm = min(LN_TM, m)
    per = rows_per_mod // tm
    return pl.pallas_call(
        _ln_mod_kernel,
        grid=(m // tm,),
        in_specs=[
            pl.BlockSpec((tm, d), lambda i: (i, 0)),
            pl.BlockSpec((1, 2, d), lambda i: (i // per, 0, 0)),
        ],
        out_specs=pl.BlockSpec((tm, d), lambda i: (i, 0)),
        out_shape=jax.ShapeDtypeStruct((m, d), BF16),
        compiler_params=_params("parallel"),
        name=name,
    )(x2, mod)


def _matmul_kernel(h_ref, *refs, shift):
    *w_refs, o_ref, wb_ref = refs

    @pl.when(pl.program_id(1) == 0)
    def _():
        tn = wb_ref.shape[1]
        step_rows = math.gcd(tn, PROJ_WT_ROWS)
        for c0 in range(0, tn, step_rows):
            lo, hi = c0 + shift, c0 + shift + step_rows
            pieces = []
            if lo < tn:
                pieces.append(w_refs[0][0, lo:min(hi, tn), :])
            if hi > tn:
                pieces.append(w_refs[1][0, max(lo, tn) - tn:hi - tn, :])
            rows_f32 = pieces[0] if len(pieces) == 1 else jnp.concatenate(pieces, axis=0)
            wb_ref[:, c0:c0 + step_rows] = jnp.transpose(rows_f32).astype(BF16)

    o_ref[...] = _dot(h_ref[...], wb_ref[...]).astype(o_ref.dtype)


def _proj_call(h2, w_t, layer, col0, ncols, tn, out_dtype, name):
    m, d = h2.shape
    tm = min(PROJ_TM, m)
    blk0, shift = divmod(col0, tn)
    nblk = ncols // tn
    assert ncols % tn == 0 and col0 + ncols <= w_t.shape[1] and shift % SUBLANES == 0
    w_spec = lambda extra: pl.BlockSpec((1, tn, d), lambda j, i: (layer, blk0 + extra + j, 0))
    w_specs = [w_spec(0), w_spec(1)] if shift else [w_spec(0)]
    return pl.pallas_call(
        functools.partial(_matmul_kernel, shift=shift),
        grid=(nblk, m // tm),
        in_specs=[pl.BlockSpec((tm, d), lambda j, i: (i, 0))] + w_specs,
        out_specs=pl.BlockSpec((tm, tn), lambda j, i: (i, j)),
        out_shape=jax.ShapeDtypeStruct((m, ncols), out_dtype),
        scratch_shapes=[pltpu.VMEM((d, tn), BF16)],
        compiler_params=_params("arbitrary", "arbitrary"),
        name=name,
    )(h2, *([w_t] * len(w_specs)))


def _softmax_pv(q, lane_lo, pieces):
    outs = []
    for h in range(2):
        keep = lane_lo if h == 0 else jnp.logical_not(lane_lo)
        qh = jnp.where(keep, q, jnp.zeros_like(q)) * jnp.asarray(NA_HEAD_DIM ** -0.5, BF16)
        scores = []
        for k, _, bias in pieces:
            s = _dot_nt(qh, k)
            if bias is not None:
                s = s + bias[h]
            scores.append(s)
        m = scores[0].max(axis=-1, keepdims=True)
        for s in scores[1:]:
            m = jnp.maximum(m, s.max(axis=-1, keepdims=True))
        den = None
        acc = None
        for s, (_, v, _) in zip(scores, pieces):
            p = jnp.exp(s - m)
            ps = p.sum(axis=-1, keepdims=True)
            pv = _dot(p.astype(BF16), v)
            den = ps if den is None else den + ps
            acc = pv if acc is None else acc + pv
        outs.append(acc / den)
    return jnp.where(lane_lo, outs[0], outs[1])


def _na_kernel(q_ref, k_ref, v_ref, z_ref, kc_ref, vc_ref, comp_ref, o_ref, vaug_ref, bias_ref, *, rows):
    tq = q_ref.shape[1]
    kh = min(NA_KH_MAX, rows)
    nkeys = kh * GRID_W
    span = vaug_ref.shape[1]
    qi = pl.program_id(2)

    @pl.when((pl.program_id(1) == 0) & (qi == 0))
    def _():
        for h in range(2):
            comp = comp_ref[0, h]
            for d0 in range(bias_ref.shape[0]):
                bias_ref[d0, h] = comp[:, d0 * GRID_W:d0 * GRID_W + nkeys]

    lane_lo = lax.broadcasted_iota(jnp.int32, (tq, LANES), 1) < NA_HEAD_DIM
    q = q_ref[0]
    kc = kc_ref[0]
    vc = vc_ref[0]
    span_row = jnp.clip(qi * ATT_TQ_ROWS - kh // 2, 0, rows - span // GRID_W)
    starts, offs, d0s = [], [], []
    for i in range(ATT_TQ_ROWS):
        r = qi * ATT_TQ_ROWS + i
        rs = jnp.clip(r - kh // 2, 0, rows - kh)
        d0s.append(rs - r + (NA_KH_MAX - 1))
        starts.append(pl.multiple_of(rs * GRID_W, GRID_W))
        offs.append(pl.multiple_of((rs - span_row) * GRID_W, GRID_W))
    v_span = v_ref[0, pl.ds(pl.multiple_of(span_row * GRID_W, GRID_W), span), :]
    one = jnp.ones((), BF16)

    def head_lanes(shape, h):
        lo = lax.broadcasted_iota(jnp.int32, shape, 1) < NA_HEAD_DIM
        return lo if h == 0 else jnp.logical_not(lo)

    qhs, vcs = [], []
    for h in range(2):
        keep = lane_lo if h == 0 else jnp.logical_not(lane_lo)
        qf = jnp.where(keep, q, jnp.zeros_like(q)).astype(F32) * (NA_HEAD_DIM ** -0.5 * LOG2E)
        qhs.append(qf.astype(BF16))
        vaug_ref[h] = jnp.where(head_lanes(v_span.shape, h), v_span, one)
        vcs.append(jnp.where(head_lanes(vc.shape, h), vc, one))

    def stage_scores(h, rows_i):
        qs = qhs[h][rows_i[0] * GRID_W:(rows_i[-1] + 1) * GRID_W]
        s_loc = jnp.concatenate(
            [_dot_nt(qhs[h][i * GRID_W:(i + 1) * GRID_W], k_ref[0, pl.ds(starts[i], nkeys), :])
             + bias_ref[d0s[i], h] for i in rows_i], axis=0)
        return h, s_loc, _dot_nt(qs, kc)

    def stage_softmax(h, s_loc, s_ctx):
        m = jnp.maximum(s_loc.max(axis=-1, keepdims=True), s_ctx.max(axis=-1, keepdims=True))
        return h, jnp.exp2(s_loc - m).astype(BF16), jnp.exp2(s_ctx - m).astype(BF16)

    def stage_values(h, p_loc, p_ctx, rows_i):
        acc = jnp.concatenate(
            [_dot(p_loc[j * GRID_W:(j + 1) * GRID_W], vaug_ref[h, pl.ds(offs[i], nkeys), :])
             for j, i in enumerate(rows_i)], axis=0)
        acc = acc + _dot(p_ctx, vcs[h])
        return acc / pltpu.roll(acc, NA_HEAD_DIM, 1)

    groups = [list(range(g, g + ATT_ITEM_ROWS)) for g in range(0, ATT_TQ_ROWS, ATT_ITEM_ROWS)]
    items = [(h, g) for h in range(2) for g in groups]
    scores, probs, outs = {}, {}, {}
    sk = ATT_SKEW
    for t in range(len(items) + 2 * sk):
        if t < len(items):
            scores[t] = stage_scores(*items[t])
        if 0 <= t - sk < len(items):
            probs[t - sk] = stage_softmax(*scores.pop(t - sk))
        if 0 <= t - 2 * sk < len(items):
            outs[t - 2 * sk] = stage_values(*probs.pop(t - 2 * sk), items[t - 2 * sk][1])
    per_head = [jnp.concatenate([outs[t] for t, (hh, _) in enumerate(items) if hh == h], axis=0)
                for h in range(2)]
    o = jnp.where(lane_lo, per_head[0], per_head[1])
    o_ref[0] = (o * _silu(z_ref[0].astype(F32))).astype(o_ref.dtype)


def _na_bias_table(rpb, rows):
    depth, heads = rpb.shape[:2]
    kh = min(NA_KH_MAX, rows)
    nd = 2 * NA_KH_MAX - kh
    for r in range(rows):
        assert 0 <= int(np.clip(r - kh // 2, 0, rows - kh)) - r + (NA_KH_MAX - 1) < nd

    c = np.arange(GRID_W)[:, None]
    kc = np.arange(GRID_W)[None, :]
    cs = np.clip(c - NA_KW // 2, 0, GRID_W - NA_KW)
    valid_c = (kc >= cs) & (kc < cs + NA_KW)
    dcol = kc - c + (NA_KW - 1)
    sel_col = ((dcol[None] == np.arange(2 * NA_KW - 1)[:, None, None]) & valid_c[None]).astype(np.float32)
    comp = jnp.einsum("lhde,eck->lhcdk", rpb, sel_col, precision=lax.Precision.HIGHEST)
    comp = jnp.where(valid_c[:, None, :], comp * LOG2E, NEG_BIG)
    ncomp = (2 * NA_KH_MAX - 1) * GRID_W
    comp = comp.reshape(depth, heads // 2, 2, GRID_W, ncomp)
    return jnp.pad(comp, ((0, 0),) * 4 + ((0, -ncomp % LANES),), constant_values=NEG_BIG)


def _na_call(p_lat, p_ctx, bias, d_a):
    b, l, _ = p_lat.shape
    lc = p_ctx.shape[1]
    rows = l // GRID_W
    assert rows % ATT_TQ_ROWS == 0
    tq = ATT_TQ_ROWS * GRID_W
    nq = l // tq
    npair = d_a // LANES
    kh = min(NA_KH_MAX, rows)
    nd = 2 * NA_KH_MAX - kh
    ncomp = bias.shape[-1]
    span_rows = min(ATT_TQ_ROWS + kh - 1, rows)

    return pl.pallas_call(
        functools.partial(_na_kernel, rows=rows),
        grid=(npair, b, nq),
        in_specs=[
            pl.BlockSpec((1, tq, LANES), lambda p, i, q: (i, q, p)),
            pl.BlockSpec((1, l, LANES), lambda p, i, q: (i, 0, npair + p)),
            pl.BlockSpec((1, l, LANES), lambda p, i, q: (i, 0, 2 * npair + p)),
            pl.BlockSpec((1, tq, LANES), lambda p, i, q: (i, q, 3 * npair + p)),
            pl.BlockSpec((1, lc, LANES), lambda p, i, q: (i, 0, npair + p)),
            pl.BlockSpec((1, lc, LANES), lambda p, i, q: (i, 0, 2 * npair + p)),
            pl.BlockSpec((1, 2, GRID_W, ncomp), lambda p, i, q: (p, 0, 0, 0)),
        ],
        out_specs=pl.BlockSpec((1, tq, LANES), lambda p, i, q: (i, q, p)),
        out_shape=jax.ShapeDtypeStruct((b, l, d_a), BF16),
        scratch_shapes=[pltpu.VMEM((2, span_rows * GRID_W, LANES), BF16),
                        pltpu.VMEM((nd, 2, GRID_W, kh * GRID_W), F32)],
        compiler_params=_params("arbitrary", "arbitrary", "arbitrary"),
        name="na_attention",
    )(p_lat, p_lat, p_lat, p_lat, p_ctx, p_ctx, bias)


def _ctx_attn_kernel(q_ref, k_ref, v_ref, z_ref, o_ref):
    tq = q_ref.shape[1]
    lane_lo = lax.broadcasted_iota(jnp.int32, (tq, LANES), 1) < NA_HEAD_DIM
    o = _softmax_pv(q_ref[0], lane_lo, [(k_ref[0], v_ref[0], None)])
    o_ref[0] = (o * _silu(z_ref[0].astype(F32))).astype(o_ref.dtype)


def _ctx_attn_call(p_ctx, d_a):
    b, lc, _ = p_ctx.shape
    npair = d_a // LANES
    return pl.pallas_call(
        _ctx_attn_kernel,
        grid=(b, npair),
        in_specs=[
            pl.BlockSpec((1, lc, LANES), lambda i, p: (i, 0, p)),
            pl.BlockSpec((1, lc, LANES), lambda i, p: (i, 0, npair + p)),
            pl.BlockSpec((1, lc, LANES), lambda i, p: (i, 0, 2 * npair + p)),
            pl.BlockSpec((1, lc, LANES), lambda i, p: (i, 0, 3 * npair + p)),
        ],
        out_specs=pl.BlockSpec((1, lc, LANES), lambda i, p: (i, 0, p)),
        out_shape=jax.ShapeDtypeStruct((b, lc, d_a), BF16),
        compiler_params=_params("parallel", "parallel"),
        name="ctx_attention",
    )(p_ctx, p_ctx, p_ctx, p_ctx)


def _rope(x, cos, sin_signed):
    halves = []
    for j in range(x.shape[1] // LANES):
        xs = x[:, j * LANES:(j + 1) * LANES]
        lane = lax.broadcasted_iota(jnp.int32, xs.shape, 1)
        first = (lane & 16) == 0
        partner = jnp.where(first, pltpu.roll(xs, LANES - 16, 1), pltpu.roll(xs, 16, 1))
        halves.append(partner)
    partner = jnp.concatenate(halves, axis=1)
    return x * cos + partner * sin_signed


def _gla_kernel(q_ref, k_ref, v_ref, lr_ref, cos_ref, sin_ref, w2_ref, b2_ref, s0_ref,
                o_ref, sfin_ref, s_ref, *, reverse, rope):
    tg = q_ref.shape[1]
    dkh = q_ref.shape[2]
    dvh = v_ref.shape[2]
    dk = dkh // GLA_HEADS
    dv = dvh // GLA_HEADS
    nchunk = tg // GLA_CHUNK
    step = pl.program_id(1)

    @pl.when(step == 0)
    def _():
        s_ref[...] = s0_ref[0]

    lr_hi, lr_lo = _split_hi_lo(lr_ref[0])
    w_hi, w_lo = _split_hi_lo(w2_ref[...])
    logits = _dot(lr_hi, w_hi) + _dot(lr_lo, w_hi) + _dot(lr_hi, w_lo) + b2_ref[...]
    g = (jnp.minimum(logits, 0.0) - jnp.log1p(jnp.exp(-jnp.abs(logits)))) * (1.0 / GATE_TAU)

    tcs = min(tg, GLA_CUMSUM_ROWS)
    r = lax.broadcasted_iota(jnp.int32, (tcs, tcs), 0)
    c = lax.broadcasted_iota(jnp.int32, (tcs, tcs), 1)
    same = _shr(r, GLA_CHUNK) == _shr(c, GLA_CHUNK)
    upto = (c >= r) if reverse else (c <= r)
    t_inc = jnp.where(same & upto, 1.0, 0.0).astype(BF16)
    g_hi, g_lo = _split_hi_lo(g)
    b_inc = jnp.concatenate(
        [_dot(t_inc, g_hi[s0:s0 + tcs]) + _dot(t_inc, g_lo[s0:s0 + tcs]) for s0 in range(0, tg, tcs)], axis=0)

    chunks = [slice(ci * GLA_CHUNK, (ci + 1) * GLA_CHUNK) for ci in range(nchunk)]
    end_row = 0 if reverse else GLA_CHUNK - 1
    b_end = [b_inc[sl][end_row:end_row + 1, :] for sl in chunks]
    b_rest = jnp.concatenate([be - b_inc[sl] for be, sl in zip(b_end, chunks)], axis=0)

    q = q_ref[0]
    k = k_ref[0]
    if rope:
        q = _rope(q, cos_ref[...], sin_ref[...])
        k = _rope(k, cos_ref[...], sin_ref[...])
    q = q * (dk ** -0.5)
    qt = (q * jnp.exp(b_inc)).astype(BF16)
    kt = (k * jnp.exp(-b_inc)).astype(BF16)
    kd = (k * jnp.exp(b_rest)).astype(BF16)
    vb = v_ref[0].astype(BF16)

    lane_head = _shr(lax.broadcasted_iota(jnp.int32, (GLA_CHUNK, dkh), 1), dk)
    hc = GLA_HEADS * GLA_CHUNK
    rk = _shr(lax.broadcasted_iota(jnp.int32, (hc, dkh), 0), GLA_CHUNK)
    ck = _shr(lax.broadcasted_iota(jnp.int32, (hc, dkh), 1), dk)
    mask_k = rk == ck
    rv = _shr(lax.broadcasted_iota(jnp.int32, (hc, dvh), 0), GLA_CHUNK)
    cv = _shr(lax.broadcasted_iota(jnp.int32, (hc, dvh), 1), dv)
    mask_v = rv == cv
    ti = lax.broadcasted_iota(jnp.int32, (GLA_CHUNK, dkh), 0)
    tj = lax.broadcasted_iota(jnp.int32, (GLA_CHUNK, dkh), 1) & (GLA_CHUNK - 1)
    causal = (tj >= ti) if reverse else (tj <= ti)
    zero = jnp.zeros((), BF16)

    def head_stack(x):
        return jnp.concatenate([jnp.where(lane_head == h, x, zero) for h in range(GLA_HEADS)], axis=0)

    o_intra, kv, decay, q_stack = [], [], [], []
    for sl, be in zip(chunks, b_end):
        k_blk = jnp.where(mask_k, jnp.concatenate([kt[sl]] * GLA_HEADS, axis=0), zero)
        att = jnp.where(causal, _dot_nt(qt[sl], k_blk), 0.0)
        v_blk = jnp.where(mask_v, jnp.concatenate([vb[sl]] * GLA_HEADS, axis=0), zero)
        o_intra.append(_dot(att.astype(BF16), v_blk))
        v_rows = jnp.concatenate([vb[sl][:, h * dv:(h + 1) * dv] for h in range(GLA_HEADS)], axis=0)
        kv.append(_dot_tn(v_rows, head_stack(kd[sl])))
        decay.append(jnp.exp(be))
        q_stack.append(head_stack(qt[sl]))

    order = list(range(nchunk - 1, -1, -1)) if reverse else list(range(nchunk))
    state = s_ref[...]
    states = {}
    for ci in order:
        states[ci] = state
        state = decay[ci] * state + kv[ci]
    s_ref[...] = state

    for ci in range(nchunk):
        inter = _dot_nt(q_stack[ci], states[ci].astype(BF16))
        inter = jnp.concatenate([inter[h * GLA_CHUNK:(h + 1) * GLA_CHUNK] for h in range(GLA_HEADS)], axis=1)
        o_ref[0, chunks[ci], :] = o_intra[ci] + inter

    @pl.when(step == pl.num_programs(1) - 1)
    def _():
        sfin_ref[0] = state


def _gla_call(p_b, cos, sin_signed, w2p, b2, s0, *, reverse, rope, dkh, dvh, name):
    b, t, _ = p_b.shape
    tg = min(GLA_TG, t)
    ng = t // tg
    lr_blk = (2 * dkh + 2 * dvh) // LANES
    dv = dvh // GLA_HEADS
    assert dv % LANES == 0

    def blk(i):
        return ng - 1 - i if reverse else i

    return pl.pallas_call(
        functools.partial(_gla_kernel, reverse=reverse, rope=rope),
        grid=(b, ng),
        in_specs=[
            pl.BlockSpec((1, tg, dkh), lambda n, i: (n, blk(i), 0)),
            pl.BlockSpec((1, tg, dkh), lambda n, i: (n, blk(i), 1)),
            pl.BlockSpec((1, tg, dvh), lambda n, i: (n, blk(i), 2 * dkh // dvh)),
            pl.BlockSpec((1, tg, LANES), lambda n, i: (n, blk(i), lr_blk)),
            pl.BlockSpec((tg, dkh), lambda n, i: (blk(i), 0)),
            pl.BlockSpec((tg, dkh), lambda n, i: (blk(i), 0)),
            pl.BlockSpec((LANES, dkh), lambda n, i: (0, 0)),
            pl.BlockSpec((1, dkh), lambda n, i: (0, 0)),
            pl.BlockSpec((1, dv, dkh), lambda n, i: (n, 0, 0)),
        ],
        out_specs=[
            pl.BlockSpec((1, tg, dvh), lambda n, i: (n, blk(i), 0)),
            pl.BlockSpec((1, dv, dkh), lambda n, i: (n, 0, 0)),
        ],
        out_shape=[
            jax.ShapeDtypeStruct((b, t, dvh), F32),
            jax.ShapeDtypeStruct((b, dv, dkh), F32),
        ],
        scratch_shapes=[pltpu.VMEM((dv, dkh), F32)],
        compiler_params=_params("parallel", "arbitrary"),
        name=name,
    )(p_b, p_b, p_b, p_b, cos, sin_signed, w2p, b2, s0)


def _rope_tables(l, dk):
    quarter = dk // 4
    pos = jnp.arange(l)
    rows_pos = (pos // GRID_W).astype(F32)
    cols_pos = (pos % GRID_W).astype(F32)
    inv = ROPE_BASE ** (-jnp.arange(0, 2 * quarter, 2, dtype=F32) / (2 * quarter))
    ang_r = rows_pos[:, None] * inv[None, :]
    ang_c = cols_pos[:, None] * inv[None, :]
    cos = jnp.concatenate([jnp.cos(ang_r)] * 2 + [jnp.cos(ang_c)] * 2, axis=1)
    sin = jnp.concatenate([-jnp.sin(ang_r), jnp.sin(ang_r), -jnp.sin(ang_c), jnp.sin(ang_c)], axis=1)
    return jnp.tile(cos, (1, GLA_HEADS)), jnp.tile(sin, (1, GLA_HEADS))


def _conv_kernel(a_ref, g_ref, z_ref, ap_ref, gp_ref, an_ref, gn_ref, w_ref, cb_ref, lg_ref, lb_ref,
                 o_ref, u_ref, us_ref):
    t = a_ref.shape[1]
    nshift = u_ref.shape[0] - SUBLANES
    step = pl.program_id(1)
    nstep = pl.num_programs(1)

    def glu(a, g):
        return a.astype(F32) * jax.nn.sigmoid(g.astype(F32))

    prev_ok = (step > 0).astype(F32)
    next_ok = (step < nstep - 1).astype(F32)
    u_ref[0:CONV_HALO, :] = glu(ap_ref[0], gp_ref[0]) * prev_ok
    u_ref[CONV_HALO:CONV_HALO + t, :] = glu(a_ref[0], g_ref[0])
    u_ref[CONV_HALO + t:2 * CONV_HALO + t, :] = glu(an_ref[0], gn_ref[0]) * next_ok

    for b in range(1, SUBLANES):
        us_ref[b - 1] = u_ref[b:b + nshift, :]

    off = CONV_HALO - CONV_K // 2
    for rc in range(t // CONV_RC):
        base = rc * CONV_RC
        acc = None
        for j in range(CONV_K):
            q8, b = divmod(off + j, SUBLANES)
            lo = base + q8 * SUBLANES
            src = u_ref[lo:lo + CONV_RC, :] if b == 0 else us_ref[b - 1, lo:lo + CONV_RC, :]
            term = src * w_ref[j:j + 1, :]
            acc = term if acc is None else acc + term
        acc = acc + cb_ref[...]
        mu = jnp.mean(acc, axis=-1, keepdims=True)
        xc = acc - mu
        var = jnp.mean(xc * xc, axis=-1, keepdims=True)
        y = xc * lax.rsqrt(var + LN_EPS) * lg_ref[...] + lb_ref[...]
        z = z_ref[0, base:base + CONV_RC, :].astype(F32)
        o_ref[0, base:base + CONV_RC, :] = (_silu(y) * _silu(z)).astype(o_ref.dtype)


def _conv_call(p, conv_w, conv_b, ln_g, ln_b, col0, d_c, name):
    b, t, _ = p.shape
    tt = min(CONV_T, t)
    nt = t // tt
    cb = col0 // d_c
    hb = tt // CONV_HALO
    nhalo = t // CONV_HALO
    w = jnp.zeros((CONV_K + 1, d_c), F32).at[:CONV_K].set(conv_w)

    def prev(i):
        return jnp.maximum(i * hb - 1, 0)

    def nxt(i):
        return jnp.minimum((i + 1) * hb, nhalo - 1)

    row = lambda v: v.reshape(1, d_c)
    return pl.pallas_call(
        _conv_kernel,
        grid=(b, nt),
        in_specs=[
            pl.BlockSpec((1, tt, d_c), lambda n, i: (n, i, cb)),
            pl.BlockSpec((1, tt, d_c), lambda n, i: (n, i, cb + 1)),
            pl.BlockSpec((1, tt, d_c), lambda n, i: (n, i, cb + 2)),
            pl.BlockSpec((1, CONV_HALO, d_c), lambda n, i: (n, prev(i), cb)),
            pl.BlockSpec((1, CONV_HALO, d_c), lambda n, i: (n, prev(i), cb + 1)),
            pl.BlockSpec((1, CONV_HALO, d_c), lambda n, i: (n, nxt(i), cb)),
            pl.BlockSpec((1, CONV_HALO, d_c), lambda n, i: (n, nxt(i), cb + 1)),
            pl.BlockSpec((CONV_K + 1, d_c), lambda n, i: (0, 0)),
            pl.BlockSpec((1, d_c), lambda n, i: (0, 0)),
            pl.BlockSpec((1, d_c), lambda n, i: (0, 0)),
            pl.BlockSpec((1, d_c), lambda n, i: (0, 0)),
        ],
        out_specs=pl.BlockSpec((1, tt, d_c), lambda n, i: (n, i, 0)),
        out_shape=jax.ShapeDtypeStruct((b, t, d_c), BF16),
        scratch_shapes=[pltpu.VMEM((tt + 2 * CONV_HALO, d_c), F32),
                        pltpu.VMEM((SUBLANES - 1, tt + 2 * CONV_HALO - SUBLANES, d_c), F32)],
        compiler_params=_params("parallel", "arbitrary"),
        name=name,
    )(p, p, p, p, p, p, p, w, row(conv_b), row(ln_g), row(ln_b))


def _out_kernel(oa_ref, of_ref, ob_ref, zb_ref, oc_ref, x_ref, gate_ref, gn_ref,
                w_ref, lg_ref, lb_ref, *rest, alpha, emit_h):
    if emit_h:
        mod_ref, o_ref, h_ref, y0_ref, y1_ref, lhs_ref = rest
    else:
        o_ref, y0_ref, y1_ref, lhs_ref = rest
    tm = x_ref.shape[0]
    d_a = oa_ref.shape[1]
    dvh = of_ref.shape[1]
    dv = dvh // GLA_HEADS
    step = pl.program_id(0)

    @pl.when(step == 0)
    def _():
        y1_ref[...] = jnp.zeros_like(y1_ref)

    def run(y_prev, y_cur):
        gate = gate_ref[0]
        if emit_h:
            shift = mod_ref[0, 0:1, :]
            scale1 = 1.0 + mod_ref[0, 1:2, :]
        for c in range(tm // OUT_LN_ROWS):
            sl = slice(c * OUT_LN_ROWS, (c + 1) * OUT_LN_ROWS)
            rv = alpha * x_ref[sl, :] + gate * y_prev[sl, :]
            xn = _layer_norm_rows(rv) * lg_ref[...] + lb_ref[...]
            o_ref[sl, :] = xn
            if emit_h:
                h_ref[sl, :] = (_layer_norm_rows(xn) * scale1 + shift).astype(BF16)

        s = of_ref[...] + ob_ref[...]
        parts = []
        for h in range(GLA_HEADS):
            sh = s[:, h * dv:(h + 1) * dv]
            ms = jnp.mean(sh * sh, axis=-1, keepdims=True)
            parts.append(sh * lax.rsqrt(ms + RMS_EPS))
        out_b = jnp.concatenate(parts, axis=1) * gn_ref[...] * _silu(zb_ref[...])
        lhs_ref[:, :d_a] = oa_ref[...]
        lhs_ref[:, d_a:d_a + dvh] = out_b.astype(BF16)
        lhs_ref[:, d_a + dvh:] = oc_ref[...]
        y_cur[...] = _dot(lhs_ref[...], w_ref[...])

    @pl.when(step % 2 == 0)
    def _():
        run(y1_ref, y0_ref)

    @pl.when(step % 2 == 1)
    def _():
        run(y0_ref, y1_ref)


def _out_call(oa, o_f, o_b, p_b, oc, x2, gate, gla_norm, w_out, ln_g, ln_b, next_mod, rows_per_gate, zb_col,
              alpha, name):
    m, d = x2.shape
    d_a, dvh, d_c = oa.shape[1], o_f.shape[1], oc.shape[1]
    tm = min(OUT_TM, m)
    per = rows_per_gate // tm
    emit_h = next_mod is not None
    w_bf = w_out.astype(BF16)
    gn = jnp.tile(gla_norm, GLA_HEADS).reshape(1, dvh)
    nt = m // tm
    cur = lambda i: jnp.minimum(i, nt - 1)
    prev = lambda i: jnp.maximum(i - 1, 0)
    full = lambda shape: pl.BlockSpec(shape, lambda i: (0,) * len(shape))
    rows = pl.BlockSpec((tm, d), lambda i: (prev(i), 0))
    in_specs = [
        pl.BlockSpec((tm, d_a), lambda i: (cur(i), 0)),
        pl.BlockSpec((tm, dvh), lambda i: (cur(i), 0)),
        pl.BlockSpec((tm, dvh), lambda i: (cur(i), 0)),
        pl.BlockSpec((tm, dvh), lambda i: (cur(i), zb_col // dvh)),
        pl.BlockSpec((tm, d_c), lambda i: (cur(i), 0)),
        rows,
        pl.BlockSpec((1, 1, d), lambda i: (prev(i) // per, 0, 0)),
        full((1, dvh)),
        full((d_a + dvh + d_c, d)),
        full((1, d)),
        full((1, d)),
    ]
    args = [oa, o_f, o_b, p_b, oc, x2, gate, gn, w_bf, ln_g.reshape(1, d), ln_b.reshape(1, d)]
    out_specs = rows
    out_shape = jax.ShapeDtypeStruct((m, d), F32)
    if emit_h:
        in_specs.append(pl.BlockSpec((1, 2, d), lambda i: (prev(i) // per, 0, 0)))
        args.append(next_mod)
        out_specs = [rows, rows]
        out_shape = [out_shape, jax.ShapeDtypeStruct((m, d), BF16)]
    return pl.pallas_call(
        functools.partial(_out_kernel, alpha=alpha, emit_h=emit_h),
        grid=(nt + 1,),
        in_specs=in_specs,
        out_specs=out_specs,
        out_shape=out_shape,
        scratch_shapes=[pltpu.VMEM((tm, d), F32), pltpu.VMEM((tm, d), F32),
                        pltpu.VMEM((tm, d_a + dvh + d_c), BF16)],
        compiler_params=_params("arbitrary"),
        name=name,
    )(*args)


def kernel(x, c, ctx, c_ctx, w_ada, b_ada, w_in, rpb, gla_w2, gla_b, gla_norm, conv_w, conv_b,
           conv_ln_g, conv_ln_b, w_out, post_ln_g, post_ln_b):
    b, l, d = x.shape
    lc = ctx.shape[1]
    depth = w_ada.shape[0]
    heads_a = rpb.shape[1]
    d_a = heads_a * NA_HEAD_DIM
    dkh = gla_w2.shape[-1]
    dvh = gla_norm.shape[-1] * GLA_HEADS
    d_c = conv_w.shape[-1]
    n_a = 4 * d_a
    n_b = 2 * dkh + 2 * dvh + 2 * GATE_RANK
    n_c = 3 * d_c
    alpha = (2 * depth) ** 0.25
    assert w_in.shape[-1] == n_a + n_b + n_c and d_a + dvh + d_c == w_out.shape[1]
    tn_c = n_c // 2
    tn_b = PROJ_TN
    n_bp = -(-n_b // tn_b) * tn_b
    assert n_a + n_bp <= w_in.shape[-1] and tn_c % LANES == 0

    cond = jnp.zeros((8, d), F32).at[:b].set(c).at[b].set(c_ctx)
    mod = _ada_call(cond, w_ada, b_ada)

    na_bias = _na_bias_table(rpb, l // GRID_W)
    w_in_t = jnp.swapaxes(w_in, 1, 2)

    cos, sin_signed = _rope_tables(l, dkh // GLA_HEADS)
    ones_c = jnp.ones((lc, dkh), F32)
    zeros_c = jnp.zeros((lc, dkh), F32)

    def layer_mods(layer):
        shift, scale, gate = jnp.split(mod[layer], 3, axis=-1)
        mod_lat = jnp.stack([shift[:b], scale[:b]], axis=1)
        mod_ctx = jnp.stack([shift[b:b + 1], scale[b:b + 1]], axis=1)
        return mod_lat, mod_ctx, gate[:b].reshape(b, 1, d), gate[b:b + 1].reshape(1, 1, d)

    x2 = x.reshape(b * l, d)
    cx2 = ctx.reshape(b * lc, d)
    mod_lat, mod_ctx, gate_lat, gate_ctx = layer_mods(0)
    h2 = _ln_mod_call(x2, mod_lat, l, "ln_mod")
    hc2 = _ln_mod_call(cx2, mod_ctx, b * lc, "ln_mod_ctx")
    for layer in range(depth):
        last = layer == depth - 1
        if not last:
            next_lat, next_ctx, next_gate_lat, next_gate_ctx = layer_mods(layer + 1)
        else:
            next_lat = next_ctx = None

        def project(hh, rows_n, tag):
            pa = _proj_call(hh, w_in_t, layer, 0, n_a, PROJ_TN, BF16, "proj_a" + tag)
            pb = _proj_call(hh, w_in_t, layer, n_a, n_bp, tn_b, F32, "proj_b" + tag)
            pc = _proj_call(hh, w_in_t, layer, n_a + n_b, n_c, tn_c, BF16, "proj_c" + tag)
            return pa.reshape(b, rows_n, n_a), pb.reshape(b, rows_n, n_bp), pc.reshape(b, rows_n, n_c)

        p_a, p_b, p_c = project(h2, l, "")
        pc_a, pc_b, pc_c = project(hc2, lc, "_ctx")

        out_a = _na_call(p_a, pc_a, na_bias[layer], d_a)

        w2p = [jnp.zeros((LANES, dkh), F32).at[i * GATE_RANK:(i + 1) * GATE_RANK].set(gla_w2[layer, i])
               for i in range(2)]
        b2 = [gla_b[layer, i].reshape(1, dkh) for i in range(2)]
        s0 = jnp.zeros((b, dvh // GLA_HEADS, dkh), F32)
        gla = functools.partial(_gla_call, dkh=dkh, dvh=dvh)
        oc_f, s_f = gla(pc_b, ones_c, zeros_c, w2p[0], b2[0], s0, reverse=False, rope=False, name="gla_ctx_fwd")
        oc_b, s_b = gla(pc_b, ones_c, zeros_c, w2p[1], b2[1], s0, reverse=True, rope=False, name="gla_ctx_bwd")
        o_f, _ = gla(p_b, cos, sin_signed, w2p[0], b2[0], s_f, reverse=False, rope=True, name="gla_fwd")
        o_b, _ = gla(p_b, cos, sin_signed, w2p[1], b2[1], s_b, reverse=True, rope=True, name="gla_bwd")

        out_c = _conv_call(p_c, conv_w[layer], conv_b[layer], conv_ln_g[layer], conv_ln_b[layer],
                           0, d_c, "conv")

        zb_col = 2 * dkh + dvh
        res = _out_call(out_a.reshape(b * l, d_a), o_f.reshape(b * l, dvh), o_b.reshape(b * l, dvh),
                        p_b.reshape(b * l, n_bp), out_c.reshape(b * l, d_c), x2, gate_lat,
                        gla_norm[layer], w_out[layer], post_ln_g[layer], post_ln_b[layer], next_lat,
                        l, zb_col, alpha, "out_proj")
        if last:
            x2 = res
        else:
            out_a_c = _ctx_attn_call(pc_a, d_a)
            out_c_c = _conv_call(pc_c, conv_w[layer], conv_b[layer], conv_ln_g[layer], conv_ln_b[layer],
                                 0, d_c, "conv_ctx")
            cx2, hc2 = _out_call(out_a_c.reshape(b * lc, d_a), oc_f.reshape(b * lc, dvh),
                                 oc_b.reshape(b * lc, dvh), pc_b.reshape(b * lc, n_bp),
                                 out_c_c.reshape(b * lc, d_c), cx2, gate_ctx, gla_norm[layer], w_out[layer],
                                 post_ln_g[layer], post_ln_b[layer], next_ctx, b * lc, zb_col, alpha,
                                 "out_proj_ctx")
            x2, h2 = res
            gate_lat, gate_ctx = next_gate_lat, next_gate_ctx
    return x2.reshape(b, l, d)
```

```python
import functools
import math

import numpy as np
import jax
import jax.numpy as jnp
from jax import lax
from jax.experimental import pallas as pl
from jax.experimental.pallas import tpu as pltpu

F32 = jnp.float32
BF16 = jnp.bfloat16

GRID_W = 64
NA_HEAD_DIM = 64
NA_KH_MAX = 8
NA_KW = 16
GLA_HEADS = 4
GATE_RANK = 16
GATE_TAU = 16.0
GLA_CHUNK = 64
CONV_K = 31
ROPE_BASE = 10000.0
LN_EPS = 1e-5
RMS_EPS = 1e-6

LANES = 128
SUBLANES = 8
VMEM_LIMIT_BYTES = 56 * 1024 * 1024

NEG_BIG = -1e30
LOG2E = 1.4426950408889634

PROJ_TM = 1024
PROJ_TN = 1024
PROJ_WT_ROWS = 256
OUT_TM = 512
LN_TM = 1024
LN_ROWS = 32
OUT_LN_ROWS = 16
LN_UNROLL = 4
ATT_TQ_ROWS = 16
ATT_ITEM_ROWS = 2
ATT_SKEW = 1
GLA_TG = 1024
GLA_CUMSUM_ROWS = 256
CONV_T = 1024
CONV_HALO = 16
CONV_RC = 64


def _silu(x):
    return x * jax.nn.sigmoid(x)


def _dot(a, b):
    return jnp.dot(a, b, preferred_element_type=F32)


def _dot_nt(a, b):
    return lax.dot_general(a, b, (((1,), (1,)), ((), ())), preferred_element_type=F32)


def _dot_tn(a, b):
    return lax.dot_general(a, b, (((0,), (0,)), ((), ())), preferred_element_type=F32)


def _shr(x, pow2):
    shift = pow2.bit_length() - 1
    assert 1 << shift == pow2
    return jnp.right_shift(x, shift)


def _split_hi_lo(x):
    hi = x.astype(BF16)
    lo = (x - hi.astype(F32)).astype(BF16)
    return hi, lo


def _params(*sem):
    return pltpu.CompilerParams(dimension_semantics=sem, vmem_limit_bytes=VMEM_LIMIT_BYTES)


def _ada_kernel(c_ref, w_ref, b_ref, o_ref):
    s = _silu(c_ref[...]).astype(BF16)
    o_ref[0] = _dot(s, w_ref[0].astype(BF16)) + b_ref[0]


def _ada_call(cond, w_ada, b_ada, tn=1536):
    depth, d, n = w_ada.shape
    rows = cond.shape[0]
    return pl.pallas_call(
        _ada_kernel,
        grid=(depth, n // tn),
        in_specs=[
            pl.BlockSpec((rows, d), lambda l, j: (0, 0)),
            pl.BlockSpec((1, d, tn), lambda l, j: (l, 0, j)),
            pl.BlockSpec((1, 1, tn), lambda l, j: (l, 0, j)),
        ],
        out_specs=pl.BlockSpec((1, rows, tn), lambda l, j: (l, 0, j)),
        out_shape=jax.ShapeDtypeStruct((depth, rows, n), F32),
        compiler_params=_params("parallel", "parallel"),
        name="ada_mod",
    )(cond, w_ada, b_ada.reshape(depth, 1, n))


def _layer_norm_rows(v):
    mu = jnp.mean(v, axis=-1, keepdims=True)
    vc = v - mu
    var = jnp.mean(vc * vc, axis=-1, keepdims=True)
    return vc * lax.rsqrt(var + LN_EPS)


def _ln_mod_kernel(x_ref, mod_ref, h_ref):
    shift = mod_ref[0, 0:1, :]
    scale1 = 1.0 + mod_ref[0, 1:2, :]

    def body(i, carry):
        r0 = pl.multiple_of(i * LN_ROWS, LN_ROWS)
        hn = _layer_norm_rows(x_ref[pl.ds(r0, LN_ROWS), :])
        h_ref[pl.ds(r0, LN_ROWS), :] = (hn * scale1 + shift).astype(BF16)
        return carry

    lax.fori_loop(0, x_ref.shape[0] // LN_ROWS, body, 0, unroll=LN_UNROLL)


def _ln_mod_call(x2, mod, rows_per_mod, name):
    m, d = x2.shape
    tm = min(LN_TM, m)
    per = rows_per_mod // tm
    return pl.pallas_call(
        _ln_mod_kernel,
        grid=(m // tm,),
        in_specs=[
            pl.BlockSpec((tm, d), lambda i: (i, 0)),
            pl.BlockSpec((1, 2, d), lambda i: (i // per, 0, 0)),
        ],
        out_specs=pl.BlockSpec((tm, d), lambda i: (i, 0)),
        out_shape=jax.ShapeDtypeStruct((m, d), BF16),
        compiler_params=_params("parallel"),
        name=name,
    )(x2, mod)


def _matmul_kernel(h_ref, *refs, shift):
    *w_refs, o_ref, wb_ref = refs

    @pl.when(pl.program_id(1) == 0)
    def _():
        tn = wb_ref.shape[1]
        step_rows = math.gcd(tn, PROJ_WT_ROWS)
        for c0 in range(0, tn, step_rows):
            lo, hi = c0 + shift, c0 + shift + step_rows
            pieces = []
            if lo < tn:
                pieces.append(w_refs[0][0, lo:min(hi, tn), :])
            if hi > tn:
                pieces.append(w_refs[1][0, max(lo, tn) - tn:hi - tn, :])
            rows_f32 = pieces[0] if len(pieces) == 1 else jnp.concatenate(pieces, axis=0)
            wb_ref[:, c0:c0 + step_rows] = jnp.transpose(rows_f32).astype(BF16)

    o_ref[...] = _dot(h_ref[...], wb_ref[...]).astype(o_ref.dtype)


def _proj_call(h2, w_t, layer, col0, ncols, tn, out_dtype, name):
    m, d = h2.shape
    tm = min(PROJ_TM * (2 if out_dtype == BF16 and col0 % tn == 0 else 1), m)
    blk0, shift = divmod(col0, tn)
    nblk = ncols // tn
    assert ncols % tn == 0 and col0 + ncols <= w_t.shape[1] and shift % SUBLANES == 0
    w_spec = lambda extra: pl.BlockSpec((1, tn, d), lambda j, i: (layer, blk0 + extra + j, 0))
    w_specs = [w_spec(0), w_spec(1)] if shift else [w_spec(0)]
    return pl.pallas_call(
        functools.partial(_matmul_kernel, shift=shift),
        grid=(nblk, m // tm),
        in_specs=[pl.BlockSpec((tm, d), lambda j, i: (i, 0))] + w_specs,
        out_specs=pl.BlockSpec((tm, tn), lambda j, i: (i, j)),
        out_shape=jax.ShapeDtypeStruct((m, ncols), out_dtype),
        scratch_shapes=[pltpu.VMEM((d, tn), BF16)],
        compiler_params=_params("arbitrary", "arbitrary"),
        name=name,
    )(h2, *([w_t] * len(w_specs)))


def _softmax_pv(q, lane_lo, pieces):
    outs = []
    for h in range(2):
        keep = lane_lo if h == 0 else jnp.logical_not(lane_lo)
        qh = jnp.where(keep, q, jnp.zeros_like(q)) * jnp.asarray(NA_HEAD_DIM ** -0.5, BF16)
        scores = []
        for k, _, bias in pieces:
            s = _dot_nt(qh, k)
            if bias is not None:
                s = s + bias[h]
            scores.append(s)
        m = scores[0].max(axis=-1, keepdims=True)
        for s in scores[1:]:
            m = jnp.maximum(m, s.max(axis=-1, keepdims=True))
        den = None
        acc = None
        for s, (_, v, _) in zip(scores, pieces):
            p = jnp.exp(s - m)
            ps = p.sum(axis=-1, keepdims=True)
            pv = _dot(p.astype(BF16), v)
            den = ps if den is None else den + ps
            acc = pv if acc is None else acc + pv
        outs.append(acc / den)
    return jnp.where(lane_lo, outs[0], outs[1])


def _na_kernel(q_ref, k_ref, v_ref, z_ref, kc_ref, vc_ref, comp_ref, o_ref, vaug_ref, bias_ref, *, rows):
    tq = q_ref.shape[1]
    kh = min(NA_KH_MAX, rows)
    nkeys = kh * GRID_W
    span = vaug_ref.shape[1]
    qi = pl.program_id(2)

    @pl.when((pl.program_id(1) == 0) & (qi == 0))
    def _():
        for h in range(2):
            comp = comp_ref[0, h]
            for d0 in range(bias_ref.shape[0]):
                bias_ref[d0, h] = comp[:, d0 * GRID_W:d0 * GRID_W + nkeys]

    lane_lo = lax.broadcasted_iota(jnp.int32, (tq, LANES), 1) < NA_HEAD_DIM
    q = q_ref[0]
    kc = kc_ref[0]
    vc = vc_ref[0]
    span_row = jnp.clip(qi * ATT_TQ_ROWS - kh // 2, 0, rows - span // GRID_W)
    starts, offs, d0s = [], [], []
    for i in range(ATT_TQ_ROWS):
        r = qi * ATT_TQ_ROWS + i
        rs = jnp.clip(r - kh // 2, 0, rows - kh)
        d0s.append(rs - r + (NA_KH_MAX - 1))
        starts.append(pl.multiple_of(rs * GRID_W, GRID_W))
        offs.append(pl.multiple_of((rs - span_row) * GRID_W, GRID_W))
    v_span = v_ref[0, pl.ds(pl.multiple_of(span_row * GRID_W, GRID_W), span), :]
    one = jnp.ones((), BF16)

    def head_lanes(shape, h):
        lo = lax.broadcasted_iota(jnp.int32, shape, 1) < NA_HEAD_DIM
        return lo if h == 0 else jnp.logical_not(lo)

    qhs, vcs = [], []
    for h in range(2):
        keep = lane_lo if h == 0 else jnp.logical_not(lane_lo)
        qf = jnp.where(keep, q, jnp.zeros_like(q)).astype(F32) * (NA_HEAD_DIM ** -0.5 * LOG2E)
        qhs.append(qf.astype(BF16))
        vaug_ref[h] = jnp.where(head_lanes(v_span.shape, h), v_span, one)
        vcs.append(jnp.where(head_lanes(vc.shape, h), vc, one))

    def stage_scores(h, rows_i):
        qs = qhs[h][rows_i[0] * GRID_W:(rows_i[-1] + 1) * GRID_W]
        s_loc = jnp.concatenate(
            [_dot_nt(qhs[h][i * GRID_W:(i + 1) * GRID_W], k_ref[0, pl.ds(starts[i], nkeys), :])
             + bias_ref[d0s[i], h] for i in rows_i], axis=0)
        return h, s_loc, _dot_nt(qs, kc)

    def stage_softmax(h, s_loc, s_ctx):
        m = jnp.maximum(s_loc.max(axis=-1, keepdims=True), s_ctx.max(axis=-1, keepdims=True))
        return h, jnp.exp2(s_loc - m).astype(BF16), jnp.exp2(s_ctx - m).astype(BF16)

    def stage_values(h, p_loc, p_ctx, rows_i):
        acc = jnp.concatenate(
            [_dot(p_loc[j * GRID_W:(j + 1) * GRID_W], vaug_ref[h, pl.ds(offs[i], nkeys), :])
             for j, i in enumerate(rows_i)], axis=0)
        acc = acc + _dot(p_ctx, vcs[h])
        return acc / pltpu.roll(acc, NA_HEAD_DIM, 1)

    groups = [list(range(g, g + ATT_ITEM_ROWS)) for g in range(0, ATT_TQ_ROWS, ATT_ITEM_ROWS)]
    items = [(h, g) for h in range(2) for g in groups]
    scores, probs, outs = {}, {}, {}
    sk = ATT_SKEW
    for t in range(len(items) + 2 * sk):
        if t < len(items):
            scores[t] = stage_scores(*items[t])
        if 0 <= t - sk < len(items):
            probs[t - sk] = stage_softmax(*scores.pop(t - sk))
        if 0 <= t - 2 * sk < len(items):
            outs[t - 2 * sk] = stage_values(*probs.pop(t - 2 * sk), items[t - 2 * sk][1])
    per_head = [jnp.concatenate([outs[t] for t, (hh, _) in enumerate(items) if hh == h], axis=0)
                for h in range(2)]
    o = jnp.where(lane_lo, per_head[0], per_head[1])
    o_ref[0] = (o * _silu(z_ref[0].astype(F32))).astype(o_ref.dtype)


def _na_bias_table(rpb, rows):
    depth, heads = rpb.shape[:2]
    kh = min(NA_KH_MAX, rows)
    nd = 2 * NA_KH_MAX - kh
    for r in range(rows):
        assert 0 <= int(np.clip(r - kh // 2, 0, rows - kh)) - r + (NA_KH_MAX - 1) < nd

    c = np.arange(GRID_W)[:, None]
    kc = np.arange(GRID_W)[None, :]
    cs = np.clip(c - NA_KW // 2, 0, GRID_W - NA_KW)
    valid_c = (kc >= cs) & (kc < cs + NA_KW)
    dcol = kc - c + (NA_KW - 1)
    sel_col = ((dcol[None] == np.arange(2 * NA_KW - 1)[:, None, None]) & valid_c[None]).astype(np.float32)
    comp = jnp.einsum("lhde,eck->lhcdk", rpb, sel_col, precision=lax.Precision.HIGHEST)
    comp = jnp.where(valid_c[:, None, :], comp * LOG2E, NEG_BIG)
    ncomp = (2 * NA_KH_MAX - 1) * GRID_W
    comp = comp.reshape(depth, heads // 2, 2, GRID_W, ncomp)
    return jnp.pad(comp, ((0, 0),) * 4 + ((0, -ncomp % LANES),), constant_values=NEG_BIG)


def _na_call(p_lat, p_ctx, bias, d_a):
    b, l, _ = p_lat.shape
    lc = p_ctx.shape[1]
    rows = l // GRID_W
    assert rows % ATT_TQ_ROWS == 0
    tq = ATT_TQ_ROWS * GRID_W
    nq = l // tq
    npair = d_a // LANES
    kh = min(NA_KH_MAX, rows)
    nd = 2 * NA_KH_MAX - kh
    ncomp = bias.shape[-1]
    span_rows = min(ATT_TQ_ROWS + kh - 1, rows)

    return pl.pallas_call(
        functools.partial(_na_kernel, rows=rows),
        grid=(npair, b, nq),
        in_specs=[
            pl.BlockSpec((1, tq, LANES), lambda p, i, q: (i, q, p)),
            pl.BlockSpec((1, l, LANES), lambda p, i, q: (i, 0, npair + p)),
            pl.BlockSpec((1, l, LANES), lambda p, i, q: (i, 0, 2 * npair + p)),
            pl.BlockSpec((1, tq, LANES), lambda p, i, q: (i, q, 3 * npair + p)),
            pl.BlockSpec((1, lc, LANES), lambda p, i, q: (i, 0, npair + p)),
            pl.BlockSpec((1, lc, LANES), lambda p, i, q: (i, 0, 2 * npair + p)),
            pl.BlockSpec((1, 2, GRID_W, ncomp), lambda p, i, q: (p, 0, 0, 0)),
        ],
        out_specs=pl.BlockSpec((1, tq, LANES), lambda p, i, q: (i, q, p)),
        out_shape=jax.ShapeDtypeStruct((b, l, d_a), BF16),
        scratch_shapes=[pltpu.VMEM((2, span_rows * GRID_W, LANES), BF16),
                        pltpu.VMEM((nd, 2, GRID_W, kh * GRID_W), F32)],
        compiler_params=_params("arbitrary", "arbitrary", "arbitrary"),
        name="na_attention",
    )(p_lat, p_lat, p_lat, p_lat, p_ctx, p_ctx, bias)


def _ctx_attn_kernel(q_ref, k_ref, v_ref, z_ref, o_ref):
    tq = q_ref.shape[1]
    lane_lo = lax.broadcasted_iota(jnp.int32, (tq, LANES), 1) < NA_HEAD_DIM
    o = _softmax_pv(q_ref[0], lane_lo, [(k_ref[0], v_ref[0], None)])
    o_ref[0] = (o * _silu(z_ref[0].astype(F32))).astype(o_ref.dtype)


def _ctx_attn_call(p_ctx, d_a):
    b, lc, _ = p_ctx.shape
    npair = d_a // LANES
    return pl.pallas_call(
        _ctx_attn_kernel,
        grid=(b, npair),
        in_specs=[
            pl.BlockSpec((1, lc, LANES), lambda i, p: (i, 0, p)),
            pl.BlockSpec((1, lc, LANES), lambda i, p: (i, 0, npair + p)),
            pl.BlockSpec((1, lc, LANES), lambda i, p: (i, 0, 2 * npair + p)),
            pl.BlockSpec((1, lc, LANES), lambda i, p: (i, 0, 3 * npair + p)),
        ],
        out_specs=pl.BlockSpec((1, lc, LANES), lambda i, p: (i, 0, p)),
        out_shape=jax.ShapeDtypeStruct((b, lc, d_a), BF16),
        compiler_params=_params("parallel", "parallel"),
        name="ctx_attention",
    )(p_ctx, p_ctx, p_ctx, p_ctx)


def _rope(x, cos, sin_signed):
    halves = []
    for j in range(x.shape[1] // LANES):
        xs = x[:, j * LANES:(j + 1) * LANES]
        lane = lax.broadcasted_iota(jnp.int32, xs.shape, 1)
        first = (lane & 16) == 0
        partner = jnp.where(first, pltpu.roll(xs, LANES - 16, 1), pltpu.roll(xs, 16, 1))
        halves.append(partner)
    partner = jnp.concatenate(halves, axis=1)
    return x * cos + partner * sin_signed


def _gla_kernel(q_ref, k_ref, v_ref, lr_ref, cos_ref, sin_ref, w2_ref, b2_ref, s0_ref,
                o_ref, sfin_ref, s_ref, *, reverse, rope):
    tg = q_ref.shape[1]
    dkh = q_ref.shape[2]
    dvh = v_ref.shape[2]
    dk = dkh // GLA_HEADS
    dv = dvh // GLA_HEADS
    nchunk = tg // GLA_CHUNK
    step = pl.program_id(1)

    @pl.when(step == 0)
    def _():
        s_ref[...] = s0_ref[0]

    lr_hi, lr_lo = _split_hi_lo(lr_ref[0])
    w_hi, w_lo = _split_hi_lo(w2_ref[...])
    logits = _dot(lr_hi, w_hi) + _dot(lr_lo, w_hi) + _dot(lr_hi, w_lo) + b2_ref[...]
    g = (jnp.minimum(logits, 0.0) - jnp.log1p(jnp.exp(-jnp.abs(logits)))) * (1.0 / GATE_TAU)

    tcs = min(tg, GLA_CUMSUM_ROWS)
    r = lax.broadcasted_iota(jnp.int32, (tcs, tcs), 0)
    c = lax.broadcasted_iota(jnp.int32, (tcs, tcs), 1)
    same = _shr(r, GLA_CHUNK) == _shr(c, GLA_CHUNK)
    upto = (c >= r) if reverse else (c <= r)
    t_inc = jnp.where(same & upto, 1.0, 0.0).astype(BF16)
    g_hi, g_lo = _split_hi_lo(g)
    b_inc = jnp.concatenate(
        [_dot(t_inc, g_hi[s0:s0 + tcs]) + _dot(t_inc, g_lo[s0:s0 + tcs]) for s0 in range(0, tg, tcs)], axis=0)

    chunks = [slice(ci * GLA_CHUNK, (ci + 1) * GLA_CHUNK) for ci in range(nchunk)]
    end_row = 0 if reverse else GLA_CHUNK - 1
    b_end = [b_inc[sl][end_row:end_row + 1, :] for sl in chunks]
    b_rest = jnp.concatenate([be - b_inc[sl] for be, sl in zip(b_end, chunks)], axis=0)

    q = q_ref[0]
    k = k_ref[0]
    if rope:
        q = _rope(q, cos_ref[...], sin_ref[...])
        k = _rope(k, cos_ref[...], sin_ref[...])
    q = q * (dk ** -0.5)
    qt = (q * jnp.exp(b_inc)).astype(BF16)
    kt = (k * jnp.exp(-b_inc)).astype(BF16)
    kd = (k * jnp.exp(b_rest)).astype(BF16)
    vb = v_ref[0].astype(BF16)

    lane_head = _shr(lax.broadcasted_iota(jnp.int32, (GLA_CHUNK, dkh), 1), dk)
    hc = GLA_HEADS * GLA_CHUNK
    rk = _shr(lax.broadcasted_iota(jnp.int32, (hc, dkh), 0), GLA_CHUNK)
    ck = _shr(lax.broadcasted_iota(jnp.int32, (hc, dkh), 1), dk)
    mask_k = rk == ck
    rv = _shr(lax.broadcasted_iota(jnp.int32, (hc, dvh), 0), GLA_CHUNK)
    cv = _shr(lax.broadcasted_iota(jnp.int32, (hc, dvh), 1), dv)
    mask_v = rv == cv
    ti = lax.broadcasted_iota(jnp.int32, (GLA_CHUNK, dkh), 0)
    tj = lax.broadcasted_iota(jnp.int32, (GLA_CHUNK, dkh), 1) & (GLA_CHUNK - 1)
    causal = (tj >= ti) if reverse else (tj <= ti)
    zero = jnp.zeros((), BF16)

    def head_stack(x):
        return jnp.concatenate([jnp.where(lane_head == h, x, zero) for h in range(GLA_HEADS)], axis=0)

    o_intra, kv, decay, q_stack = [], [], [], []
    for sl, be in zip(chunks, b_end):
        k_blk = jnp.where(mask_k, jnp.concatenate([kt[sl]] * GLA_HEADS, axis=0), zero)
        att = jnp.where(causal, _dot_nt(qt[sl], k_blk), 0.0)
        v_blk = jnp.where(mask_v, jnp.concatenate([vb[sl]] * GLA_HEADS, axis=0), zero)
        o_intra.append(_dot(att.astype(BF16), v_blk))
        v_rows = jnp.concatenate([vb[sl][:, h * dv:(h + 1) * dv] for h in range(GLA_HEADS)], axis=0)
        kv.append(_dot_tn(v_rows, head_stack(kd[sl])))
        decay.append(jnp.exp(be))
        q_stack.append(head_stack(qt[sl]))

    order = list(range(nchunk - 1, -1, -1)) if reverse else list(range(nchunk))
    state = s_ref[...]
    states = {}
    for ci in order:
        states[ci] = state
        state = decay[ci] * state + kv[ci]
    s_ref[...] = state

    for ci in range(nchunk):
        inter = _dot_nt(q_stack[ci], states[ci].astype(BF16))
        inter = jnp.concatenate([inter[h * GLA_CHUNK:(h + 1) * GLA_CHUNK] for h in range(GLA_HEADS)], axis=1)
        o_ref[0, chunks[ci], :] = o_intra[ci] + inter

    @pl.when(step == pl.num_programs(1) - 1)
    def _():
        sfin_ref[0] = state


def _gla_call(p_b, cos, sin_signed, w2p, b2, s0, *, reverse, rope, dkh, dvh, name):
    b, t, _ = p_b.shape
    tg = min(GLA_TG, t)
    ng = t // tg
    lr_blk = (2 * dkh + 2 * dvh) // LANES
    dv = dvh // GLA_HEADS
    assert dv % LANES == 0

    def blk(i):
        return ng - 1 - i if reverse else i

    return pl.pallas_call(
        functools.partial(_gla_kernel, reverse=reverse, rope=rope),
        grid=(b, ng),
        in_specs=[
            pl.BlockSpec((1, tg, dkh), lambda n, i: (n, blk(i), 0)),
            pl.BlockSpec((1, tg, dkh), lambda n, i: (n, blk(i), 1)),
            pl.BlockSpec((1, tg, dvh), lambda n, i: (n, blk(i), 2 * dkh // dvh)),
            pl.BlockSpec((1, tg, LANES), lambda n, i: (n, blk(i), lr_blk)),
            pl.BlockSpec((tg, dkh), lambda n, i: (blk(i), 0)),
            pl.BlockSpec((tg, dkh), lambda n, i: (blk(i), 0)),
            pl.BlockSpec((LANES, dkh), lambda n, i: (0, 0)),
            pl.BlockSpec((1, dkh), lambda n, i: (0, 0)),
            pl.BlockSpec((1, dv, dkh), lambda n, i: (n, 0, 0)),
        ],
        out_specs=[
            pl.BlockSpec((1, tg, dvh), lambda n, i: (n, blk(i), 0)),
            pl.BlockSpec((1, dv, dkh), lambda n, i: (n, 0, 0)),
        ],
        out_shape=[
            jax.ShapeDtypeStruct((b, t, dvh), F32),
            jax.ShapeDtypeStruct((b, dv, dkh), F32),
        ],
        scratch_shapes=[pltpu.VMEM((dv, dkh), F32)],
        compiler_params=_params("parallel", "arbitrary"),
        name=name,
    )(p_b, p_b, p_b, p_b, cos, sin_signed, w2p, b2, s0)


def _rope_tables(l, dk):
    quarter = dk // 4
    pos = jnp.arange(l)
    rows_pos = (pos // GRID_W).astype(F32)
    cols_pos = (pos % GRID_W).astype(F32)
    inv = ROPE_BASE ** (-jnp.arange(0, 2 * quarter, 2, dtype=F32) / (2 * quarter))
    ang_r = rows_pos[:, None] * inv[None, :]
    ang_c = cols_pos[:, None] * inv[None, :]
    cos = jnp.concatenate([jnp.cos(ang_r)] * 2 + [jnp.cos(ang_c)] * 2, axis=1)
    sin = jnp.concatenate([-jnp.sin(ang_r), jnp.sin(ang_r), -jnp.sin(ang_c), jnp.sin(ang_c)], axis=1)
    return jnp.tile(cos, (1, GLA_HEADS)), jnp.tile(sin, (1, GLA_HEADS))


def _conv_kernel(a_ref, g_ref, z_ref, ap_ref, gp_ref, an_ref, gn_ref, w_ref, cb_ref, lg_ref, lb_ref,
                 o_ref, u_ref, us_ref):
    t = a_ref.shape[1]
    nshift = u_ref.shape[0] - SUBLANES
    step = pl.program_id(1)
    nstep = pl.num_programs(1)

    def glu(a, g):
        return a.astype(F32) * jax.nn.sigmoid(g.astype(F32))

    prev_ok = (step > 0).astype(F32)
    next_ok = (step < nstep - 1).astype(F32)
    u_ref[0:CONV_HALO, :] = glu(ap_ref[0], gp_ref[0]) * prev_ok
    u_ref[CONV_HALO:CONV_HALO + t, :] = glu(a_ref[0], g_ref[0])
    u_ref[CONV_HALO + t:2 * CONV_HALO + t, :] = glu(an_ref[0], gn_ref[0]) * next_ok

    for b in range(1, SUBLANES):
        us_ref[b - 1] = u_ref[b:b + nshift, :]

    off = CONV_HALO - CONV_K // 2
    for rc in range(t // CONV_RC):
        base = rc * CONV_RC
        acc = None
        for j in range(CONV_K):
            q8, b = divmod(off + j, SUBLANES)
            lo = base + q8 * SUBLANES
            src = u_ref[lo:lo + CONV_RC, :] if b == 0 else us_ref[b - 1, lo:lo + CONV_RC, :]
            term = src * w_ref[j:j + 1, :]
            acc = term if acc is None else acc + term
        acc = acc + cb_ref[...]
        mu = jnp.mean(acc, axis=-1, keepdims=True)
        xc = acc - mu
        var = jnp.mean(xc * xc, axis=-1, keepdims=True)
        y = xc * lax.rsqrt(var + LN_EPS) * lg_ref[...] + lb_ref[...]
        z = z_ref[0, base:base + CONV_RC, :].astype(F32)
        o_ref[0, base:base + CONV_RC, :] = (_silu(y) * _silu(z)).astype(o_ref.dtype)


def _conv_call(p, conv_w, conv_b, ln_g, ln_b, col0, d_c, name):
    b, t, _ = p.shape
    tt = min(CONV_T, t)
    nt = t // tt
    cb = col0 // d_c
    hb = tt // CONV_HALO
    nhalo = t // CONV_HALO
    w = jnp.zeros((CONV_K + 1, d_c), F32).at[:CONV_K].set(conv_w)

    def prev(i):
        return jnp.maximum(i * hb - 1, 0)

    def nxt(i):
        return jnp.minimum((i + 1) * hb, nhalo - 1)

    row = lambda v: v.reshape(1, d_c)
    return pl.pallas_call(
        _conv_kernel,
        grid=(b, nt),
        in_specs=[
            pl.BlockSpec((1, tt, d_c), lambda n, i: (n, i, cb)),
            pl.BlockSpec((1, tt, d_c), lambda n, i: (n, i, cb + 1)),
            pl.BlockSpec((1, tt, d_c), lambda n, i: (n, i, cb + 2)),
            pl.BlockSpec((1, CONV_HALO, d_c), lambda n, i: (n, prev(i), cb)),
            pl.BlockSpec((1, CONV_HALO, d_c), lambda n, i: (n, prev(i), cb + 1)),
            pl.BlockSpec((1, CONV_HALO, d_c), lambda n, i: (n, nxt(i), cb)),
            pl.BlockSpec((1, CONV_HALO, d_c), lambda n, i: (n, nxt(i), cb + 1)),
            pl.BlockSpec((CONV_K + 1, d_c), lambda n, i: (0, 0)),
            pl.BlockSpec((1, d_c), lambda n, i: (0, 0)),
            pl.BlockSpec((1, d_c), lambda n, i: (0, 0)),
            pl.BlockSpec((1, d_c), lambda n, i: (0, 0)),
        ],
        out_specs=pl.BlockSpec((1, tt, d_c), lambda n, i: (n, i, 0)),
        out_shape=jax.ShapeDtypeStruct((b, t, d_c), BF16),
        scratch_shapes=[pltpu.VMEM((tt + 2 * CONV_HALO, d_c), F32),
                        pltpu.VMEM((SUBLANES - 1, tt + 2 * CONV_HALO - SUBLANES, d_c), F32)],
        compiler_params=_params("parallel", "arbitrary"),
        name=name,
    )(p, p, p, p, p, p, p, w, row(conv_b), row(ln_g), row(ln_b))


def _out_kernel(oa_ref, of_ref, ob_ref, zb_ref, oc_ref, x_ref, gate_ref, gn_ref,
                w_ref, lg_ref, lb_ref, *rest, alpha, emit_h):
    if emit_h:
        mod_ref, o_ref, h_ref, y0_ref, y1_ref, lhs_ref = rest
    else:
        o_ref, y0_ref, y1_ref, lhs_ref = rest
    tm = x_ref.shape[0]
    d_a = oa_ref.shape[1]
    dvh = of_ref.shape[1]
    dv = dvh // GLA_HEADS
    step = pl.program_id(0)

    @pl.when(step == 0)
    def _():
        y1_ref[...] = jnp.zeros_like(y1_ref)

    def run(y_prev, y_cur):
        gate = gate_ref[0]
        if emit_h:
            shift = mod_ref[0, 0:1, :]
            scale1 = 1.0 + mod_ref[0, 1:2, :]
        for c in range(tm // OUT_LN_ROWS):
            sl = slice(c * OUT_LN_ROWS, (c + 1) * OUT_LN_ROWS)
            rv = alpha * x_ref[sl, :] + gate * y_prev[sl, :]
            xn = _layer_norm_rows(rv) * lg_ref[...] + lb_ref[...]
            o_ref[sl, :] = xn
            if emit_h:
                h_ref[sl, :] = (_layer_norm_rows(xn) * scale1 + shift).astype(BF16)

        s = of_ref[...] + ob_ref[...]
        parts = []
        for h in range(GLA_HEADS):
            sh = s[:, h * dv:(h + 1) * dv]
            ms = jnp.mean(sh * sh, axis=-1, keepdims=True)
            parts.append(sh * lax.rsqrt(ms + RMS_EPS))
        out_b = jnp.concatenate(parts, axis=1) * gn_ref[...] * _silu(zb_ref[...])
        lhs_ref[:, :d_a] = oa_ref[...]
        lhs_ref[:, d_a:d_a + dvh] = out_b.astype(BF16)
        lhs_ref[:, d_a + dvh:] = oc_ref[...]
        y_cur[...] = _dot(lhs_ref[...], w_ref[...])

    @pl.when(step % 2 == 0)
    def _():
        run(y1_ref, y0_ref)

    @pl.when(step % 2 == 1)
    def _():
        run(y0_ref, y1_ref)


def _out_call(oa, o_f, o_b, p_b, oc, x2, gate, gla_norm, w_out, ln_g, ln_b, next_mod, rows_per_gate, zb_col,
              alpha, name):
    m, d = x2.shape
    d_a, dvh, d_c = oa.shape[1], o_f.shape[1], oc.shape[1]
    tm = min(OUT_TM, m)
    per = rows_per_gate // tm
    emit_h = next_mod is not None
    w_bf = w_out.astype(BF16)
    gn = jnp.tile(gla_norm, GLA_HEADS).reshape(1, dvh)
    nt = m // tm
    cur = lambda i: jnp.minimum(i, nt - 1)
    prev = lambda i: jnp.maximum(i - 1, 0)
    full = lambda shape: pl.BlockSpec(shape, lambda i: (0,) * len(shape))
    rows = pl.BlockSpec((tm, d), lambda i: (prev(i), 0))
    in_specs = [
        pl.BlockSpec((tm, d_a), lambda i: (cur(i), 0)),
        pl.BlockSpec((tm, dvh), lambda i: (cur(i), 0)),
        pl.BlockSpec((tm, dvh), lambda i: (cur(i), 0)),
        pl.BlockSpec((tm, dvh), lambda i: (cur(i), zb_col // dvh)),
        pl.BlockSpec((tm, d_c), lambda i: (cur(i), 0)),
        rows,
        pl.BlockSpec((1, 1, d), lambda i: (prev(i) // per, 0, 0)),
        full((1, dvh)),
        full((d_a + dvh + d_c, d)),
        full((1, d)),
        full((1, d)),
    ]
    args = [oa, o_f, o_b, p_b, oc, x2, gate, gn, w_bf, ln_g.reshape(1, d), ln_b.reshape(1, d)]
    out_specs = rows
    out_shape = jax.ShapeDtypeStruct((m, d), F32)
    if emit_h:
        in_specs.append(pl.BlockSpec((1, 2, d), lambda i: (prev(i) // per, 0, 0)))
        args.append(next_mod)
        out_specs = [rows, rows]
        out_shape = [out_shape, jax.ShapeDtypeStruct((m, d), BF16)]
    return pl.pallas_call(
        functools.partial(_out_kernel, alpha=alpha, emit_h=emit_h),
        grid=(nt + 1,),
        in_specs=in_specs,
        out_specs=out_specs,
        out_shape=out_shape,
        scratch_shapes=[pltpu.VMEM((tm, d), F32), pltpu.VMEM((tm, d), F32),
                        pltpu.VMEM((tm, d_a + dvh + d_c), BF16)],
        compiler_params=_params("arbitrary"),
        name=name,
    )(*args)


def kernel(x, c, ctx, c_ctx, w_ada, b_ada, w_in, rpb, gla_w2, gla_b, gla_norm, conv_w, conv_b,
           conv_ln_g, conv_ln_b, w_out, post_ln_g, post_ln_b):
    b, l, d = x.shape
    lc = ctx.shape[1]
    depth = w_ada.shape[0]
    heads_a = rpb.shape[1]
    d_a = heads_a * NA_HEAD_DIM
    dkh = gla_w2.shape[-1]
    dvh = gla_norm.shape[-1] * GLA_HEADS
    d_c = conv_w.shape[-1]
    n_a = 4 * d_a
    n_b = 2 * dkh + 2 * dvh + 2 * GATE_RANK
    n_c = 3 * d_c
    alpha = (2 * depth) ** 0.25
    assert w_in.shape[-1] == n_a + n_b + n_c and d_a + dvh + d_c == w_out.shape[1]
    tn_c = n_c // 2
    tn_b = PROJ_TN
    n_bp = -(-n_b // tn_b) * tn_b
    assert n_a + n_bp <= w_in.shape[-1] and tn_c % LANES == 0

    cond = jnp.zeros((8, d), F32).at[:b].set(c).at[b].set(c_ctx)
    mod = _ada_call(cond, w_ada, b_ada)

    na_bias = _na_bias_table(rpb, l // GRID_W)
    w_in_t = jnp.swapaxes(w_in, 1, 2)

    cos, sin_signed = _rope_tables(l, dkh // GLA_HEADS)
    ones_c = jnp.ones((lc, dkh), F32)
    zeros_c = jnp.zeros((lc, dkh), F32)

    def layer_mods(layer):
        shift, scale, gate = jnp.split(mod[layer], 3, axis=-1)
        mod_lat = jnp.stack([shift[:b], scale[:b]], axis=1)
        mod_ctx = jnp.stack([shift[b:b + 1], scale[b:b + 1]], axis=1)
        return mod_lat, mod_ctx, gate[:b].reshape(b, 1, d), gate[b:b + 1].reshape(1, 1, d)

    x2 = x.reshape(b * l, d)
    cx2 = ctx.reshape(b * lc, d)
    mod_lat, mod_ctx, gate_lat, gate_ctx = layer_mods(0)
    h2 = _ln_mod_call(x2, mod_lat, l, "ln_mod")
    hc2 = _ln_mod_call(cx2, mod_ctx, b * lc, "ln_mod_ctx")
    for layer in range(depth):
        last = layer == depth - 1
        if not last:
            next_lat, next_ctx, next_gate_lat, next_gate_ctx = layer_mods(layer + 1)
        else:
            next_lat = next_ctx = None

        def project(hh, rows_n, tag):
            pa = _proj_call(hh, w_in_t, layer, 0, n_a, PROJ_TN, BF16, "proj_a" + tag)
            pb = _proj_call(hh, w_in_t, layer, n_a, n_bp, tn_b, F32, "proj_b" + tag)
            pc = _proj_call(hh, w_in_t, layer, n_a + n_b, n_c, tn_c, BF16, "proj_c" + tag)
            return pa.reshape(b, rows_n, n_a), pb.reshape(b, rows_n, n_bp), pc.reshape(b, rows_n, n_c)

        p_a, p_b, p_c = project(h2, l, "")
        pc_a, pc_b, pc_c = project(hc2, lc, "_ctx")

        out_a = _na_call(p_a, pc_a, na_bias[layer], d_a)

        w2p = [jnp.zeros((LANES, dkh), F32).at[i * GATE_RANK:(i + 1) * GATE_RANK].set(gla_w2[layer, i])
               for i in range(2)]
        b2 = [gla_b[layer, i].reshape(1, dkh) for i in range(2)]
        s0 = jnp.zeros((b, dvh // GLA_HEADS, dkh), F32)
        gla = functools.partial(_gla_call, dkh=dkh, dvh=dvh)
        oc_f, s_f = gla(pc_b, ones_c, zeros_c, w2p[0], b2[0], s0, reverse=False, rope=False, name="gla_ctx_fwd")
        oc_b, s_b = gla(pc_b, ones_c, zeros_c, w2p[1], b2[1], s0, reverse=True, rope=False, name="gla_ctx_bwd")
        o_f, _ = gla(p_b, cos, sin_signed, w2p[0], b2[0], s_f, reverse=False, rope=True, name="gla_fwd")
        o_b, _ = gla(p_b, cos, sin_signed, w2p[1], b2[1], s_b, reverse=True, rope=True, name="gla_bwd")

        out_c = _conv_call(p_c, conv_w[layer], conv_b[layer], conv_ln_g[layer], conv_ln_b[layer],
                           0, d_c, "conv")

        zb_col = 2 * dkh + dvh
        res = _out_call(out_a.reshape(b * l, d_a), o_f.reshape(b * l, dvh), o_b.reshape(b * l, dvh),
                        p_b.reshape(b * l, n_bp), out_c.reshape(b * l, d_c), x2, gate_lat,
                        gla_norm[layer], w_out[layer], post_ln_g[layer], post_ln_b[layer], next_lat,
                        l, zb_col, alpha, "out_proj")
        if last:
            x2 = res
        else:
            out_a_c = _ctx_attn_call(pc_a, d_a)
            out_c_c = _conv_call(pc_c, conv_w[layer], conv_b[layer], conv_ln_g[layer], conv_ln_b[layer],
                                 0, d_c, "conv_ctx")
            cx2, hc2 = _out_call(out_a_c.reshape(b * lc, d_a), oc_f.reshape(b * lc, dvh),
                                 oc_b.reshape(b * lc, dvh), pc_b.reshape(b * lc, n_bp),
                                 out_c_c.reshape(b * lc, d_c), cx2, gate_ctx, gla_norm[layer], w_out[layer],
                                 post_ln_g[layer], post_ln_b[layer], next_ctx, b * lc, zb_col, alpha,
                                 "out_proj_ctx")
            x2, h2 = res
            gate_lat, gate_ctx = next_gate_lat, next_gate_ctx
    return x2.reshape(b, l, d)
```

```python
import functools
import math

import numpy as np
import jax
import jax.numpy as jnp
from jax import lax
from jax.experimental import pallas as pl
from jax.experimental.pallas import tpu as pltpu

F32 = jnp.float32
BF16 = jnp.bfloat16

GRID_W = 64
NA_HEAD_DIM = 64
NA_KH_MAX = 8
NA_KW = 16
GLA_HEADS = 4
GATE_RANK = 16
GATE_TAU = 16.0
GLA_CHUNK = 64
CONV_K = 31
ROPE_BASE = 10000.0
LN_EPS = 1e-5
RMS_EPS = 1e-6

LANES = 128
SUBLANES = 8
VMEM_LIMIT_BYTES = 56 * 1024 * 1024

NEG_BIG = -1e30
LOG2E = 1.4426950408889634

PROJ_TM = 1024
PROJ_TN = 1024
PROJ_WT_ROWS = 256
OUT_TM = 512
LN_TM = 1024
LN_ROWS = 32
OUT_LN_ROWS = 16
LN_UNROLL = 4
ATT_TQ_ROWS = 16
ATT_ITEM_ROWS = 2
ATT_SKEW = 1
GLA_TG = 1024
GLA_CUMSUM_ROWS = 256
CONV_T = 1024
CONV_HALO = 16
CONV_RC = 64


def _silu(x):
    return x * jax.nn.sigmoid(x)


def _dot(a, b):
    return jnp.dot(a, b, preferred_element_type=F32)


def _dot_nt(a, b):
    return lax.dot_general(a, b, (((1,), (1,)), ((), ())), preferred_element_type=F32)


def _dot_tn(a, b):
    return lax.dot_general(a, b, (((0,), (0,)), ((), ())), preferred_element_type=F32)


def _shr(x, pow2):
    shift = pow2.bit_length() - 1
    assert 1 << shift == pow2
    return jnp.right_shift(x, shift)


def _split_hi_lo(x):
    hi = x.astype(BF16)
    lo = (x - hi.astype(F32)).astype(BF16)
    return hi, lo


def _params(*sem):
    return pltpu.CompilerParams(dimension_semantics=sem, vmem_limit_bytes=VMEM_LIMIT_BYTES)


def _ada_kernel(c_ref, w_ref, b_ref, o_ref):
    s = _silu(c_ref[...]).astype(BF16)
    o_ref[0] = _dot(s, w_ref[0].astype(BF16)) + b_ref[0]


def _ada_call(cond, w_ada, b_ada, tn=1536):
    depth, d, n = w_ada.shape
    rows = cond.shape[0]
    return pl.pallas_call(
        _ada_kernel,
        grid=(depth, n // tn),
        in_specs=[
            pl.BlockSpec((rows, d), lambda l, j: (0, 0)),
            pl.BlockSpec((1, d, tn), lambda l, j: (l, 0, j)),
            pl.BlockSpec((1, 1, tn), lambda l, j: (l, 0, j)),
        ],
        out_specs=pl.BlockSpec((1, rows, tn), lambda l, j: (l, 0, j)),
        out_shape=jax.ShapeDtypeStruct((depth, rows, n), F32),
        compiler_params=_params("parallel", "parallel"),
        name="ada_mod",
    )(cond, w_ada, b_ada.reshape(depth, 1, n))


def _layer_norm_rows(v):
    mu = jnp.mean(v, axis=-1, keepdims=True)
    vc = v - mu
    var = jnp.mean(vc * vc, axis=-1, keepdims=True)
    return vc * lax.rsqrt(var + LN_EPS)


def _ln_mod_kernel(x_ref, mod_ref, h_ref):
    shift = mod_ref[0, 0:1, :]
    scale1 = 1.0 + mod_ref[0, 1:2, :]

    def body(i, carry):
        r0 = pl.multiple_of(i * LN_ROWS, LN_ROWS)
        hn = _layer_norm_rows(x_ref[pl.ds(r0, LN_ROWS), :])
        h_ref[pl.ds(r0, LN_ROWS), :] = (hn * scale1 + shift).astype(BF16)
        return carry

    lax.fori_loop(0, x_ref.shape[0] // LN_ROWS, body, 0, unroll=LN_UNROLL)


def _ln_mod_call(x2, mod, rows_per_mod, name):
    m, d = x2.shape
    tm = min(LN_TM, m)
    per = rows_per_mod // tm
    return pl.pallas_call(
        _ln_mod_kernel,
        grid=(m // tm,),
        in_specs=[
            pl.BlockSpec((tm, d), lambda i: (i, 0)),
            pl.BlockSpec((1, 2, d), lambda i: (i // per, 0, 0)),
        ],
        out_specs=pl.BlockSpec((tm, d), lambda i: (i, 0)),
        out_shape=jax.ShapeDtypeStruct((m, d), BF16),
        compiler_params=_params("parallel"),
        name=name,
    )(x2, mod)


def _matmul_kernel(h_ref, *refs, shift):
    *w_refs, o_ref, wb_ref = refs

    @pl.when(pl.program_id(1) == 0)
    def _():
        tn = wb_ref.shape[1]
        step_rows = math.gcd(tn, PROJ_WT_ROWS)
        for c0 in range(0, tn, step_rows):
            lo, hi = c0 + shift, c0 + shift + step_rows
            pieces = []
            if lo < tn:
                pieces.append(w_refs[0][0, lo:min(hi, tn), :])
            if hi > tn:
                pieces.append(w_refs[1][0, max(lo, tn) - tn:hi - tn, :])
            rows_f32 = pieces[0] if len(pieces) == 1 else jnp.concatenate(pieces, axis=0)
            wb_ref[:, c0:c0 + step_rows] = jnp.transpose(rows_f32).astype(BF16)

    o_ref[...] = _dot(h_ref[...], wb_ref[...]).astype(o_ref.dtype)


def _proj_call(h2, w_t, layer, col0, ncols, tn, out_dtype, name):
    m, d = h2.shape
    tm = min(PROJ_TM * (2 if out_dtype == BF16 and col0 % tn == 0 else 1), m)
    blk0, shift = divmod(col0, tn)
    nblk = ncols // tn
    assert ncols % tn == 0 and col0 + ncols <= w_t.shape[1] and shift % SUBLANES == 0
    w_spec = lambda extra: pl.BlockSpec((1, tn, d), lambda j, i: (layer, blk0 + extra + j, 0))
    w_specs = [w_spec(0), w_spec(1)] if shift else [w_spec(0)]
    return pl.pallas_call(
        functools.partial(_matmul_kernel, shift=shift),
        grid=(nblk, m // tm),
        in_specs=[pl.BlockSpec((tm, d), lambda j, i: (i, 0))] + w_specs,
        out_specs=pl.BlockSpec((tm, tn), lambda j, i: (i, j)),
        out_shape=jax.ShapeDtypeStruct((m, ncols), out_dtype),
        scratch_shapes=[pltpu.VMEM((d, tn), BF16)],
        compiler_params=_params("arbitrary", "arbitrary"),
        name=name,
    )(h2, *([w_t] * len(w_specs)))


def _softmax_pv(q, lane_lo, pieces):
    outs = []
    for h in range(2):
        keep = lane_lo if h == 0 else jnp.logical_not(lane_lo)
        qh = jnp.where(keep, q, jnp.zeros_like(q)) * jnp.asarray(NA_HEAD_DIM ** -0.5, BF16)
        scores = []
        for k, _, bias in pieces:
            s = _dot_nt(qh, k)
            if bias is not None:
                s = s + bias[h]
            scores.append(s)
        m = scores[0].max(axis=-1, keepdims=True)
        for s in scores[1:]:
            m = jnp.maximum(m, s.max(axis=-1, keepdims=True))
        den = None
        acc = None
        for s, (_, v, _) in zip(scores, pieces):
            p = jnp.exp(s - m)
            ps = p.sum(axis=-1, keepdims=True)
            pv = _dot(p.astype(BF16), v)
            den = ps if den is None else den + ps
            acc = pv if acc is None else acc + pv
        outs.append(acc / den)
    return jnp.where(lane_lo, outs[0], outs[1])


def _na_kernel(q_ref, k_ref, v_ref, z_ref, kc_ref, vc_ref, comp_ref, o_ref, vaug_ref, bias_ref, *, rows):
    tq = q_ref.shape[1]
    kh = min(NA_KH_MAX, rows)
    nkeys = kh * GRID_W
    span = vaug_ref.shape[1]
    qi = pl.program_id(2)

    @pl.when((pl.program_id(1) == 0) & (qi == 0))
    def _():
        for h in range(2):
            comp = comp_ref[0, h]
            for d0 in range(bias_ref.shape[0]):
                bias_ref[d0, h] = comp[:, d0 * GRID_W:d0 * GRID_W + nkeys]

    lane_lo = lax.broadcasted_iota(jnp.int32, (tq, LANES), 1) < NA_HEAD_DIM
    q = q_ref[0]
    kc = kc_ref[0]
    vc = vc_ref[0]
    span_row = jnp.clip(qi * ATT_TQ_ROWS - kh // 2, 0, rows - span // GRID_W)
    starts, offs, d0s = [], [], []
    for i in range(ATT_TQ_ROWS):
        r = qi * ATT_TQ_ROWS + i
        rs = jnp.clip(r - kh // 2, 0, rows - kh)
        d0s.append(rs - r + (NA_KH_MAX - 1))
        starts.append(pl.multiple_of(rs * GRID_W, GRID_W))
        offs.append(pl.multiple_of((rs - span_row) * GRID_W, GRID_W))
    v_span = v_ref[0, pl.ds(pl.multiple_of(span_row * GRID_W, GRID_W), span), :]
    one = jnp.ones((), BF16)

    def head_lanes(shape, h):
        lo = lax.broadcasted_iota(jnp.int32, shape, 1) < NA_HEAD_DIM
        return lo if h == 0 else jnp.logical_not(lo)

    qhs, vcs = [], []
    for h in range(2):
        keep = lane_lo if h == 0 else jnp.logical_not(lane_lo)
        qf = jnp.where(keep, q, jnp.zeros_like(q)).astype(F32) * (NA_HEAD_DIM ** -0.5 * LOG2E)
        qhs.append(qf.astype(BF16))
        vaug_ref[h] = jnp.where(head_lanes(v_span.shape, h), v_span, one)
        vcs.append(jnp.where(head_lanes(vc.shape, h), vc, one))

    def stage_scores(h, rows_i):
        qs = qhs[h][rows_i[0] * GRID_W:(rows_i[-1] + 1) * GRID_W]
        s_loc = jnp.concatenate(
            [_dot_nt(qhs[h][i * GRID_W:(i + 1) * GRID_W], k_ref[0, pl.ds(starts[i], nkeys), :])
             + bias_ref[d0s[i], h] for i in rows_i], axis=0)
        return h, s_loc, _dot_nt(qs, kc)

    def stage_softmax(h, s_loc, s_ctx):
        m = jnp.maximum(s_loc.max(axis=-1, keepdims=True), s_ctx.max(axis=-1, keepdims=True))
        return h, jnp.exp2((s_loc - m).astype(BF16)), jnp.exp2((s_ctx - m).astype(BF16))

    def stage_values(h, p_loc, p_ctx, rows_i):
        acc = jnp.concatenate(
            [_dot(p_loc[j * GRID_W:(j + 1) * GRID_W], vaug_ref[h, pl.ds(offs[i], nkeys), :])
             for j, i in enumerate(rows_i)], axis=0)
        acc = acc + _dot(p_ctx, vcs[h])
        return acc / pltpu.roll(acc, NA_HEAD_DIM, 1)

    groups = [list(range(g, g + ATT_ITEM_ROWS)) for g in range(0, ATT_TQ_ROWS, ATT_ITEM_ROWS)]
    items = [(h, g) for h in range(2) for g in groups]
    scores, probs, outs = {}, {}, {}
    sk = ATT_SKEW
    for t in range(len(items) + 2 * sk):
        if t < len(items):
            scores[t] = stage_scores(*items[t])
        if 0 <= t - sk < len(items):
            probs[t - sk] = stage_softmax(*scores.pop(t - sk))
        if 0 <= t - 2 * sk < len(items):
            outs[t - 2 * sk] = stage_values(*probs.pop(t - 2 * sk), items[t - 2 * sk][1])
    per_head = [jnp.concatenate([outs[t] for t, (hh, _) in enumerate(items) if hh == h], axis=0)
                for h in range(2)]
    o = jnp.where(lane_lo, per_head[0], per_head[1])
    o_ref[0] = (o * _silu(z_ref[0].astype(F32))).astype(o_ref.dtype)


def _na_bias_table(rpb, rows):
    depth, heads = rpb.shape[:2]
    kh = min(NA_KH_MAX, rows)
    nd = 2 * NA_KH_MAX - kh
    for r in range(rows):
        assert 0 <= int(np.clip(r - kh // 2, 0, rows - kh)) - r + (NA_KH_MAX - 1) < nd

    c = np.arange(GRID_W)[:, None]
    kc = np.arange(GRID_W)[None, :]
    cs = np.clip(c - NA_KW // 2, 0, GRID_W - NA_KW)
    valid_c = (kc >= cs) & (kc < cs + NA_KW)
    dcol = kc - c + (NA_KW - 1)
    sel_col = ((dcol[None] == np.arange(2 * NA_KW - 1)[:, None, None]) & valid_c[None]).astype(np.float32)
    comp = jnp.einsum("lhde,eck->lhcdk", rpb, sel_col, precision=lax.Precision.HIGHEST)
    comp = jnp.where(valid_c[:, None, :], comp * LOG2E, NEG_BIG)
    ncomp = (2 * NA_KH_MAX - 1) * GRID_W
    comp = comp.reshape(depth, heads // 2, 2, GRID_W, ncomp)
    return jnp.pad(comp, ((0, 0),) * 4 + ((0, -ncomp % LANES),), constant_values=NEG_BIG)


def _na_call(p_lat, p_ctx, bias, d_a):
    b, l, _ = p_lat.shape
    lc = p_ctx.shape[1]
    rows = l // GRID_W
    assert rows % ATT_TQ_ROWS == 0
    tq = ATT_TQ_ROWS * GRID_W
    nq = l // tq
    npair = d_a // LANES
    kh = min(NA_KH_MAX, rows)
    nd = 2 * NA_KH_MAX - kh
    ncomp = bias.shape[-1]
    span_rows = min(ATT_TQ_ROWS + kh - 1, rows)

    return pl.pallas_call(
        functools.partial(_na_kernel, rows=rows),
        grid=(npair, b, nq),
        in_specs=[
            pl.BlockSpec((1, tq, LANES), lambda p, i, q: (i, q, p)),
            pl.BlockSpec((1, l, LANES), lambda p, i, q: (i, 0, npair + p)),
            pl.BlockSpec((1, l, LANES), lambda p, i, q: (i, 0, 2 * npair + p)),
            pl.BlockSpec((1, tq, LANES), lambda p, i, q: (i, q, 3 * npair + p)),
            pl.BlockSpec((1, lc, LANES), lambda p, i, q: (i, 0, npair + p)),
            pl.BlockSpec((1, lc, LANES), lambda p, i, q: (i, 0, 2 * npair + p)),
            pl.BlockSpec((1, 2, GRID_W, ncomp), lambda p, i, q: (p, 0, 0, 0)),
        ],
        out_specs=pl.BlockSpec((1, tq, LANES), lambda p, i, q: (i, q, p)),
        out_shape=jax.ShapeDtypeStruct((b, l, d_a), BF16),
        scratch_shapes=[pltpu.VMEM((2, span_rows * GRID_W, LANES), BF16),
                        pltpu.VMEM((nd, 2, GRID_W, kh * GRID_W), F32)],
        compiler_params=_params("arbitrary", "arbitrary", "arbitrary"),
        name="na_attention",
    )(p_lat, p_lat, p_lat, p_lat, p_ctx, p_ctx, bias)


def _ctx_attn_kernel(q_ref, k_ref, v_ref, z_ref, o_ref):
    tq = q_ref.shape[1]
    lane_lo = lax.broadcasted_iota(jnp.int32, (tq, LANES), 1) < NA_HEAD_DIM
    o = _softmax_pv(q_ref[0], lane_lo, [(k_ref[0], v_ref[0], None)])
    o_ref[0] = (o * _silu(z_ref[0].astype(F32))).astype(o_ref.dtype)


def _ctx_attn_call(p_ctx, d_a):
    b, lc, _ = p_ctx.shape
    npair = d_a // LANES
    return pl.pallas_call(
        _ctx_attn_kernel,
        grid=(b, npair),
        in_specs=[
            pl.BlockSpec((1, lc, LANES), lambda i, p: (i, 0, p)),
            pl.BlockSpec((1, lc, LANES), lambda i, p: (i, 0, npair + p)),
            pl.BlockSpec((1, lc, LANES), lambda i, p: (i, 0, 2 * npair + p)),
            pl.BlockSpec((1, lc, LANES), lambda i, p: (i, 0, 3 * npair + p)),
        ],
        out_specs=pl.BlockSpec((1, lc, LANES), lambda i, p: (i, 0, p)),
        out_shape=jax.ShapeDtypeStruct((b, lc, d_a), BF16),
        compiler_params=_params("parallel", "parallel"),
        name="ctx_attention",
    )(p_ctx, p_ctx, p_ctx, p_ctx)


def _rope(x, cos, sin_signed):
    halves = []
    for j in range(x.shape[1] // LANES):
        xs = x[:, j * LANES:(j + 1) * LANES]
        lane = lax.broadcasted_iota(jnp.int32, xs.shape, 1)
        first = (lane & 16) == 0
        partner = jnp.where(first, pltpu.roll(xs, LANES - 16, 1), pltpu.roll(xs, 16, 1))
        halves.append(partner)
    partner = jnp.concatenate(halves, axis=1)
    return x * cos + partner * sin_signed


def _gla_kernel(q_ref, k_ref, v_ref, lr_ref, cos_ref, sin_ref, w2_ref, b2_ref, s0_ref,
                o_ref, sfin_ref, s_ref, *, reverse, rope):
    tg = q_ref.shape[1]
    dkh = q_ref.shape[2]
    dvh = v_ref.shape[2]
    dk = dkh // GLA_HEADS
    dv = dvh // GLA_HEADS
    nchunk = tg // GLA_CHUNK
    step = pl.program_id(1)

    @pl.when(step == 0)
    def _():
        s_ref[...] = s0_ref[0]

    lr_hi, lr_lo = _split_hi_lo(lr_ref[0])
    w_hi, w_lo = _split_hi_lo(w2_ref[...])
    logits = _dot(lr_hi, w_hi) + _dot(lr_lo, w_hi) + _dot(lr_hi, w_lo) + b2_ref[...]
    g = (jnp.minimum(logits, 0.0) - jnp.log1p(jnp.exp(-jnp.abs(logits)))) * (1.0 / GATE_TAU)

    tcs = min(tg, GLA_CUMSUM_ROWS)
    r = lax.broadcasted_iota(jnp.int32, (tcs, tcs), 0)
    c = lax.broadcasted_iota(jnp.int32, (tcs, tcs), 1)
    same = _shr(r, GLA_CHUNK) == _shr(c, GLA_CHUNK)
    upto = (c >= r) if reverse else (c <= r)
    t_inc = jnp.where(same & upto, 1.0, 0.0).astype(BF16)
    g_hi, g_lo = _split_hi_lo(g)
    b_inc = jnp.concatenate(
        [_dot(t_inc, g_hi[s0:s0 + tcs]) + _dot(t_inc, g_lo[s0:s0 + tcs]) for s0 in range(0, tg, tcs)], axis=0)

    chunks = [slice(ci * GLA_CHUNK, (ci + 1) * GLA_CHUNK) for ci in range(nchunk)]
    end_row = 0 if reverse else GLA_CHUNK - 1
    b_end = [b_inc[sl][end_row:end_row + 1, :] for sl in chunks]
    b_rest = jnp.concatenate([be - b_inc[sl] for be, sl in zip(b_end, chunks)], axis=0)

    q = q_ref[0]
    k = k_ref[0]
    if rope:
        q = _rope(q, cos_ref[...], sin_ref[...])
        k = _rope(k, cos_ref[...], sin_ref[...])
    q = q * (dk ** -0.5)
    qt = (q * jnp.exp(b_inc)).astype(BF16)
    kt = (k * jnp.exp(-b_inc)).astype(BF16)
    kd = (k * jnp.exp(b_rest)).astype(BF16)
    vb = v_ref[0].astype(BF16)

    lane_head = _shr(lax.broadcasted_iota(jnp.int32, (GLA_CHUNK, dkh), 1), dk)
    hc = GLA_HEADS * GLA_CHUNK
    rk = _shr(lax.broadcasted_iota(jnp.int32, (hc, dkh), 0), GLA_CHUNK)
    ck = _shr(lax.broadcasted_iota(jnp.int32, (hc, dkh), 1), dk)
    mask_k = rk == ck
    rv = _shr(lax.broadcasted_iota(jnp.int32, (hc, dvh), 0), GLA_CHUNK)
    cv = _shr(lax.broadcasted_iota(jnp.int32, (hc, dvh), 1), dv)
    mask_v = rv == cv
    ti = lax.broadcasted_iota(jnp.int32, (GLA_CHUNK, dkh), 0)
    tj = lax.broadcasted_iota(jnp.int32, (GLA_CHUNK, dkh), 1) & (GLA_CHUNK - 1)
    causal = (tj >= ti) if reverse else (tj <= ti)
    zero = jnp.zeros((), BF16)

    def head_stack(x):
        return jnp.concatenate([jnp.where(lane_head == h, x, zero) for h in range(GLA_HEADS)], axis=0)

    o_intra, kv, decay, q_stack = [], [], [], []
    for sl, be in zip(chunks, b_end):
        k_blk = jnp.where(mask_k, jnp.concatenate([kt[sl]] * GLA_HEADS, axis=0), zero)
        att = jnp.where(causal, _dot_nt(qt[sl], k_blk), 0.0)
        v_blk = jnp.where(mask_v, jnp.concatenate([vb[sl]] * GLA_HEADS, axis=0), zero)
        o_intra.append(_dot(att.astype(BF16), v_blk))
        v_rows = jnp.concatenate([vb[sl][:, h * dv:(h + 1) * dv] for h in range(GLA_HEADS)], axis=0)
        kv.append(_dot_tn(v_rows, head_stack(kd[sl])))
        decay.append(jnp.exp(be))
        q_stack.append(head_stack(qt[sl]))

    order = list(range(nchunk - 1, -1, -1)) if reverse else list(range(nchunk))
    state = s_ref[...]
    states = {}
    for ci in order:
        states[ci] = state
        state = decay[ci] * state + kv[ci]
    s_ref[...] = state

    for ci in range(nchunk):
        inter = _dot_nt(q_stack[ci], states[ci].astype(BF16))
        inter = jnp.concatenate([inter[h * GLA_CHUNK:(h + 1) * GLA_CHUNK] for h in range(GLA_HEADS)], axis=1)
        o_ref[0, chunks[ci], :] = o_intra[ci] + inter

    @pl.when(step == pl.num_programs(1) - 1)
    def _():
        sfin_ref[0] = state


def _gla_call(p_b, cos, sin_signed, w2p, b2, s0, *, reverse, rope, dkh, dvh, name):
    b, t, _ = p_b.shape
    tg = min(GLA_TG, t)
    ng = t // tg
    lr_blk = (2 * dkh + 2 * dvh) // LANES
    dv = dvh // GLA_HEADS
    assert dv % LANES == 0

    def blk(i):
        return ng - 1 - i if reverse else i

    return pl.pallas_call(
        functools.partial(_gla_kernel, reverse=reverse, rope=rope),
        grid=(b, ng),
        in_specs=[
            pl.BlockSpec((1, tg, dkh), lambda n, i: (n, blk(i), 0)),
            pl.BlockSpec((1, tg, dkh), lambda n, i: (n, blk(i), 1)),
            pl.BlockSpec((1, tg, dvh), lambda n, i: (n, blk(i), 2 * dkh // dvh)),
            pl.BlockSpec((1, tg, LANES), lambda n, i: (n, blk(i), lr_blk)),
            pl.BlockSpec((tg, dkh), lambda n, i: (blk(i), 0)),
            pl.BlockSpec((tg, dkh), lambda n, i: (blk(i), 0)),
            pl.BlockSpec((LANES, dkh), lambda n, i: (0, 0)),
            pl.BlockSpec((1, dkh), lambda n, i: (0, 0)),
            pl.BlockSpec((1, dv, dkh), lambda n, i: (n, 0, 0)),
        ],
        out_specs=[
            pl.BlockSpec((1, tg, dvh), lambda n, i: (n, blk(i), 0)),
            pl.BlockSpec((1, dv, dkh), lambda n, i: (n, 0, 0)),
        ],
        out_shape=[
            jax.ShapeDtypeStruct((b, t, dvh), F32),
            jax.ShapeDtypeStruct((b, dv, dkh), F32),
        ],
        scratch_shapes=[pltpu.VMEM((dv, dkh), F32)],
        compiler_params=_params("parallel", "arbitrary"),
        name=name,
    )(p_b, p_b, p_b, p_b, cos, sin_signed, w2p, b2, s0)


def _rope_tables(l, dk):
    quarter = dk // 4
    pos = jnp.arange(l)
    rows_pos = (pos // GRID_W).astype(F32)
    cols_pos = (pos % GRID_W).astype(F32)
    inv = ROPE_BASE ** (-jnp.arange(0, 2 * quarter, 2, dtype=F32) / (2 * quarter))
    ang_r = rows_pos[:, None] * inv[None, :]
    ang_c = cols_pos[:, None] * inv[None, :]
    cos = jnp.concatenate([jnp.cos(ang_r)] * 2 + [jnp.cos(ang_c)] * 2, axis=1)
    sin = jnp.concatenate([-jnp.sin(ang_r), jnp.sin(ang_r), -jnp.sin(ang_c), jnp.sin(ang_c)], axis=1)
    return jnp.tile(cos, (1, GLA_HEADS)), jnp.tile(sin, (1, GLA_HEADS))


def _conv_kernel(a_ref, g_ref, z_ref, ap_ref, gp_ref, an_ref, gn_ref, w_ref, cb_ref, lg_ref, lb_ref,
                 o_ref, u_ref, us_ref):
    t = a_ref.shape[1]
    nshift = u_ref.shape[0] - SUBLANES
    step = pl.program_id(1)
    nstep = pl.num_programs(1)

    def glu(a, g):
        return a.astype(F32) * jax.nn.sigmoid(g.astype(F32))

    prev_ok = (step > 0).astype(F32)
    next_ok = (step < nstep - 1).astype(F32)
    u_ref[0:CONV_HALO, :] = glu(ap_ref[0], gp_ref[0]) * prev_ok
    u_ref[CONV_HALO:CONV_HALO + t, :] = glu(a_ref[0], g_ref[0])
    u_ref[CONV_HALO + t:2 * CONV_HALO + t, :] = glu(an_ref[0], gn_ref[0]) * next_ok

    for b in range(1, SUBLANES):
        us_ref[b - 1] = u_ref[b:b + nshift, :]

    off = CONV_HALO - CONV_K // 2
    for rc in range(t // CONV_RC):
        base = rc * CONV_RC
        acc = None
        for j in range(CONV_K):
            q8, b = divmod(off + j, SUBLANES)
            lo = base + q8 * SUBLANES
            src = u_ref[lo:lo + CONV_RC, :] if b == 0 else us_ref[b - 1, lo:lo + CONV_RC, :]
            term = src * w_ref[j:j + 1, :]
            acc = term if acc is None else acc + term
        acc = acc + cb_ref[...]
        mu = jnp.mean(acc, axis=-1, keepdims=True)
        xc = acc - mu
        var = jnp.mean(xc * xc, axis=-1, keepdims=True)
        y = xc * lax.rsqrt(var + LN_EPS) * lg_ref[...] + lb_ref[...]
        z = z_ref[0, base:base + CONV_RC, :].astype(F32)
        o_ref[0, base:base + CONV_RC, :] = (_silu(y) * _silu(z)).astype(o_ref.dtype)


def _conv_call(p, conv_w, conv_b, ln_g, ln_b, col0, d_c, name):
    b, t, _ = p.shape
    tt = min(CONV_T, t)
    nt = t // tt
    cb = col0 // d_c
    hb = tt // CONV_HALO
    nhalo = t // CONV_HALO
    w = jnp.zeros((CONV_K + 1, d_c), F32).at[:CONV_K].set(conv_w)

    def prev(i):
        return jnp.maximum(i * hb - 1, 0)

    def nxt(i):
        return jnp.minimum((i + 1) * hb, nhalo - 1)

    row = lambda v: v.reshape(1, d_c)
    return pl.pallas_call(
        _conv_kernel,
        grid=(b, nt),
        in_specs=[
            pl.BlockSpec((1, tt, d_c), lambda n, i: (n, i, cb)),
            pl.BlockSpec((1, tt, d_c), lambda n, i: (n, i, cb + 1)),
            pl.BlockSpec((1, tt, d_c), lambda n, i: (n, i, cb + 2)),
            pl.BlockSpec((1, CONV_HALO, d_c), lambda n, i: (n, prev(i), cb)),
            pl.BlockSpec((1, CONV_HALO, d_c), lambda n, i: (n, prev(i), cb + 1)),
            pl.BlockSpec((1, CONV_HALO, d_c), lambda n, i: (n, nxt(i), cb)),
            pl.BlockSpec((1, CONV_HALO, d_c), lambda n, i: (n, nxt(i), cb + 1)),
            pl.BlockSpec((CONV_K + 1, d_c), lambda n, i: (0, 0)),
            pl.BlockSpec((1, d_c), lambda n, i: (0, 0)),
            pl.BlockSpec((1, d_c), lambda n, i: (0, 0)),
            pl.BlockSpec((1, d_c), lambda n, i: (0, 0)),
        ],
        out_specs=pl.BlockSpec((1, tt, d_c), lambda n, i: (n, i, 0)),
        out_shape=jax.ShapeDtypeStruct((b, t, d_c), BF16),
        scratch_shapes=[pltpu.VMEM((tt + 2 * CONV_HALO, d_c), F32),
                        pltpu.VMEM((SUBLANES - 1, tt + 2 * CONV_HALO - SUBLANES, d_c), F32)],
        compiler_params=_params("parallel", "arbitrary"),
        name=name,
    )(p, p, p, p, p, p, p, w, row(conv_b), row(ln_g), row(ln_b))


def _out_kernel(oa_ref, of_ref, ob_ref, zb_ref, oc_ref, x_ref, gate_ref, gn_ref,
                w_ref, lg_ref, lb_ref, *rest, alpha, emit_h):
    if emit_h:
        mod_ref, o_ref, h_ref, y0_ref, y1_ref, lhs_ref = rest
    else:
        o_ref, y0_ref, y1_ref, lhs_ref = rest
    tm = x_ref.shape[0]
    d_a = oa_ref.shape[1]
    dvh = of_ref.shape[1]
    dv = dvh // GLA_HEADS
    step = pl.program_id(0)

    @pl.when(step == 0)
    def _():
        y1_ref[...] = jnp.zeros_like(y1_ref)

    def run(y_prev, y_cur):
        gate = gate_ref[0]
        if emit_h:
            shift = mod_ref[0, 0:1, :]
            scale1 = 1.0 + mod_ref[0, 1:2, :]
        for c in range(tm // OUT_LN_ROWS):
            sl = slice(c * OUT_LN_ROWS, (c + 1) * OUT_LN_ROWS)
            rv = alpha * x_ref[sl, :] + gate * y_prev[sl, :]
            xn = _layer_norm_rows(rv) * lg_ref[...] + lb_ref[...]
            o_ref[sl, :] = xn
            if emit_h:
                h_ref[sl, :] = (_layer_norm_rows(xn) * scale1 + shift).astype(BF16)

        s = of_ref[...] + ob_ref[...]
        parts = []
        for h in range(GLA_HEADS):
            sh = s[:, h * dv:(h + 1) * dv]
            ms = jnp.mean(sh * sh, axis=-1, keepdims=True)
            parts.append(sh * lax.rsqrt(ms + RMS_EPS))
        out_b = jnp.concatenate(parts, axis=1) * gn_ref[...] * _silu(zb_ref[...])
        lhs_ref[:, :d_a] = oa_ref[...]
        lhs_ref[:, d_a:d_a + dvh] = out_b.astype(BF16)
        lhs_ref[:, d_a + dvh:] = oc_ref[...]
        y_cur[...] = _dot(lhs_ref[...], w_ref[...])

    @pl.when(step % 2 == 0)
    def _():
        run(y1_ref, y0_ref)

    @pl.when(step % 2 == 1)
    def _():
        run(y0_ref, y1_ref)


def _out_call(oa, o_f, o_b, p_b, oc, x2, gate, gla_norm, w_out, ln_g, ln_b, next_mod, rows_per_gate, zb_col,
              alpha, name):
    m, d = x2.shape
    d_a, dvh, d_c = oa.shape[1], o_f.shape[1], oc.shape[1]
    tm = min(OUT_TM, m)
    per = rows_per_gate // tm
    emit_h = next_mod is not None
    w_bf = w_out.astype(BF16)
    gn = jnp.tile(gla_norm, GLA_HEADS).reshape(1, dvh)
    nt = m // tm
    cur = lambda i: jnp.minimum(i, nt - 1)
    prev = lambda i: jnp.maximum(i - 1, 0)
    full = lambda shape: pl.BlockSpec(shape, lambda i: (0,) * len(shape))
    rows = pl.BlockSpec((tm, d), lambda i: (prev(i), 0))
    in_specs = [
        pl.BlockSpec((tm, d_a), lambda i: (cur(i), 0)),
        pl.BlockSpec((tm, dvh), lambda i: (cur(i), 0)),
        pl.BlockSpec((tm, dvh), lambda i: (cur(i), 0)),
        pl.BlockSpec((tm, dvh), lambda i: (cur(i), zb_col // dvh)),
        pl.BlockSpec((tm, d_c), lambda i: (cur(i), 0)),
        rows,
        pl.BlockSpec((1, 1, d), lambda i: (prev(i) // per, 0, 0)),
        full((1, dvh)),
        full((d_a + dvh + d_c, d)),
        full((1, d)),
        full((1, d)),
    ]
    args = [oa, o_f, o_b, p_b, oc, x2, gate, gn, w_bf, ln_g.reshape(1, d), ln_b.reshape(1, d)]
    out_specs = rows
    out_shape = jax.ShapeDtypeStruct((m, d), F32)
    if emit_h:
        in_specs.append(pl.BlockSpec((1, 2, d), lambda i: (prev(i) // per, 0, 0)))
        args.append(next_mod)
        out_specs = [rows, rows]
        out_shape = [out_shape, jax.ShapeDtypeStruct((m, d), BF16)]
    return pl.pallas_call(
        functools.partial(_out_kernel, alpha=alpha, emit_h=emit_h),
        grid=(nt + 1,),
        in_specs=in_specs,
        out_specs=out_specs,
        out_shape=out_shape,
        scratch_shapes=[pltpu.VMEM((tm, d), F32), pltpu.VMEM((tm, d), F32),
                        pltpu.VMEM((tm, d_a + dvh + d_c), BF16)],
        compiler_params=_params("arbitrary"),
        name=name,
    )(*args)


def kernel(x, c, ctx, c_ctx, w_ada, b_ada, w_in, rpb, gla_w2, gla_b, gla_norm, conv_w, conv_b,
           conv_ln_g, conv_ln_b, w_out, post_ln_g, post_ln_b):
    b, l, d = x.shape
    lc = ctx.shape[1]
    depth = w_ada.shape[0]
    heads_a = rpb.shape[1]
    d_a = heads_a * NA_HEAD_DIM
    dkh = gla_w2.shape[-1]
    dvh = gla_norm.shape[-1] * GLA_HEADS
    d_c = conv_w.shape[-1]
    n_a = 4 * d_a
    n_b = 2 * dkh + 2 * dvh + 2 * GATE_RANK
    n_c = 3 * d_c
    alpha = (2 * depth) ** 0.25
    assert w_in.shape[-1] == n_a + n_b + n_c and d_a + dvh + d_c == w_out.shape[1]
    tn_c = n_c // 2
    tn_b = PROJ_TN
    n_bp = -(-n_b // tn_b) * tn_b
    assert n_a + n_bp <= w_in.shape[-1] and tn_c % LANES == 0

    cond = jnp.zeros((8, d), F32).at[:b].set(c).at[b].set(c_ctx)
    mod = _ada_call(cond, w_ada, b_ada)

    na_bias = _na_bias_table(rpb, l // GRID_W)
    w_in_t = jnp.swapaxes(w_in, 1, 2)

    cos, sin_signed = _rope_tables(l, dkh // GLA_HEADS)
    ones_c = jnp.ones((lc, dkh), F32)
    zeros_c = jnp.zeros((lc, dkh), F32)

    def layer_mods(layer):
        shift, scale, gate = jnp.split(mod[layer], 3, axis=-1)
        mod_lat = jnp.stack([shift[:b], scale[:b]], axis=1)
        mod_ctx = jnp.stack([shift[b:b + 1], scale[b:b + 1]], axis=1)
        return mod_lat, mod_ctx, gate[:b].reshape(b, 1, d), gate[b:b + 1].reshape(1, 1, d)

    x2 = x.reshape(b * l, d)
    cx2 = ctx.reshape(b * lc, d)
    mod_lat, mod_ctx, gate_lat, gate_ctx = layer_mods(0)
    h2 = _ln_mod_call(x2, mod_lat, l, "ln_mod")
    hc2 = _ln_mod_call(cx2, mod_ctx, b * lc, "ln_mod_ctx")
    for layer in range(depth):
        last = layer == depth - 1
        if not last:
            next_lat, next_ctx, next_gate_lat, next_gate_ctx = layer_mods(layer + 1)
        else:
            next_lat = next_ctx = None

        def project(hh, rows_n, tag):
            pa = _proj_call(hh, w_in_t, layer, 0, n_a, PROJ_TN, BF16, "proj_a" + tag)
            pb = _proj_call(hh, w_in_t, layer, n_a, n_bp, tn_b, F32, "proj_b" + tag)
            pc = _proj_call(hh, w_in_t, layer, n_a + n_b, n_c, tn_c, BF16, "proj_c" + tag)
            return pa.reshape(b, rows_n, n_a), pb.reshape(b, rows_n, n_bp), pc.reshape(b, rows_n, n_c)

        p_a, p_b, p_c = project(h2, l, "")
        pc_a, pc_b, pc_c = project(hc2, lc, "_ctx")

        out_a = _na_call(p_a, pc_a, na_bias[layer], d_a)

        w2p = [jnp.zeros((LANES, dkh), F32).at[i * GATE_RANK:(i + 1) * GATE_RANK].set(gla_w2[layer, i])
               for i in range(2)]
        b2 = [gla_b[layer, i].reshape(1, dkh) for i in range(2)]
        s0 = jnp.zeros((b, dvh // GLA_HEADS, dkh), F32)
        gla = functools.partial(_gla_call, dkh=dkh, dvh=dvh)
        oc_f, s_f = gla(pc_b, ones_c, zeros_c, w2p[0], b2[0], s0, reverse=False, rope=False, name="gla_ctx_fwd")
        oc_b, s_b = gla(pc_b, ones_c, zeros_c, w2p[1], b2[1], s0, reverse=True, rope=False, name="gla_ctx_bwd")
        o_f, _ = gla(p_b, cos, sin_signed, w2p[0], b2[0], s_f, reverse=False, rope=True, name="gla_fwd")
        o_b, _ = gla(p_b, cos, sin_signed, w2p[1], b2[1], s_b, reverse=True, rope=True, name="gla_bwd")

        out_c = _conv_call(p_c, conv_w[layer], conv_b[layer], conv_ln_g[layer], conv_ln_b[layer],
                           0, d_c, "conv")

        zb_col = 2 * dkh + dvh
        res = _out_call(out_a.reshape(b * l, d_a), o_f.reshape(b * l, dvh), o_b.reshape(b * l, dvh),
                        p_b.reshape(b * l, n_bp), out_c.reshape(b * l, d_c), x2, gate_lat,
                        gla_norm[layer], w_out[layer], post_ln_g[layer], post_ln_b[layer], next_lat,
                        l, zb_col, alpha, "out_proj")
        if last:
            x2 = res
        else:
            out_a_c = _ctx_attn_call(pc_a, d_a)
            out_c_c = _conv_call(pc_c, conv_w[layer], conv_b[layer], conv_ln_g[layer], conv_ln_b[layer],
                                 0, d_c, "conv_ctx")
            cx2, hc2 = _out_call(out_a_c.reshape(b * lc, d_a), oc_f.reshape(b * lc, dvh),
                                 oc_b.reshape(b * lc, dvh), pc_b.reshape(b * lc, n_bp),
                                 out_c_c.reshape(b * lc, d_c), cx2, gate_ctx, gla_norm[layer], w_out[layer],
                                 post_ln_g[layer], post_ln_b[layer], next_ctx, b * lc, zb_col, alpha,
                                 "out_proj_ctx")
            x2, h2 = res
            gate_lat, gate_ctx = next_gate_lat, next_gate_ctx
    return x2.reshape(b, l, d)
```
